```python
import math
import jax, jax.numpy as jnp
from jax import lax
import numpy as np

D_MODEL = 1024
BATCH = 4
SEQ = 8192
DEPTH = 2

HEAD_DIM = 64
SB_HEADS = D_MODEL // (4 * HEAD_DIM)
SWA_HEADS = D_MODEL // (2 * HEAD_DIM)
SWA_KV_HEADS = SWA_HEADS // 4
ML_HEADS = D_MODEL // (4 * HEAD_DIM)
SB_WIDTH = SB_HEADS * HEAD_DIM
SWA_WIDTH = SWA_HEADS * HEAD_DIM
SWA_KV_WIDTH = SWA_KV_HEADS * HEAD_DIM
ML_WIDTH = ML_HEADS * HEAD_DIM
MIX_WIDTH = SB_WIDTH + SWA_WIDTH + ML_WIDTH
IN_SPLITS = (SB_WIDTH, SB_WIDTH, SB_WIDTH, SWA_WIDTH, SWA_KV_WIDTH, SWA_KV_WIDTH,
             2 * ML_WIDTH, ML_WIDTH, ML_HEADS, ML_HEADS, ML_WIDTH)
N_IN = sum(IN_SPLITS)
BLOCK = 128
WINDOW = 128
ML_CHUNK = 128
CONV_WIDTH = 4
NUM_BUCKETS = 32
MAX_DISTANCE = 128
MEM_TOKENS = 256
XATTN_HEADS = 4
XATTN_HEAD_DIM = D_MODEL // XATTN_HEADS
D_FF = ((8 * D_MODEL // 3 + 255) // 256) * 256
ALPHA = (2 * DEPTH) ** 0.25
BETA = (8 * DEPTH) ** -0.25
LN_EPS = 1e-5

kernel_name = "hybrid_sb_swa_mlstm_macaron_deepnorm"


def layer_norm(x, g, b):
    xf = x.astype(jnp.float32)
    mu = xf.mean(-1, keepdims=True)
    var = jnp.square(xf - mu).mean(-1, keepdims=True)
    return ((xf - mu) * lax.rsqrt(var + LN_EPS) * g + b).astype(x.dtype)


def swiglu(x, w_in, w_out):
    a, b = jnp.split(x @ w_in, 2, axis=-1)
    return (jax.nn.silu(a) * b) @ w_out


def t5_bucket(dist):
    max_exact = NUM_BUCKETS // 2
    d = np.maximum(dist, 1)
    large = max_exact + (np.log(d / max_exact) / np.log(MAX_DISTANCE / max_exact)
                         * (NUM_BUCKETS - max_exact)).astype(np.int32)
    large = np.minimum(large, NUM_BUCKETS - 1)
    return np.where(dist < max_exact, dist, large).astype(np.int32)


def causal_conv(x, w, b):
    out = lax.conv_general_dilated(x, w[:, None, :], window_strides=(1,),
                                   padding=[(CONV_WIDTH - 1, 0)],
                                   dimension_numbers=('NWC', 'WIO', 'NWC'),
                                   feature_group_count=x.shape[-1])
    return out + b


def stick_breaking(q, k, v):
    bsz, s_len, h, d = q.shape
    nb = s_len // BLOCK
    kf = k.astype(jnp.float32).transpose(0, 2, 1, 3)
    vf = v.astype(jnp.float32).transpose(0, 2, 1, 3)
    qb = (q.astype(jnp.float32) * d ** -0.5).reshape(bsz, nb, BLOCK, h, d).transpose(1, 0, 3, 2, 4)
    key_pos = jnp.arange(s_len)

    def block(args):
        q_blk, blk = args
        t = blk * BLOCK + jnp.arange(BLOCK)
        causal = key_pos[None, :] < t[:, None]
        z = jnp.einsum('bhqd,bhsd->bhqs', q_blk, kf)
        log_one_minus = jnp.where(causal, jax.nn.log_sigmoid(-z), 0.0)
        after = lax.cumsum(log_one_minus, axis=3, reverse=True) - log_one_minus
        weight = jnp.where(causal, jnp.exp(jax.nn.log_sigmoid(z) + after), 0.0)
        return jnp.einsum('bhqs,bhsd->bhqd', weight, vf)

    out = lax.map(block, (qb, jnp.arange(nb)))
    return out.transpose(1, 0, 3, 2, 4).reshape(bsz, s_len, h * d).astype(q.dtype)


def sliding_window_attention(q, k, v, sinks, rel_bias):
    bsz, s_len, h, d = q.shape
    hkv = k.shape[2]
    g = h // hkv
    nb = s_len // BLOCK
    qf = (q.astype(jnp.float32) * d ** -0.5).reshape(bsz, nb, BLOCK, hkv, g, d)

    def band(t):
        tb = t.astype(jnp.float32).reshape(bsz, nb, BLOCK, hkv, d)
        prev = jnp.concatenate([jnp.zeros_like(tb[:, :1]), tb[:, :-1]], axis=1)
        return jnp.concatenate([prev, tb], axis=2)

    kb, vb = band(k), band(v)
    qi = np.arange(BLOCK)[:, None]
    kj = np.arange(2 * BLOCK)[None, :]
    dist = qi + BLOCK - kj
    in_window = (dist >= 0) & (dist < WINDOW)
    bucket = t5_bucket(np.clip(dist, 0, None))
    bias = rel_bias.astype(jnp.float32)[bucket]
    bias = bias.transpose(2, 0, 1).reshape(hkv, g, BLOCK, 2 * BLOCK)
    key_valid = (np.arange(nb)[:, None] * BLOCK + kj - BLOCK) >= 0
    mask = in_window[None] & key_valid[:, None, :]
    logits = jnp.einsum('bnqkgd,bnskd->bnkgqs', qf, kb) + bias
    logits = jnp.where(mask[None, :, None, None], logits, -jnp.inf)
    sink = sinks.astype(jnp.float32).reshape(hkv, g)[:, :, None, None]
    m = jnp.maximum(logits.max(-1, keepdims=True), sink)
    p = jnp.exp(logits - m)
    denom = p.sum(-1, keepdims=True) + jnp.exp(sink - m)
    out = jnp.einsum('bnkgqs,bnskd->bnqkgd', p / denom, vb)
    return out.reshape(bsz, s_len, h * d).astype(q.dtype)


def mlstm(q, k, v, i_raw, f_raw, norm_g):
    bsz, s_len, h, d = q.shape
    L = ML_CHUNK
    nc = s_len // L

    def chunks(t):
        return t.astype(jnp.float32).reshape(bsz, nc, L, h, d).transpose(1, 0, 3, 2, 4)

    def chunks_gate(t):
        return t.astype(jnp.float32).reshape(bsz, nc, L, h).transpose(1, 0, 3, 2)

    qc, kc, vc = chunks(q), chunks(k) * d ** -0.5, chunks(v)
    ic = chunks_gate(i_raw)
    b = jnp.cumsum(jax.nn.log_sigmoid(chunks_gate(f_raw)), axis=-1)

    def step(carry, inp):
        C, n, m = carry
        k_c, v_c, i_c, b_c = inp
        b_end = b_c[..., -1]
        w_log = b_end[..., None] - b_c + i_c
        m_new = jnp.maximum(b_end + m, w_log.max(-1))
        decay = jnp.exp(b_end + m - m_new)
        w = jnp.exp(w_log - m_new[..., None])
        C_new = decay[..., None, None] * C + jnp.einsum('bhl,bhlk,bhlv->bhkv', w, k_c, v_c)
        n_new = decay[..., None] * n + jnp.einsum('bhl,bhlk->bhk', w, k_c)
        return (C_new, n_new, m_new), (C, n, m)

    init = (jnp.zeros((bsz, h, d, d), jnp.float32), jnp.zeros((bsz, h, d), jnp.float32),
            jnp.zeros((bsz, h), jnp.float32))
    _, (C_prev, n_prev, m_prev) = lax.scan(step, init, (kc, vc, ic, b))

    g = b + m_prev[..., None]
    causal = np.tril(np.ones((L, L), dtype=bool))
    D = jnp.where(causal, b[..., :, None] - b[..., None, :] + ic[..., None, :], -jnp.inf)
    m_t = jnp.maximum(g, D.max(-1))
    s = jnp.einsum('cbhld,cbhsd->cbhls', qc, kc) * jnp.exp(D - m_t[..., None])
    inter = jnp.exp(g - m_t)
    num = inter[..., None] * jnp.einsum('cbhld,cbhdv->cbhlv', qc, C_prev) \
        + jnp.einsum('cbhls,cbhsv->cbhlv', s, vc)
    den = inter * jnp.einsum('cbhld,cbhd->cbhl', qc, n_prev) + s.sum(-1)
    hid = num / jnp.maximum(jnp.abs(den), jnp.exp(-m_t))[..., None]
    mu = hid.mean(-1, keepdims=True)
    var = jnp.square(hid - mu).mean(-1, keepdims=True)
    hid = (hid - mu) * lax.rsqrt(var + LN_EPS)
    hid = hid.transpose(1, 0, 3, 2, 4).reshape(bsz, s_len, h * d).astype(q.dtype)
    return hid * norm_g


def hybrid_mixer(x, w_in, conv_w, conv_b, i_bias, f_bias, ml_norm_g, sinks, rel_bias, w_out):
    bsz, s_len, _ = x.shape
    idx = [int(i) for i in np.cumsum(IN_SPLITS)[:-1]]
    (q_sb, k_sb, v_sb, q_sw, k_sw, v_sw, qk_ml, v_ml, i_ml, f_ml, o_ml) = jnp.split(x @ w_in, idx, axis=-1)

    def heads(t):
        return t.reshape(bsz, s_len, -1, HEAD_DIM)

    y_sb = stick_breaking(heads(q_sb), heads(k_sb), heads(v_sb))
    y_sw = sliding_window_attention(heads(q_sw), heads(k_sw), heads(v_sw), sinks, rel_bias)
    q_ml, k_ml = jnp.split(jax.nn.silu(causal_conv(qk_ml, conv_w, conv_b)), 2, axis=-1)
    y_ml = jax.nn.sigmoid(o_ml) * mlstm(heads(q_ml), heads(k_ml), heads(v_ml),
                                        i_ml + i_bias, f_ml + f_bias, ml_norm_g)
    y = jnp.concatenate([y_sb, y_sw, y_ml.astype(y_sb.dtype)], axis=-1)
    return y @ w_out


def cross_attention(x, mem, w_q, w_kv, w_o):
    bsz, s_len, _ = x.shape
    q = (x @ w_q).reshape(bsz, s_len, XATTN_HEADS, XATTN_HEAD_DIM).astype(jnp.float32)
    k, v = jnp.split(mem @ w_kv, 2, axis=-1)
    k = k.reshape(bsz, -1, XATTN_HEADS, XATTN_HEAD_DIM).astype(jnp.float32)
    v = v.reshape(bsz, -1, XATTN_HEADS, XATTN_HEAD_DIM).astype(jnp.float32)
    p = jax.nn.softmax(jnp.einsum('bshd,bmhd->bhsm', q, k) * XATTN_HEAD_DIM ** -0.5, axis=-1)
    out = jnp.einsum('bhsm,bmhd->bshd', p, v).reshape(bsz, s_len, D_MODEL).astype(x.dtype)
    return out @ w_o


def setup_inputs(seed: int = 0) -> dict:
    key = jax.random.key(seed)
    ks = jax.random.split(key, 22)
    f32 = jnp.float32

    def dense(k, shape, fan_in, scale=1.0):
        return jax.random.normal(k, shape, f32) * (scale * fan_in ** -0.5)

    return {
        "x": jax.random.normal(ks[0], (BATCH, SEQ, D_MODEL), f32),
        "mem": jax.random.normal(ks[1], (BATCH, MEM_TOKENS, D_MODEL), f32),
        "ffn1_w_in": dense(ks[2], (DEPTH, D_MODEL, 2 * D_FF), D_MODEL),
        "ffn1_w_out": dense(ks[3], (DEPTH, D_FF, D_MODEL), D_FF, BETA),
        "mix_w_in": dense(ks[4], (DEPTH, D_MODEL, N_IN), D_MODEL),
        "ml_conv_w": dense(ks[5], (DEPTH, CONV_WIDTH, 2 * ML_WIDTH), CONV_WIDTH),
        "ml_conv_b": 0.01 * jax.random.normal(ks[6], (DEPTH, 2 * ML_WIDTH), f32),
        "ml_i_bias": 0.1 * jax.random.normal(ks[7], (DEPTH, ML_HEADS), f32),
        "ml_f_bias": jnp.linspace(3.0, 6.0, ML_HEADS, dtype=f32)[None, :]
                     + 0.1 * jax.random.normal(ks[8], (DEPTH, ML_HEADS), f32),
        "ml_norm_g": 1.0 + 0.01 * jax.random.normal(ks[9], (DEPTH, ML_WIDTH), f32),
        "swa_sinks": 0.5 * jax.random.normal(ks[10], (DEPTH, SWA_HEADS), f32),
        "rel_bias": 0.1 * jax.random.normal(ks[11], (NUM_BUCKETS, SWA_HEADS), f32),
        "mix_w_out": dense(ks[12], (DEPTH, MIX_WIDTH, D_MODEL), MIX_WIDTH, BETA),
        "xattn_w_q": dense(ks[13], (DEPTH, D_MODEL, D_MODEL), D_MODEL),
        "xattn_w_kv": dense(ks[14], (DEPTH, D_MODEL, 2 * D_MODEL), D_MODEL),
        "xattn_w_o": dense(ks[15], (DEPTH, D_MODEL, D_MODEL), D_MODEL, BETA),
        "ffn2_w_in": dense(ks[16], (DEPTH, D_MODEL, 2 * D_FF), D_MODEL),
        "ffn2_w_out": dense(ks[17], (DEPTH, D_FF, D_MODEL), D_FF, BETA),
        "ln_g": 1.0 + 0.01 * jax.random.normal(ks[18], (DEPTH, 4, D_MODEL), f32),
        "ln_b": 0.01 * jax.random.normal(ks[19], (DEPTH, 4, D_MODEL), f32),
    }


def reference(x, mem, ffn1_w_in, ffn1_w_out, mix_w_in, ml_conv_w, ml_conv_b, ml_i_bias, ml_f_bias,
              ml_norm_g, swa_sinks, rel_bias, mix_w_out, xattn_w_q, xattn_w_kv, xattn_w_o,
              ffn2_w_in, ffn2_w_out, ln_g, ln_b):
    for l in range(DEPTH):
        x = layer_norm(ALPHA * x + 0.5 * swiglu(x, ffn1_w_in[l], ffn1_w_out[l]), ln_g[l, 0], ln_b[l, 0])
        x = layer_norm(ALPHA * x + hybrid_mixer(x, mix_w_in[l], ml_conv_w[l], ml_conv_b[l], ml_i_bias[l],
                                                ml_f_bias[l], ml_norm_g[l], swa_sinks[l], rel_bias,
                                                mix_w_out[l]),
                       ln_g[l, 1], ln_b[l, 1])
        x = layer_norm(ALPHA * x + cross_attention(x, mem, xattn_w_q[l], xattn_w_kv[l], xattn_w_o[l]),
                       ln_g[l, 2], ln_b[l, 2])
        x = layer_norm(ALPHA * x + 0.5 * swiglu(x, ffn2_w_in[l], ffn2_w_out[l]), ln_g[l, 3], ln_b[l, 3])
    return x
```

```python
import functools

import numpy as np
import jax
import jax.numpy as jnp
from jax import lax
from jax.experimental import pallas as pl
from jax.experimental.pallas import tpu as pltpu

F32 = jnp.float32
BF16 = jnp.bfloat16

D_MODEL = 1024
DEPTH = 2
HEAD_DIM = 64
SB_HEADS = 4
SWA_HEADS = 8
SWA_KV_HEADS = 2
SWA_GROUP = SWA_HEADS // SWA_KV_HEADS
ML_HEADS = 4
SB_WIDTH = SB_HEADS * HEAD_DIM
SWA_WIDTH = SWA_HEADS * HEAD_DIM
SWA_KV_WIDTH = SWA_KV_HEADS * HEAD_DIM
ML_WIDTH = ML_HEADS * HEAD_DIM
BLOCK = 128
CONV_WIDTH = 4
NUM_BUCKETS = 32
MAX_DISTANCE = 128
XATTN_HEADS = 4
XATTN_HEAD_DIM = D_MODEL // XATTN_HEADS
D_FF = 2816
ALPHA = (2 * DEPTH) ** 0.25
LN_EPS = 1e-5
NEG_BIG = -1e30

_SB_Q, _SB_K, _SB_V = 0, 256, 512
_SW_Q, _SW_K, _SW_V = 768, 1280, 1408
_ML_QK, _ML_V, _ML_I, _ML_F, _ML_O = 1536, 2048, 2304, 2308, 2312
N_IN = 2568
GATE_PAD = 128

VMEM_LIMIT = 56 * 1024 * 1024

TOKEN_TILE = 512
FF_CHUNK = 256


def _params(*sem):
    return pltpu.CompilerParams(dimension_semantics=sem, vmem_limit_bytes=VMEM_LIMIT)


def _resident(shape):
    nd = len(shape)
    return pl.BlockSpec(shape, lambda *_: (0,) * nd, pipeline_mode=pl.Buffered(1))


def _layer_norm(y, g, b):
    mu = jnp.mean(y, axis=-1, keepdims=True)
    d = y - mu
    var = jnp.mean(d * d, axis=-1, keepdims=True)
    return d * lax.rsqrt(var + LN_EPS) * g + b


def _log_sigmoid(x):
    return jnp.minimum(x, 0.0) - jnp.log(1.0 + jnp.exp(-jnp.abs(x)))


def _sigmoid(x):
    return 1.0 / (1.0 + jnp.exp(-x))


def _dot(a, b):
    return jnp.dot(a, b, preferred_element_type=F32)


def _dot_nt(a, b):
    return lax.dot_general(a, b, (((1,), (1,)), ((), ())), preferred_element_type=F32)


def _dot_tn(a, b):
    return lax.dot_general(a, b, (((0,), (0,)), ((), ())), preferred_element_type=F32)


def _split_bf16(x, pieces):
    out = []
    r = x
    for _ in range(pieces - 1):
        p = r.astype(BF16)
        out.append(p)
        r = r - p.astype(F32)
    out.append(r.astype(BF16))
    return out


def _ffn_kernel(x_ref, win_ref, wout_ref, g_ref, b_ref, o_ref, acc_ref):
    xb = x_ref[...].astype(BF16)
    for c in range(D_FF // FF_CHUNK):
        lo = c * FF_CHUNK
        a = _dot(xb, win_ref[:, lo:lo + FF_CHUNK])
        b = _dot(xb, win_ref[:, D_FF + lo:D_FF + lo + FF_CHUNK])
        h = (a * _sigmoid(a) * b).astype(BF16)
        part = _dot(h, wout_ref[lo:lo + FF_CHUNK, :])
        if c == 0:
            acc_ref[...] = part
        else:
            acc_ref[...] += part
    y = ALPHA * x_ref[...] + 0.5 * acc_ref[...]
    o_ref[...] = _layer_norm(y, g_ref[...], b_ref[...])


def _ffn_call(x, w_in, w_out, g, b):
    n = x.shape[0]
    tm = min(TOKEN_TILE, n)
    row = lambda i: (i, 0)
    return pl.pallas_call(
        _ffn_kernel,
        grid=(n // tm,),
        in_specs=[pl.BlockSpec((tm, D_MODEL), row), _resident(w_in.shape), _resident(w_out.shape),
                  _resident(g.shape), _resident(b.shape)],
        out_specs=pl.BlockSpec((tm, D_MODEL), row),
        out_shape=jax.ShapeDtypeStruct((n, D_MODEL), F32),
        scratch_shapes=[pltpu.VMEM((tm, D_MODEL), F32)],
        compiler_params=_params("parallel"),
        name="ffn_ln",
    )(x, w_in, w_out, g, b)


_INPROJ_OUTS = (
    ("sb_q", 0, 256, BF16), ("sb_k", 256, 256, BF16), ("sb_v", 512, 256, BF16),
    ("sw_q", 768, 512, BF16), ("sw_kv", 1280, 256, BF16),
    ("ml_qk", 1536, 512, F32), ("ml_v", 2048, 256, BF16),
    ("ml_o", 2304, 256, F32), ("gates", 2560, GATE_PAD, F32),
)
N_IN_PADDED = 2688


def _inproj_kernel(x_ref, w_ref, *out_refs):
    xb = x_ref[...].astype(BF16)
    for (_, lo, width, dt), o_ref in zip(_INPROJ_OUTS, out_refs):
        o_ref[...] = _dot(xb, w_ref[:, lo:lo + width]).astype(dt)


def _inproj_call(x, w):
    n = x.shape[0]
    tm = min(TOKEN_TILE, n)
    row = lambda i: (i, 0)
    return pl.pallas_call(
        _inproj_kernel,
        grid=(n // tm,),
        in_specs=[pl.BlockSpec((tm, D_MODEL), row), _resident(w.shape)],
        out_specs=[pl.BlockSpec((tm, width), row) for _, _, width, _ in _INPROJ_OUTS],
        out_shape=[jax.ShapeDtypeStruct((n, width), dt) for _, _, width, dt in _INPROJ_OUTS],
        compiler_params=_params("parallel"),
        name="mix_in_proj",
    )(x, w)


def _sb_kernel(q_ref, k_ref, v_ref, o_ref, acc_ref):
    qi = pl.program_id(1)
    q = q_ref[0]
    lane_head = lax.broadcasted_iota(jnp.int32, (1, SB_WIDTH), 1) // HEAD_DIM
    row = lax.broadcasted_iota(jnp.int32, (BLOCK, BLOCK), 0)
    col = lax.broadcasted_iota(jnp.int32, (BLOCK, BLOCK), 1)
    strict = col < row
    suffix = jnp.where(row >= col, 1.0, 0.0).astype(BF16)

    acc_ref[...] = jnp.zeros_like(acc_ref)
    for h in range(SB_HEADS):
        head = lane_head == h
        qh = jnp.where(head, q, jnp.zeros_like(q))

        def block(j, carry, diagonal):
            start = pl.multiple_of(j * BLOCK, BLOCK)
            kb = k_ref[0, pl.ds(start, BLOCK), :]
            vb = v_ref[0, pl.ds(start, BLOCK), :]
            z = _dot_nt(qh, kb) * (HEAD_DIM ** -0.5)
            lom = -(jnp.maximum(z, 0.0) + jnp.log(1.0 + jnp.exp(-jnp.abs(z))))
            if diagonal:
                lom = jnp.where(strict, lom, 0.0)
            hi, lo = _split_bf16(lom, 2)
            suf = _dot(hi, suffix) + _dot(lo, suffix)
            w = jnp.exp(z + suf + carry)
            if diagonal:
                w = jnp.where(strict, w, 0.0)
            vh = jnp.where(head, vb, jnp.zeros_like(vb))
            acc_ref[...] += _dot(w.astype(BF16), vh)
            return carry + suf[:, 0:1]

        carry = block(qi, jnp.zeros((BLOCK, 1), F32), True)
        lax.fori_loop(0, qi, lambda jj, c: block(qi - 1 - jj, c, False), carry)
    o_ref[0] = acc_ref[...].astype(o_ref.dtype)


def _sb_call(q, k, v):
    bsz, s_len, _ = q.shape
    blk = pl.BlockSpec((1, BLOCK, SB_WIDTH), lambda b, i: (b, i, 0))
    whole = pl.BlockSpec((1, s_len, SB_WIDTH), lambda b, i: (b, 0, 0))
    return pl.pallas_call(
        _sb_kernel,
        grid=(bsz, s_len // BLOCK),
        in_specs=[blk, whole, whole],
        out_specs=blk,
        out_shape=jax.ShapeDtypeStruct((bsz, s_len, SB_WIDTH), BF16),
        scratch_shapes=[pltpu.VMEM((BLOCK, SB_WIDTH), F32)],
        compiler_params=_params("parallel", "arbitrary"),
        name="stick_breaking",
    )(q, k, v)


def _swa_kernel(sinks_ref, q_ref, kvp_ref, kvc_ref, bias_ref, o_ref):
    i = pl.program_id(1)
    q = q_ref[0]
    kv = jnp.concatenate([kvp_ref[0], kvc_ref[0]], axis=0)
    row = lax.broadcasted_iota(jnp.int32, (BLOCK, 2 * BLOCK), 0)
    col = lax.broadcasted_iota(jnp.int32, (BLOCK, 2 * BLOCK), 1)
    dist = row + BLOCK - col
    valid = (dist >= 0) & (dist < BLOCK) & ((col >= BLOCK) | (i > 0))
    outs = []
    for h in range(SWA_HEADS):
        g = h // SWA_GROUP
        qh = q[:, h * HEAD_DIM:(h + 1) * HEAD_DIM]
        kh = kv[:, g * HEAD_DIM:(g + 1) * HEAD_DIM]
        vh = kv[:, SWA_KV_WIDTH + g * HEAD_DIM:SWA_KV_WIDTH + (g + 1) * HEAD_DIM]
        logits = _dot_nt(qh, kh) * (HEAD_DIM ** -0.5) + bias_ref[h]
        logits = jnp.where(valid, logits, NEG_BIG)
        sink = sinks_ref[h]
        m = jnp.maximum(jnp.max(logits, axis=-1, keepdims=True), sink)
        p = jnp.exp(logits - m)
        denom = jnp.sum(p, axis=-1, keepdims=True) + jnp.exp(sink - m)
        outs.append(_dot(p.astype(BF16), vh) / denom)
    o_ref[0] = jnp.concatenate(outs, axis=-1).astype(o_ref.dtype)


def _swa_call(q, kv, bias, sinks):
    bsz, s_len, _ = q.shape
    return pl.pallas_call(
        _swa_kernel,
        grid=(bsz, s_len // BLOCK),
        in_specs=[pl.BlockSpec(memory_space=pltpu.SMEM),
                  pl.BlockSpec((1, BLOCK, SWA_WIDTH), lambda b, i: (b, i, 0)),
                  pl.BlockSpec((1, BLOCK, 2 * SWA_KV_WIDTH), lambda b, i: (b, jnp.maximum(i - 1, 0), 0)),
                  pl.BlockSpec((1, BLOCK, 2 * SWA_KV_WIDTH), lambda b, i: (b, i, 0)),
                  _resident(bias.shape)],
        out_specs=pl.BlockSpec((1, BLOCK, SWA_WIDTH), lambda b, i: (b, i, 0)),
        out_shape=jax.ShapeDtypeStruct((bsz, s_len, SWA_WIDTH), BF16),
        compiler_params=_params("parallel", "arbitrary"),
        name="sliding_window",
    )(sinks, q, kv, kv, bias)


def _t5_bucket(dist):
    max_exact = NUM_BUCKETS // 2
    d = np.maximum(dist, 1)
    large = max_exact + (np.log(d / max_exact) / np.log(MAX_DISTANCE / max_exact)
                         * (NUM_BUCKETS - max_exact)).astype(np.int32)
    large = np.minimum(large, NUM_BUCKETS - 1)
    return np.where(dist < max_exact, dist, large).astype(np.int32)


def _swa_bias(rel_bias):
    qi = np.arange(BLOCK)[:, None]
    kj = np.arange(2 * BLOCK)[None, :]
    bucket = _t5_bucket(np.clip(qi + BLOCK - kj, 0, None))
    return rel_bias.astype(F32)[bucket].transpose(2, 0, 1)


def _mlstm_kernel(qk_ref, v_ref, og_ref, gates_ref, gates_t_ref, convw_ref, convb_ref,
                  gbias_row_ref, gbias_col_ref, normg_ref, y_ref,
                  xbuf_ref, c_ref, n_ref, m_ref):
    chunk = pl.program_id(1)

    @pl.when(chunk == 0)
    def _():
        xbuf_ref[0:8, :] = jnp.zeros((8, 2 * ML_WIDTH), F32)
        c_ref[...] = jnp.zeros_like(c_ref)
        n_ref[...] = jnp.zeros_like(n_ref)
        m_ref[...] = jnp.zeros_like(m_ref)

    xbuf_ref[8:8 + BLOCK, :] = qk_ref[0]
    conv = convb_ref[...]
    for j in range(CONV_WIDTH):
        lo = 8 - (CONV_WIDTH - 1) + j
        conv = conv + convw_ref[j:j + 1, :] * xbuf_ref[lo:lo + BLOCK, :]
    xbuf_ref[0:8, :] = xbuf_ref[BLOCK:BLOCK + 8, :]
    act = conv * _sigmoid(conv)
    q_all = act[:, :ML_WIDTH]
    k_all = act[:, ML_WIDTH:] * (HEAD_DIM ** -0.5)
    v_all = v_ref[0]
    og = _sigmoid(og_ref[0])
    normg = normg_ref[...]

    row = lax.broadcasted_iota(jnp.int32, (BLOCK, BLOCK), 0)
    col = lax.broadcasted_iota(jnp.int32, (BLOCK, BLOCK), 1)
    causal = col <= row
    tri = jnp.where(causal, 1.0, 0.0).astype(BF16)
    tri_t = jnp.where(row <= col, 1.0, 0.0).astype(BF16)

    gates = gates_ref[0] + gbias_row_ref[...]
    gates_t = gates_t_ref[0] + gbias_col_ref[...]
    b_cols = sum(_dot(tri, p) for p in _split_bf16(_log_sigmoid(gates), 3))
    b_rows = sum(_dot(p, tri_t) for p in _split_bf16(_log_sigmoid(gates_t), 3))

    outs = []
    for h in range(ML_HEADS):
        sl = slice(h * HEAD_DIM, (h + 1) * HEAD_DIM)
        b_c = b_cols[:, ML_HEADS + h:ML_HEADS + h + 1]
        i_c = gates[:, h:h + 1]
        b_r = b_rows[ML_HEADS + h:ML_HEADS + h + 1, :]
        i_r = gates_t[h:h + 1, :]
        m_prev = m_ref[h:h + 1, 0:1]
        c_prev = c_ref[h]
        n_prev = n_ref[h:h + 1, :]

        qh = q_all[:, sl]
        kh = k_all[:, sl]
        vh = v_all[:, sl]
        qb = qh.astype(BF16)

        dmat = jnp.where(causal, b_c - b_r + i_r, NEG_BIG)
        g = b_c + m_prev
        m_t = jnp.maximum(g, jnp.max(dmat, axis=-1, keepdims=True))
        s = _dot_nt(qb, kh.astype(BF16)) * jnp.exp(dmat - m_t)
        inter = jnp.exp(g - m_t)
        num = inter * _dot(qb, c_prev.astype(BF16)) + _dot(s.astype(BF16), vh)
        den = inter * jnp.sum(qh * n_prev, axis=-1, keepdims=True) + jnp.sum(s, axis=-1, keepdims=True)
        hid = num / jnp.maximum(jnp.abs(den), jnp.exp(-m_t))
        mu = jnp.mean(hid, axis=-1, keepdims=True)
        dev = hid - mu
        var = jnp.mean(dev * dev, axis=-1, keepdims=True)
        outs.append(og[:, sl] * (dev * lax.rsqrt(var + LN_EPS) * normg[:, sl]))

        b_end = b_c[BLOCK - 1:BLOCK, :]
        w_log = b_end - b_c + i_c
        m_new = jnp.maximum(b_end + m_prev, jnp.max(w_log, axis=0, keepdims=True))
        decay = jnp.exp(b_end + m_prev - m_new)
        wk = jnp.exp(w_log - m_new) * kh
        c_ref[h] = decay * c_prev + _dot_tn(wk.astype(BF16), vh)
        n_ref[h:h + 1, :] = decay * n_prev + jnp.sum(wk, axis=0, keepdims=True)
        m_ref[h:h + 1, :] = jnp.broadcast_to(m_new, (1, m_ref.shape[1]))
    y_ref[0] = jnp.concatenate(outs, axis=-1).astype(y_ref.dtype)


def _mlstm_call(qk, v, og, gates, gates_t, conv_w, conv_b, gbias_row, gbias_col, norm_g):
    bsz, s_len, _ = qk.shape
    tok = lambda w: pl.BlockSpec((1, BLOCK, w), lambda b, c: (b, c, 0))
    return pl.pallas_call(
        _mlstm_kernel,
        grid=(bsz, s_len // BLOCK),
        in_specs=[tok(2 * ML_WIDTH), tok(ML_WIDTH), tok(ML_WIDTH), tok(GATE_PAD),
                  pl.BlockSpec((1, 8, BLOCK), lambda b, c: (b, 0, c)),
                  _resident(conv_w.shape), _resident(conv_b.shape), _resident(gbias_row.shape),
                  _resident(gbias_col.shape), _resident(norm_g.shape)],
        out_specs=tok(ML_WIDTH),
        out_shape=jax.ShapeDtypeStruct((bsz, s_len, ML_WIDTH), BF16),
        scratch_shapes=[pltpu.VMEM((8 + BLOCK, 2 * ML_WIDTH), F32),
                        pltpu.VMEM((ML_HEADS, HEAD_DIM, HEAD_DIM), F32),
                        pltpu.VMEM((8, HEAD_DIM), F32),
                        pltpu.VMEM((8, 128), F32)],
        compiler_params=_params("parallel", "arbitrary"),
        name="mlstm",
    )(qk, v, og, gates, gates_t, conv_w, conv_b, gbias_row, gbias_col, norm_g)


def _outproj_kernel(x_ref, ysb_ref, ysw_ref, yml_ref, w_ref, g_ref, b_ref, o_ref):
    mix = (_dot(ysb_ref[...], w_ref[0:SB_WIDTH, :])
           + _dot(ysw_ref[...], w_ref[SB_WIDTH:SB_WIDTH + SWA_WIDTH, :])
           + _dot(yml_ref[...], w_ref[SB_WIDTH + SWA_WIDTH:, :]))
    o_ref[...] = _layer_norm(ALPHA * x_ref[...] + mix, g_ref[...], b_ref[...])


def _outproj_call(x, y_sb, y_sw, y_ml, w, g, b):
    n = x.shape[0]
    tm = min(TOKEN_TILE, n)
    row = lambda i: (i, 0)
    return pl.pallas_call(
        _outproj_kernel,
        grid=(n // tm,),
        in_specs=[pl.BlockSpec((tm, D_MODEL), row), pl.BlockSpec((tm, SB_WIDTH), row),
                  pl.BlockSpec((tm, SWA_WIDTH), row), pl.BlockSpec((tm, ML_WIDTH), row),
                  _resident(w.shape), _resident(g.shape), _resident(b.shape)],
        out_specs=pl.BlockSpec((tm, D_MODEL), row),
        out_shape=jax.ShapeDtypeStruct((n, D_MODEL), F32),
        compiler_params=_params("parallel"),
        name="mix_out_proj_ln",
    )(x, y_sb, y_sw, y_ml, w, g, b)


def _kv_kernel(mem_ref, w_ref, o_ref):
    o_ref[...] = _dot(mem_ref[...].astype(BF16), w_ref[...]).astype(o_ref.dtype)


def _kv_call(mem, w):
    n = mem.shape[0]
    tm = min(256, n)
    return pl.pallas_call(
        _kv_kernel,
        grid=(n // tm,),
        in_specs=[pl.BlockSpec((tm, D_MODEL), lambda i: (i, 0)), _resident(w.shape)],
        out_specs=pl.BlockSpec((tm, 2 * D_MODEL), lambda i: (i, 0)),
        out_shape=jax.ShapeDtypeStruct((n, 2 * D_MODEL), BF16),
        compiler_params=_params("parallel"),
        name="xattn_kv_proj",
    )(mem, w)


def _xattn_kernel(x_ref, kv_ref, wq_ref, wo_ref, g_ref, b_ref, o_ref):
    x = x_ref[0]
    q = _dot(x.astype(BF16), wq_ref[...]).astype(BF16)
    outs = []
    for h in range(XATTN_HEADS):
        lo = h * XATTN_HEAD_DIM
        kh = kv_ref[0, :, lo:lo + XATTN_HEAD_DIM]
        vh = kv_ref[0, :, D_MODEL + lo:D_MODEL + lo + XATTN_HEAD_DIM]
        s = _dot_nt(q[:, lo:lo + XATTN_HEAD_DIM], kh) * (XATTN_HEAD_DIM ** -0.5)
        p = jnp.exp(s - jnp.max(s, axis=-1, keepdims=True))
        denom = jnp.sum(p, axis=-1, keepdims=True)
        outs.append((_dot(p.astype(BF16), vh) / denom).astype(BF16))
    attn = jnp.concatenate(outs, axis=-1)
    y = ALPHA * x + _dot(attn, wo_ref[...])
    o_ref[0] = _layer_norm(y, g_ref[...], b_ref[...])


def _xattn_call(x, kv, w_q, w_o, g, b):
    bsz, s_len, _ = x.shape
    tm = min(TOKEN_TILE, s_len)
    mem_tokens = kv.shape[1]
    return pl.pallas_call(
        _xattn_kernel,
        grid=(bsz, s_len // tm),
        in_specs=[pl.BlockSpec((1, tm, D_MODEL), lambda bi, i: (bi, i, 0)),
                  pl.BlockSpec((1, mem_tokens, 2 * D_MODEL), lambda bi, i: (bi, 0, 0)),
                  _resident(w_q.shape), _resident(w_o.shape), _resident(g.shape), _resident(b.shape)],
        out_specs=pl.BlockSpec((1, tm, D_MODEL), lambda bi, i: (bi, i, 0)),
        out_shape=jax.ShapeDtypeStruct((bsz, s_len, D_MODEL), F32),
        compiler_params=_params("parallel", "parallel"),
        name="xattn_ln",
    )(x, kv, w_q, w_o, g, b)


def _hybrid_mixer(x, w_in, conv_w, conv_b, i_bias, f_bias, norm_g, sinks, bias, w_out, ln_g, ln_b):
    bsz, s_len, _ = x.shape
    n = bsz * s_len
    x2 = x.reshape(n, D_MODEL)
    w_perm = jnp.concatenate(
        [w_in[:, :_ML_I], w_in[:, _ML_O:], w_in[:, _ML_I:_ML_O],
         jnp.zeros((D_MODEL, GATE_PAD - 2 * ML_HEADS), w_in.dtype)], axis=1).astype(BF16)
    sb_q, sb_k, sb_v, sw_q, sw_kv, ml_qk, ml_v, ml_o, gates = _inproj_call(x2, w_perm)
    seq = lambda t: t.reshape(bsz, s_len, t.shape[-1])

    y_sb = _sb_call(seq(sb_q), seq(sb_k), seq(sb_v))
    y_sw = _swa_call(seq(sw_q), seq(sw_kv), bias, sinks)

    gates = seq(gates)
    gates_t = jnp.swapaxes(gates[..., :2 * ML_HEADS], 1, 2)
    gate_bias = jnp.concatenate([i_bias, f_bias])
    gbias_row = jnp.pad(gate_bias, (0, GATE_PAD - 2 * ML_HEADS))[None, :]
    y_ml = _mlstm_call(seq(ml_qk), seq(ml_v), seq(ml_o), gates, gates_t, conv_w, conv_b[None, :],
                       gbias_row, gate_bias[:, None], norm_g[None, :])

    out = _outproj_call(x2, y_sb.reshape(n, -1), y_sw.reshape(n, -1), y_ml.reshape(n, -1),
                        w_out.astype(BF16), ln_g[None, :], ln_b[None, :])
    return out.reshape(bsz, s_len, D_MODEL)


def kernel(x, mem, ffn1_w_in, ffn1_w_out, mix_w_in, ml_conv_w, ml_conv_b, ml_i_bias, ml_f_bias, ml_norm_g, swa_sinks, rel_bias, mix_w_out, xattn_w_q, xattn_w_kv, xattn_w_o, ffn2_w_in, ffn2_w_out, ln_g, ln_b):
    bsz, s_len, _ = x.shape
    n = bsz * s_len
    mem2 = mem.reshape(-1, D_MODEL)
    bias = _swa_bias(rel_bias)
    for l in range(DEPTH):
        g = lambda k: ln_g[l, k][None, :]
        b = lambda k: ln_b[l, k][None, :]
        x = _ffn_call(x.reshape(n, D_MODEL), ffn1_w_in[l].astype(BF16), ffn1_w_out[l].astype(BF16),
                      g(0), b(0)).reshape(bsz, s_len, D_MODEL)
        x = _hybrid_mixer(x, mix_w_in[l], ml_conv_w[l], ml_conv_b[l], ml_i_bias[l], ml_f_bias[l],
                          ml_norm_g[l], swa_sinks[l], bias, mix_w_out[l], ln_g[l, 1], ln_b[l, 1])
        kv = _kv_call(mem2, xattn_w_kv[l].astype(BF16)).reshape(bsz, -1, 2 * D_MODEL)
        x = _xattn_call(x, kv, xattn_w_q[l].astype(BF16), xattn_w_o[l].astype(BF16), g(2), b(2))
        x = _ffn_call(x.reshape(n, D_MODEL), ffn2_w_in[l].astype(BF16), ffn2_w_out[l].astype(BF16),
                      g(3), b(3)).reshape(bsz, s_len, D_MODEL)
    return x
```

```python
import functools

import numpy as np
import jax
import jax.numpy as jnp
from jax import lax
from jax.experimental import pallas as pl
from jax.experimental.pallas import tpu as pltpu

F32 = jnp.float32
BF16 = jnp.bfloat16

D_MODEL = 1024
DEPTH = 2
HEAD_DIM = 64
SB_HEADS = 4
SWA_HEADS = 8
SWA_KV_HEADS = 2
SWA_GROUP = SWA_HEADS // SWA_KV_HEADS
ML_HEADS = 4
SB_WIDTH = SB_HEADS * HEAD_DIM
SWA_WIDTH = SWA_HEADS * HEAD_DIM
SWA_KV_WIDTH = SWA_KV_HEADS * HEAD_DIM
ML_WIDTH = ML_HEADS * HEAD_DIM
BLOCK = 128
CONV_WIDTH = 4
NUM_BUCKETS = 32
MAX_DISTANCE = 128
XATTN_HEADS = 4
XATTN_HEAD_DIM = D_MODEL // XATTN_HEADS
D_FF = 2816
ALPHA = (2 * DEPTH) ** 0.25
LN_EPS = 1e-5
NEG_BIG = -1e30
SB_SKIP_BELOW = -104.0

_SB_Q, _SB_K, _SB_V = 0, 256, 512
_SW_Q, _SW_K, _SW_V = 768, 1280, 1408
_ML_QK, _ML_V, _ML_I, _ML_F, _ML_O = 1536, 2048, 2304, 2308, 2312
N_IN = 2568
GATE_PAD = 128

VMEM_LIMIT = 56 * 1024 * 1024

TOKEN_TILE = 512
FF_CHUNK = 256


def _params(*sem):
    return pltpu.CompilerParams(dimension_semantics=sem, vmem_limit_bytes=VMEM_LIMIT)


def _resident(shape):
    nd = len(shape)
    return pl.BlockSpec(shape, lambda *_: (0,) * nd, pipeline_mode=pl.Buffered(1))


def _layer_norm(y, g, b):
    mu = jnp.mean(y, axis=-1, keepdims=True)
    d = y - mu
    var = jnp.mean(d * d, axis=-1, keepdims=True)
    return d * lax.rsqrt(var + LN_EPS) * g + b


def _log_sigmoid(x):
    return jnp.minimum(x, 0.0) - jnp.log(1.0 + jnp.exp(-jnp.abs(x)))


def _sigmoid(x):
    return 1.0 / (1.0 + jnp.exp(-x))


def _dot(a, b):
    return jnp.dot(a, b, preferred_element_type=F32)


def _dot_nt(a, b):
    return lax.dot_general(a, b, (((1,), (1,)), ((), ())), preferred_element_type=F32)


def _dot_tn(a, b):
    return lax.dot_general(a, b, (((0,), (0,)), ((), ())), preferred_element_type=F32)


def _split_bf16(x, pieces):
    out = []
    r = x
    for _ in range(pieces - 1):
        p = r.astype(BF16)
        out.append(p)
        r = r - p.astype(F32)
    out.append(r.astype(BF16))
    return out


def _ffn_kernel(x_ref, win_ref, wout_ref, g_ref, b_ref, o_ref, acc_ref):
    xb = x_ref[...].astype(BF16)
    for c in range(D_FF // FF_CHUNK):
        lo = c * FF_CHUNK
        a = _dot(xb, win_ref[:, lo:lo + FF_CHUNK])
        b = _dot(xb, win_ref[:, D_FF + lo:D_FF + lo + FF_CHUNK])
        h = (a * _sigmoid(a) * b).astype(BF16)
        part = _dot(h, wout_ref[lo:lo + FF_CHUNK, :])
        if c == 0:
            acc_ref[...] = part
        else:
            acc_ref[...] += part
    y = ALPHA * x_ref[...] + 0.5 * acc_ref[...]
    o_ref[...] = _layer_norm(y, g_ref[...], b_ref[...])


def _ffn_call(x, w_in, w_out, g, b):
    n = x.shape[0]
    tm = min(TOKEN_TILE, n)
    row = lambda i: (i, 0)
    return pl.pallas_call(
        _ffn_kernel,
        grid=(n // tm,),
        in_specs=[pl.BlockSpec((tm, D_MODEL), row), _resident(w_in.shape), _resident(w_out.shape),
                  _resident(g.shape), _resident(b.shape)],
        out_specs=pl.BlockSpec((tm, D_MODEL), row),
        out_shape=jax.ShapeDtypeStruct((n, D_MODEL), F32),
        scratch_shapes=[pltpu.VMEM((tm, D_MODEL), F32)],
        compiler_params=_params("parallel"),
        name="ffn_ln",
    )(x, w_in, w_out, g, b)


_INPROJ_OUTS = (
    ("sb_q", 0, 256, BF16), ("sb_k", 256, 256, BF16), ("sb_v", 512, 256, BF16),
    ("sw_q", 768, 512, BF16), ("sw_kv", 1280, 256, BF16),
    ("ml_qk", 1536, 512, F32), ("ml_v", 2048, 256, BF16),
    ("ml_o", 2304, 256, F32), ("gates", 2560, GATE_PAD, F32),
)
N_IN_PADDED = 2688


def _inproj_kernel(x_ref, w_ref, *out_refs):
    xb = x_ref[...].astype(BF16)
    for (_, lo, width, dt), o_ref in zip(_INPROJ_OUTS, out_refs):
        o_ref[...] = _dot(xb, w_ref[:, lo:lo + width]).astype(dt)


def _inproj_call(x, w):
    n = x.shape[0]
    tm = min(TOKEN_TILE, n)
    row = lambda i: (i, 0)
    return pl.pallas_call(
        _inproj_kernel,
        grid=(n // tm,),
        in_specs=[pl.BlockSpec((tm, D_MODEL), row), _resident(w.shape)],
        out_specs=[pl.BlockSpec((tm, width), row) for _, _, width, _ in _INPROJ_OUTS],
        out_shape=[jax.ShapeDtypeStruct((n, width), dt) for _, _, width, dt in _INPROJ_OUTS],
        compiler_params=_params("parallel"),
        name="mix_in_proj",
    )(x, w)


def _sb_kernel(q_ref, k_ref, v_ref, o_ref, acc_ref, carry_ref):
    qi = pl.program_id(1)
    q = (q_ref[0].astype(F32) * (HEAD_DIM ** -0.5)).astype(BF16)
    lane_head = lax.broadcasted_iota(jnp.int32, (1, SB_WIDTH), 1) // HEAD_DIM
    heads = [lane_head == h for h in range(SB_HEADS)]
    q_heads = [jnp.where(head, q, jnp.zeros_like(q)) for head in heads]
    row = lax.broadcasted_iota(jnp.int32, (BLOCK, 2 * BLOCK), 0)
    col = lax.broadcasted_iota(jnp.int32, (BLOCK, 2 * BLOCK), 1)
    strict = (lax.broadcasted_iota(jnp.int32, (BLOCK, BLOCK), 1)
              < lax.broadcasted_iota(jnp.int32, (BLOCK, BLOCK), 0))
    suffix_and_total = jnp.where((row >= col) | (col >= BLOCK), 1.0, 0.0).astype(BF16)

    def step(j, diagonal):
        start = pl.multiple_of(j * BLOCK, BLOCK)
        kb = k_ref[0, pl.ds(start, BLOCK), :]
        vb = v_ref[0, pl.ds(start, BLOCK), :]
        weights = []
        largest = None
        for h in range(SB_HEADS):
            z = _dot_nt(q_heads[h], kb)
            lom = jnp.minimum(-z, 0.0) - jnp.log(1.0 + jnp.exp(-jnp.abs(z)))
            if diagonal:
                lom = jnp.where(strict, lom, 0.0)
            hi, lo = _split_bf16(lom, 2)
            sums = _dot(hi, suffix_and_total) + _dot(lo, suffix_and_total)
            if diagonal:
                w = jnp.where(strict, jnp.exp(z + sums[:, :BLOCK]), 0.0)
                carry = sums[:, BLOCK:]
            else:
                carry = carry_ref[h]
                w = jnp.exp(z + sums[:, :BLOCK] + carry)
                carry = carry + sums[:, BLOCK:]
            carry_ref[h] = carry
            largest = carry if largest is None else jnp.maximum(largest, carry)
            weights.append(w.astype(BF16))
        v_heads = jnp.concatenate([jnp.where(head, vb, jnp.zeros_like(vb)) for head in heads], axis=0)
        pv = _dot(jnp.concatenate(weights, axis=1), v_heads)
        if diagonal:
            acc_ref[...] = pv
        else:
            acc_ref[...] += pv
        return jnp.max(largest)

    def more_blocks(state):
        j, largest_carry = state
        return (j >= 0) & (largest_carry > SB_SKIP_BELOW)

    def next_block(state):
        j, _ = state
        return j - 1, step(j, False)

    lax.while_loop(more_blocks, next_block, (qi - 1, step(qi, True)))
    o_ref[0] = acc_ref[...].astype(o_ref.dtype)


def _sb_call(q, k, v):
    bsz, s_len, _ = q.shape
    blk = pl.BlockSpec((1, BLOCK, SB_WIDTH), lambda b, i: (b, i, 0))
    whole = pl.BlockSpec((1, s_len, SB_WIDTH), lambda b, i: (b, 0, 0))
    return pl.pallas_call(
        _sb_kernel,
        grid=(bsz, s_len // BLOCK),
        in_specs=[blk, whole, whole],
        out_specs=blk,
        out_shape=jax.ShapeDtypeStruct((bsz, s_len, SB_WIDTH), BF16),
        scratch_shapes=[pltpu.VMEM((BLOCK, SB_WIDTH), F32), pltpu.VMEM((SB_HEADS, BLOCK, BLOCK), F32)],
        compiler_params=_params("parallel", "arbitrary"),
        name="stick_breaking",
    )(q, k, v)


def _swa_kernel(sinks_ref, q_ref, kvp_ref, kvc_ref, bias_ref, o_ref):
    i = pl.program_id(1)
    q = q_ref[0]
    kv = jnp.concatenate([kvp_ref[0], kvc_ref[0]], axis=0)
    row = lax.broadcasted_iota(jnp.int32, (BLOCK, 2 * BLOCK), 0)
    col = lax.broadcasted_iota(jnp.int32, (BLOCK, 2 * BLOCK), 1)
    dist = row + BLOCK - col
    valid = (dist >= 0) & (dist < BLOCK) & ((col >= BLOCK) | (i > 0))
    outs = []
    for h in range(SWA_HEADS):
        g = h // SWA_GROUP
        qh = q[:, h * HEAD_DIM:(h + 1) * HEAD_DIM]
        kh = kv[:, g * HEAD_DIM:(g + 1) * HEAD_DIM]
        vh = kv[:, SWA_KV_WIDTH + g * HEAD_DIM:SWA_KV_WIDTH + (g + 1) * HEAD_DIM]
        logits = _dot_nt(qh, kh) * (HEAD_DIM ** -0.5) + bias_ref[h]
        logits = jnp.where(valid, logits, NEG_BIG)
        sink = sinks_ref[h]
        m = jnp.maximum(jnp.max(logits, axis=-1, keepdims=True), sink)
        p = jnp.exp(logits - m)
        denom = jnp.sum(p, axis=-1, keepdims=True) + jnp.exp(sink - m)
        outs.append(_dot(p.astype(BF16), vh) / denom)
    o_ref[0] = jnp.concatenate(outs, axis=-1).astype(o_ref.dtype)


def _swa_call(q, kv, bias, sinks):
    bsz, s_len, _ = q.shape
    return pl.pallas_call(
        _swa_kernel,
        grid=(bsz, s_len // BLOCK),
        in_specs=[pl.BlockSpec(memory_space=pltpu.SMEM),
                  pl.BlockSpec((1, BLOCK, SWA_WIDTH), lambda b, i: (b, i, 0)),
                  pl.BlockSpec((1, BLOCK, 2 * SWA_KV_WIDTH), lambda b, i: (b, jnp.maximum(i - 1, 0), 0)),
                  pl.BlockSpec((1, BLOCK, 2 * SWA_KV_WIDTH), lambda b, i: (b, i, 0)),
                  _resident(bias.shape)],
        out_specs=pl.BlockSpec((1, BLOCK, SWA_WIDTH), lambda b, i: (b, i, 0)),
        out_shape=jax.ShapeDtypeStruct((bsz, s_len, SWA_WIDTH), BF16),
        compiler_params=_params("parallel", "arbitrary"),
        name="sliding_window",
    )(sinks, q, kv, kv, bias)


def _t5_bucket(dist):
    max_exact = NUM_BUCKETS // 2
    d = np.maximum(dist, 1)
    large = max_exact + (np.log(d / max_exact) / np.log(MAX_DISTANCE / max_exact)
                         * (NUM_BUCKETS - max_exact)).astype(np.int32)
    large = np.minimum(large, NUM_BUCKETS - 1)
    return np.where(dist < max_exact, dist, large).astype(np.int32)


def _swa_bias(rel_bias):
    qi = np.arange(BLOCK)[:, None]
    kj = np.arange(2 * BLOCK)[None, :]
    bucket = jnp.asarray(_t5_bucket(np.clip(qi + BLOCK - kj, 0, None)))[None]
    table = rel_bias.astype(F32)
    bias = jnp.zeros((SWA_HEADS, BLOCK, 2 * BLOCK), F32)
    for b in range(NUM_BUCKETS):
        bias = jnp.where(bucket == b, table[b][:, None, None], bias)
    return bias


def _mlstm_kernel(qk_ref, v_ref, og_ref, gates_ref, gates_t_ref, convw_ref, convb_ref,
                  gbias_row_ref, gbias_col_ref, normg_ref, y_ref,
                  xbuf_ref, c_ref, n_ref, m_ref):
    chunk = pl.program_id(1)

    @pl.when(chunk == 0)
    def _():
        xbuf_ref[0:8, :] = jnp.zeros((8, 2 * ML_WIDTH), F32)
        c_ref[...] = jnp.zeros_like(c_ref)
        n_ref[...] = jnp.zeros_like(n_ref)
        m_ref[...] = jnp.zeros_like(m_ref)

    xbuf_ref[8:8 + BLOCK, :] = qk_ref[0]
    conv = convb_ref[...]
    for j in range(CONV_WIDTH):
        lo = 8 - (CONV_WIDTH - 1) + j
        conv = conv + convw_ref[j:j + 1, :] * xbuf_ref[lo:lo + BLOCK, :]
    xbuf_ref[0:8, :] = xbuf_ref[BLOCK:BLOCK + 8, :]
    act = conv * _sigmoid(conv)
    q_all = act[:, :ML_WIDTH]
    k_all = act[:, ML_WIDTH:] * (HEAD_DIM ** -0.5)
    v_all = v_ref[0]
    og = _sigmoid(og_ref[0])
    normg = normg_ref[...]

    row = lax.broadcasted_iota(jnp.int32, (BLOCK, BLOCK), 0)
    col = lax.broadcasted_iota(jnp.int32, (BLOCK, BLOCK), 1)
    causal = col <= row
    tri = jnp.where(causal, 1.0, 0.0).astype(BF16)
    tri_t = jnp.where(row <= col, 1.0, 0.0).astype(BF16)

    gates = gates_ref[0] + gbias_row_ref[...]
    gates_t = gates_t_ref[0] + gbias_col_ref[...]
    b_cols = sum(_dot(tri, p) for p in _split_bf16(_log_sigmoid(gates), 3))
    b_rows = sum(_dot(p, tri_t) for p in _split_bf16(_log_sigmoid(gates_t), 3))

    outs = []
    for h in range(ML_HEADS):
        sl = slice(h * HEAD_DIM, (h + 1) * HEAD_DIM)
        b_c = b_cols[:, ML_HEADS + h:ML_HEADS + h + 1]
        i_c = gates[:, h:h + 1]
        b_r = b_rows[ML_HEADS + h:ML_HEADS + h + 1, :]
        i_r = gates_t[h:h + 1, :]
        m_prev = m_ref[h:h + 1, 0:1]
        c_prev = c_ref[h]
        n_prev = n_ref[h:h + 1, :]

        qh = q_all[:, sl]
        kh = k_all[:, sl]
        vh = v_all[:, sl]
        qb = qh.astype(BF16)

        dmat = jnp.where(causal, b_c - b_r + i_r, NEG_BIG)
        g = b_c + m_prev
        m_t = jnp.maximum(g, jnp.max(dmat, axis=-1, keepdims=True))
        s = _dot_nt(qb, kh.astype(BF16)) * jnp.exp(dmat - m_t)
        inter = jnp.exp(g - m_t)
        num = inter * _dot(qb, c_prev.astype(BF16)) + _dot(s.astype(BF16), vh)
        den = inter * jnp.sum(qh * n_prev, axis=-1, keepdims=True) + jnp.sum(s, axis=-1, keepdims=True)
        hid = num / jnp.maximum(jnp.abs(den), jnp.exp(-m_t))
        mu = jnp.mean(hid, axis=-1, keepdims=True)
        dev = hid - mu
        var = jnp.mean(dev * dev, axis=-1, keepdims=True)
        outs.append(og[:, sl] * (dev * lax.rsqrt(var + LN_EPS) * normg[:, sl]))

        b_end = b_c[BLOCK - 1:BLOCK, :]
        w_log = b_end - b_c + i_c
        m_new = jnp.maximum(b_end + m_prev, jnp.max(w_log, axis=0, keepdims=True))
        decay = jnp.exp(b_end + m_prev - m_new)
        wk = jnp.exp(w_log - m_new) * kh
        c_ref[h] = decay * c_prev + _dot_tn(wk.astype(BF16), vh)
        n_ref[h:h + 1, :] = decay * n_prev + jnp.sum(wk, axis=0, keepdims=True)
        m_ref[h:h + 1, :] = jnp.broadcast_to(m_new, (1, m_ref.shape[1]))
    y_ref[0] = jnp.concatenate(outs, axis=-1).astype(y_ref.dtype)


def _mlstm_call(qk, v, og, gates, gates_t, conv_w, conv_b, gbias_row, gbias_col, norm_g):
    bsz, s_len, _ = qk.shape
    tok = lambda w: pl.BlockSpec((1, BLOCK, w), lambda b, c: (b, c, 0))
    return pl.pallas_call(
        _mlstm_kernel,
        grid=(bsz, s_len // BLOCK),
        in_specs=[tok(2 * ML_WIDTH), tok(ML_WIDTH), tok(ML_WIDTH), tok(GATE_PAD),
                  pl.BlockSpec((1, 8, BLOCK), lambda b, c: (b, 0, c)),
                  _resident(conv_w.shape), _resident(conv_b.shape), _resident(gbias_row.shape),
                  _resident(gbias_col.shape), _resident(norm_g.shape)],
        out_specs=tok(ML_WIDTH),
        out_shape=jax.ShapeDtypeStruct((bsz, s_len, ML_WIDTH), BF16),
        scratch_shapes=[pltpu.VMEM((8 + BLOCK, 2 * ML_WIDTH), F32),
                        pltpu.VMEM((ML_HEADS, HEAD_DIM, HEAD_DIM), F32),
                        pltpu.VMEM((8, HEAD_DIM), F32),
                        pltpu.VMEM((8, 128), F32)],
        compiler_params=_params("parallel", "arbitrary"),
        name="mlstm",
    )(qk, v, og, gates, gates_t, conv_w, conv_b, gbias_row, gbias_col, norm_g)


def _outproj_kernel(x_ref, ysb_ref, ysw_ref, yml_ref, w_ref, g_ref, b_ref, o_ref):
    mix = (_dot(ysb_ref[...], w_ref[0:SB_WIDTH, :])
           + _dot(ysw_ref[...], w_ref[SB_WIDTH:SB_WIDTH + SWA_WIDTH, :])
           + _dot(yml_ref[...], w_ref[SB_WIDTH + SWA_WIDTH:, :]))
    o_ref[...] = _layer_norm(ALPHA * x_ref[...] + mix, g_ref[...], b_ref[...])


def _outproj_call(x, y_sb, y_sw, y_ml, w, g, b):
    n = x.shape[0]
    tm = min(TOKEN_TILE, n)
    row = lambda i: (i, 0)
    return pl.pallas_call(
        _outproj_kernel,
        grid=(n // tm,),
        in_specs=[pl.BlockSpec((tm, D_MODEL), row), pl.BlockSpec((tm, SB_WIDTH), row),
                  pl.BlockSpec((tm, SWA_WIDTH), row), pl.BlockSpec((tm, ML_WIDTH), row),
                  _resident(w.shape), _resident(g.shape), _resident(b.shape)],
        out_specs=pl.BlockSpec((tm, D_MODEL), row),
        out_shape=jax.ShapeDtypeStruct((n, D_MODEL), F32),
        compiler_params=_params("parallel"),
        name="mix_out_proj_ln",
    )(x, y_sb, y_sw, y_ml, w, g, b)


def _kv_kernel(mem_ref, w_ref, o_ref):
    o_ref[...] = _dot(mem_ref[...].astype(BF16), w_ref[...]).astype(o_ref.dtype)


def _kv_call(mem, w):
    n = mem.shape[0]
    tm = min(256, n)
    return pl.pallas_call(
        _kv_kernel,
        grid=(n // tm,),
        in_specs=[pl.BlockSpec((tm, D_MODEL), lambda i: (i, 0)), _resident(w.shape)],
        out_specs=pl.BlockSpec((tm, 2 * D_MODEL), lambda i: (i, 0)),
        out_shape=jax.ShapeDtypeStruct((n, 2 * D_MODEL), BF16),
        compiler_params=_params("parallel"),
        name="xattn_kv_proj",
    )(mem, w)


def _xattn_kernel(x_ref, kv_ref, wq_ref, wo_ref, g_ref, b_ref, o_ref):
    x = x_ref[0]
    q = _dot(x.astype(BF16), wq_ref[...]).astype(BF16)
    outs = []
    for h in range(XATTN_HEADS):
        lo = h * XATTN_HEAD_DIM
        kh = kv_ref[0, :, lo:lo + XATTN_HEAD_DIM]
        vh = kv_ref[0, :, D_MODEL + lo:D_MODEL + lo + XATTN_HEAD_DIM]
        s = _dot_nt(q[:, lo:lo + XATTN_HEAD_DIM], kh) * (XATTN_HEAD_DIM ** -0.5)
        p = jnp.exp(s - jnp.max(s, axis=-1, keepdims=True))
        denom = jnp.sum(p, axis=-1, keepdims=True)
        outs.append((_dot(p.astype(BF16), vh) / denom).astype(BF16))
    attn = jnp.concatenate(outs, axis=-1)
    y = ALPHA * x + _dot(attn, wo_ref[...])
    o_ref[0] = _layer_norm(y, g_ref[...], b_ref[...])


def _xattn_call(x, kv, w_q, w_o, g, b):
    bsz, s_len, _ = x.shape
    tm = min(TOKEN_TILE, s_len)
    mem_tokens = kv.shape[1]
    return pl.pallas_call(
        _xattn_kernel,
        grid=(bsz, s_len // tm),
        in_specs=[pl.BlockSpec((1, tm, D_MODEL), lambda bi, i: (bi, i, 0)),
                  pl.BlockSpec((1, mem_tokens, 2 * D_MODEL), lambda bi, i: (bi, 0, 0)),
                  _resident(w_q.shape), _resident(w_o.shape), _resident(g.shape), _resident(b.shape)],
        out_specs=pl.BlockSpec((1, tm, D_MODEL), lambda bi, i: (bi, i, 0)),
        out_shape=jax.ShapeDtypeStruct((bsz, s_len, D_MODEL), F32),
        compiler_params=_params("parallel", "parallel"),
        name="xattn_ln",
    )(x, kv, w_q, w_o, g, b)


def _hybrid_mixer(x, w_in, conv_w, conv_b, i_bias, f_bias, norm_g, sinks, bias, w_out, ln_g, ln_b):
    bsz, s_len, _ = x.shape
    n = bsz * s_len
    x2 = x.reshape(n, D_MODEL)
    w_perm = jnp.concatenate(
        [w_in[:, :_ML_I], w_in[:, _ML_O:], w_in[:, _ML_I:_ML_O],
         jnp.zeros((D_MODEL, GATE_PAD - 2 * ML_HEADS), w_in.dtype)], axis=1).astype(BF16)
    sb_q, sb_k, sb_v, sw_q, sw_kv, ml_qk, ml_v, ml_o, gates = _inproj_call(x2, w_perm)
    seq = lambda t: t.reshape(bsz, s_len, t.shape[-1])

    y_sb = _sb_call(seq(sb_q), seq(sb_k), seq(sb_v))
    y_sw = _swa_call(seq(sw_q), seq(sw_kv), bias, sinks)

    gates = seq(gates)
    gates_t = jnp.swapaxes(gates[..., :2 * ML_HEADS], 1, 2)
    gate_bias = jnp.concatenate([i_bias, f_bias])
    gbias_row = jnp.pad(gate_bias, (0, GATE_PAD - 2 * ML_HEADS))[None, :]
    y_ml = _mlstm_call(seq(ml_qk), seq(ml_v), seq(ml_o), gates, gates_t, conv_w, conv_b[None, :],
                       gbias_row, gate_bias[:, None], norm_g[None, :])

    out = _outproj_call(x2, y_sb.reshape(n, -1), y_sw.reshape(n, -1), y_ml.reshape(n, -1),
                        w_out.astype(BF16), ln_g[None, :], ln_b[None, :])
    return out.reshape(bsz, s_len, D_MODEL)


def kernel(x, mem, ffn1_w_in, ffn1_w_out, mix_w_in, ml_conv_w, ml_conv_b, ml_i_bias, ml_f_bias, ml_norm_g, swa_sinks, rel_bias, mix_w_out, xattn_w_q, xattn_w_kv, xattn_w_o, ffn2_w_in, ffn2_w_out, ln_g, ln_b):
    bsz, s_len, _ = x.shape
    n = bsz * s_len
    mem2 = mem.reshape(-1, D_MODEL)
    bias = _swa_bias(rel_bias)
    for l in range(DEPTH):
        g = lambda k: ln_g[l, k][None, :]
        b = lambda k: ln_b[l, k][None, :]
        x = _ffn_call(x.reshape(n, D_MODEL), ffn1_w_in[l].astype(BF16), ffn1_w_out[l].astype(BF16),
                      g(0), b(0)).reshape(bsz, s_len, D_MODEL)
        x = _hybrid_mixer(x, mix_w_in[l], ml_conv_w[l], ml_conv_b[l], ml_i_bias[l], ml_f_bias[l],
                          ml_norm_g[l], swa_sinks[l], bias, mix_w_out[l], ln_g[l, 1], ln_b[l, 1])
        kv = _kv_call(mem2, xattn_w_kv[l].astype(BF16)).reshape(bsz, -1, 2 * D_MODEL)
        x = _xattn_call(x, kv, xattn_w_q[l].astype(BF16), xattn_w_o[l].astype(BF16), g(2), b(2))
        x = _ffn_call(x.reshape(n, D_MODEL), ffn2_w_in[l].astype(BF16), ffn2_w_out[l].astype(BF16),
                      g(3), b(3)).reshape(bsz, s_len, D_MODEL)
    return x
```

```python
import functools

import numpy as np
import jax
import jax.numpy as jnp
from jax import lax
from jax.experimental import pallas as pl
from jax.experimental.pallas import tpu as pltpu

F32 = jnp.float32
BF16 = jnp.bfloat16

D_MODEL = 1024
DEPTH = 2
HEAD_DIM = 64
SB_HEADS = 4
SWA_HEADS = 8
SWA_KV_HEADS = 2
SWA_GROUP = SWA_HEADS // SWA_KV_HEADS
ML_HEADS = 4
SB_WIDTH = SB_HEADS * HEAD_DIM
SWA_WIDTH = SWA_HEADS * HEAD_DIM
SWA_KV_WIDTH = SWA_KV_HEADS * HEAD_DIM
ML_WIDTH = ML_HEADS * HEAD_DIM
BLOCK = 128
CONV_WIDTH = 4
NUM_BUCKETS = 32
MAX_DISTANCE = 128
XATTN_HEADS = 4
XATTN_HEAD_DIM = D_MODEL // XATTN_HEADS
D_FF = 2816
ALPHA = (2 * DEPTH) ** 0.25
LN_EPS = 1e-5
NEG_BIG = -1e30
SB_SKIP_BELOW = -104.0

_SB_Q, _SB_K, _SB_V = 0, 256, 512
_SW_Q, _SW_K, _SW_V = 768, 1280, 1408
_ML_QK, _ML_V, _ML_I, _ML_F, _ML_O = 1536, 2048, 2304, 2308, 2312
N_IN = 2568
GATE_PAD = 128

VMEM_LIMIT = 56 * 1024 * 1024

SB_ROWS_PER_STEP = 2
TOKEN_TILE = 512
FF_CHUNK = 256


def _params(*sem):
    return pltpu.CompilerParams(dimension_semantics=sem, vmem_limit_bytes=VMEM_LIMIT)


def _resident(shape):
    nd = len(shape)
    return pl.BlockSpec(shape, lambda *_: (0,) * nd, pipeline_mode=pl.Buffered(1))


def _layer_norm(y, g, b):
    mu = jnp.mean(y, axis=-1, keepdims=True)
    d = y - mu
    var = jnp.mean(d * d, axis=-1, keepdims=True)
    return d * lax.rsqrt(var + LN_EPS) * g + b


def _log_sigmoid(x):
    return jnp.minimum(x, 0.0) - jnp.log(1.0 + jnp.exp(-jnp.abs(x)))


def _sigmoid(x):
    return 1.0 / (1.0 + jnp.exp(-x))


def _dot(a, b):
    return jnp.dot(a, b, preferred_element_type=F32)


def _dot_nt(a, b):
    return lax.dot_general(a, b, (((1,), (1,)), ((), ())), preferred_element_type=F32)


def _dot_tn(a, b):
    return lax.dot_general(a, b, (((0,), (0,)), ((), ())), preferred_element_type=F32)


def _split_bf16(x, pieces):
    out = []
    r = x
    for _ in range(pieces - 1):
        p = r.astype(BF16)
        out.append(p)
        r = r - p.astype(F32)
    out.append(r.astype(BF16))
    return out


def _ffn_kernel(x_ref, win_ref, wout_ref, g_ref, b_ref, o_ref, acc_ref):
    xb = x_ref[...].astype(BF16)
    for c in range(D_FF // FF_CHUNK):
        lo = c * FF_CHUNK
        a = _dot(xb, win_ref[:, lo:lo + FF_CHUNK])
        b = _dot(xb, win_ref[:, D_FF + lo:D_FF + lo + FF_CHUNK])
        h = (a * _sigmoid(a) * b).astype(BF16)
        part = _dot(h, wout_ref[lo:lo + FF_CHUNK, :])
        if c == 0:
            acc_ref[...] = part
        else:
            acc_ref[...] += part
    y = ALPHA * x_ref[...] + 0.5 * acc_ref[...]
    o_ref[...] = _layer_norm(y, g_ref[...], b_ref[...])


def _ffn_call(x, w_in, w_out, g, b):
    n = x.shape[0]
    tm = min(TOKEN_TILE, n)
    row = lambda i: (i, 0)
    return pl.pallas_call(
        _ffn_kernel,
        grid=(n // tm,),
        in_specs=[pl.BlockSpec((tm, D_MODEL), row), _resident(w_in.shape), _resident(w_out.shape),
                  _resident(g.shape), _resident(b.shape)],
        out_specs=pl.BlockSpec((tm, D_MODEL), row),
        out_shape=jax.ShapeDtypeStruct((n, D_MODEL), F32),
        scratch_shapes=[pltpu.VMEM((tm, D_MODEL), F32)],
        compiler_params=_params("parallel"),
        name="ffn_ln",
    )(x, w_in, w_out, g, b)


_INPROJ_OUTS = (
    ("sb_q", 0, 256, BF16), ("sb_k", 256, 256, BF16), ("sb_v", 512, 256, BF16),
    ("sw_q", 768, 512, BF16), ("sw_kv", 1280, 256, BF16),
    ("ml_qk", 1536, 512, F32), ("ml_v", 2048, 256, BF16),
    ("ml_o", 2304, 256, F32), ("gates", 2560, GATE_PAD, F32),
)
N_IN_PADDED = 2688


def _inproj_kernel(x_ref, w_ref, *out_refs):
    xb = x_ref[...].astype(BF16)
    for (_, lo, width, dt), o_ref in zip(_INPROJ_OUTS, out_refs):
        o_ref[...] = _dot(xb, w_ref[:, lo:lo + width]).astype(dt)


def _inproj_call(x, w):
    n = x.shape[0]
    tm = min(TOKEN_TILE, n)
    row = lambda i: (i, 0)
    return pl.pallas_call(
        _inproj_kernel,
        grid=(n // tm,),
        in_specs=[pl.BlockSpec((tm, D_MODEL), row), _resident(w.shape)],
        out_specs=[pl.BlockSpec((tm, width), row) for _, _, width, _ in _INPROJ_OUTS],
        out_shape=[jax.ShapeDtypeStruct((n, width), dt) for _, _, width, dt in _INPROJ_OUTS],
        compiler_params=_params("parallel"),
        name="mix_in_proj",
    )(x, w)


def _sb_kernel(q_ref, k_ref, v_ref, o_ref, acc_ref, carry_ref):
    qi = pl.program_id(1)
    rows = q_ref.shape[0]
    lane_head = lax.broadcasted_iota(jnp.int32, (1, SB_WIDTH), 1) // HEAD_DIM
    heads = [lane_head == h for h in range(SB_HEADS)]
    q_heads = []
    for bi in range(rows):
        q = (q_ref[bi].astype(F32) * (HEAD_DIM ** -0.5)).astype(BF16)
        q_heads.append([jnp.where(head, q, jnp.zeros_like(q)) for head in heads])
    row = lax.broadcasted_iota(jnp.int32, (BLOCK, 2 * BLOCK), 0)
    col = lax.broadcasted_iota(jnp.int32, (BLOCK, 2 * BLOCK), 1)
    strict = (lax.broadcasted_iota(jnp.int32, (BLOCK, BLOCK), 1)
              < lax.broadcasted_iota(jnp.int32, (BLOCK, BLOCK), 0))
    suffix_and_total = jnp.where((row >= col) | (col >= BLOCK), 1.0, 0.0).astype(BF16)

    def step(j, diagonal):
        start = pl.multiple_of(j * BLOCK, BLOCK)
        largest = None
        for bi in range(rows):
            kb = k_ref[bi, pl.ds(start, BLOCK), :]
            vb = v_ref[bi, pl.ds(start, BLOCK), :]
            weights = []
            for h in range(SB_HEADS):
                z = _dot_nt(q_heads[bi][h], kb)
                lom = jnp.minimum(-z, 0.0) - jnp.log(1.0 + jnp.exp(-jnp.abs(z)))
                if diagonal:
                    lom = jnp.where(strict, lom, 0.0)
                hi, lo = _split_bf16(lom, 2)
                sums = _dot(hi, suffix_and_total) + _dot(lo, suffix_and_total)
                if diagonal:
                    w = jnp.where(strict, jnp.exp(z + sums[:, :BLOCK]), 0.0)
                    carry = sums[:, BLOCK:]
                else:
                    carry = carry_ref[bi * SB_HEADS + h]
                    w = jnp.exp(z + sums[:, :BLOCK] + carry)
                    carry = carry + sums[:, BLOCK:]
                carry_ref[bi * SB_HEADS + h] = carry
                largest = carry if largest is None else jnp.maximum(largest, carry)
                weights.append(w.astype(BF16))
            v_heads = jnp.concatenate([jnp.where(head, vb, jnp.zeros_like(vb)) for head in heads], axis=0)
            pv = _dot(jnp.concatenate(weights, axis=1), v_heads)
            if diagonal:
                acc_ref[bi] = pv
            else:
                acc_ref[bi] += pv
        return jnp.max(largest)

    def more_blocks(state):
        j, largest_carry = state
        return (j >= 0) & (largest_carry > SB_SKIP_BELOW)

    def next_block(state):
        j, _ = state
        return j - 1, step(j, False)

    lax.while_loop(more_blocks, next_block, (qi - 1, step(qi, True)))
    o_ref[...] = acc_ref[...].astype(o_ref.dtype)


def _sb_call(q, k, v):
    bsz, s_len, _ = q.shape
    rows = SB_ROWS_PER_STEP if bsz % SB_ROWS_PER_STEP == 0 else 1
    blk = pl.BlockSpec((rows, BLOCK, SB_WIDTH), lambda b, i: (b, i, 0))
    whole = pl.BlockSpec((rows, s_len, SB_WIDTH), lambda b, i: (b, 0, 0), pipeline_mode=pl.Buffered(1))
    return pl.pallas_call(
        _sb_kernel,
        grid=(bsz // rows, s_len // BLOCK),
        in_specs=[blk, whole, whole],
        out_specs=blk,
        out_shape=jax.ShapeDtypeStruct((bsz, s_len, SB_WIDTH), BF16),
        scratch_shapes=[pltpu.VMEM((rows, BLOCK, SB_WIDTH), F32),
                        pltpu.VMEM((rows * SB_HEADS, BLOCK, BLOCK), F32)],
        compiler_params=_params("parallel", "arbitrary"),
        name="stick_breaking",
    )(q, k, v)


def _swa_kernel(sinks_ref, q_ref, kvp_ref, kvc_ref, bias_ref, o_ref):
    i = pl.program_id(1)
    q = q_ref[0]
    kv = jnp.concatenate([kvp_ref[0], kvc_ref[0]], axis=0)
    row = lax.broadcasted_iota(jnp.int32, (BLOCK, 2 * BLOCK), 0)
    col = lax.broadcasted_iota(jnp.int32, (BLOCK, 2 * BLOCK), 1)
    dist = row + BLOCK - col
    valid = (dist >= 0) & (dist < BLOCK) & ((col >= BLOCK) | (i > 0))
    outs = []
    for h in range(SWA_HEADS):
        g = h // SWA_GROUP
        qh = q[:, h * HEAD_DIM:(h + 1) * HEAD_DIM]
        kh = kv[:, g * HEAD_DIM:(g + 1) * HEAD_DIM]
        vh = kv[:, SWA_KV_WIDTH + g * HEAD_DIM:SWA_KV_WIDTH + (g + 1) * HEAD_DIM]
        logits = _dot_nt(qh, kh) * (HEAD_DIM ** -0.5) + bias_ref[h]
        logits = jnp.where(valid, logits, NEG_BIG)
        sink = sinks_ref[h]
        m = jnp.maximum(jnp.max(logits, axis=-1, keepdims=True), sink)
        p = jnp.exp(logits - m)
        denom = jnp.sum(p, axis=-1, keepdims=True) + jnp.exp(sink - m)
        outs.append(_dot(p.astype(BF16), vh) / denom)
    o_ref[0] = jnp.concatenate(outs, axis=-1).astype(o_ref.dtype)


def _swa_call(q, kv, bias, sinks):
    bsz, s_len, _ = q.shape
    return pl.pallas_call(
        _swa_kernel,
        grid=(bsz, s_len // BLOCK),
        in_specs=[pl.BlockSpec(memory_space=pltpu.SMEM),
                  pl.BlockSpec((1, BLOCK, SWA_WIDTH), lambda b, i: (b, i, 0)),
                  pl.BlockSpec((1, BLOCK, 2 * SWA_KV_WIDTH), lambda b, i: (b, jnp.maximum(i - 1, 0), 0)),
                  pl.BlockSpec((1, BLOCK, 2 * SWA_KV_WIDTH), lambda b, i: (b, i, 0)),
                  _resident(bias.shape)],
        out_specs=pl.BlockSpec((1, BLOCK, SWA_WIDTH), lambda b, i: (b, i, 0)),
        out_shape=jax.ShapeDtypeStruct((bsz, s_len, SWA_WIDTH), BF16),
        compiler_params=_params("parallel", "arbitrary"),
        name="sliding_window",
    )(sinks, q, kv, kv, bias)


def _t5_bucket(dist):
    max_exact = NUM_BUCKETS // 2
    d = np.maximum(dist, 1)
    large = max_exact + (np.log(d / max_exact) / np.log(MAX_DISTANCE / max_exact)
                         * (NUM_BUCKETS - max_exact)).astype(np.int32)
    large = np.minimum(large, NUM_BUCKETS - 1)
    return np.where(dist < max_exact, dist, large).astype(np.int32)


def _swa_bias(rel_bias):
    qi = np.arange(BLOCK)[:, None]
    kj = np.arange(2 * BLOCK)[None, :]
    bucket = jnp.asarray(_t5_bucket(np.clip(qi + BLOCK - kj, 0, None)))[None]
    table = rel_bias.astype(F32)
    bias = jnp.zeros((SWA_HEADS, BLOCK, 2 * BLOCK), F32)
    for b in range(NUM_BUCKETS):
        bias = jnp.where(bucket == b, table[b][:, None, None], bias)
    return bias


def _mlstm_kernel(qk_ref, v_ref, og_ref, gates_ref, gates_t_ref, convw_ref, convb_ref,
                  gbias_row_ref, gbias_col_ref, normg_ref, y_ref,
                  xbuf_ref, c_ref, n_ref, m_ref):
    chunk = pl.program_id(1)

    @pl.when(chunk == 0)
    def _():
        xbuf_ref[0:8, :] = jnp.zeros((8, 2 * ML_WIDTH), F32)
        c_ref[...] = jnp.zeros_like(c_ref)
        n_ref[...] = jnp.zeros_like(n_ref)
        m_ref[...] = jnp.zeros_like(m_ref)

    xbuf_ref[8:8 + BLOCK, :] = qk_ref[0]
    conv = convb_ref[...]
    for j in range(CONV_WIDTH):
        lo = 8 - (CONV_WIDTH - 1) + j
        conv = conv + convw_ref[j:j + 1, :] * xbuf_ref[lo:lo + BLOCK, :]
    xbuf_ref[0:8, :] = xbuf_ref[BLOCK:BLOCK + 8, :]
    act = conv * _sigmoid(conv)
    qb = act[:, :ML_WIDTH].astype(BF16)
    k_all = act[:, ML_WIDTH:] * (HEAD_DIM ** -0.5)
    kb = k_all.astype(BF16)
    v_all = v_ref[0]

    iota = lambda shape, axis: lax.broadcasted_iota(jnp.int32, shape, axis)
    row = iota((BLOCK, BLOCK), 0)
    col = iota((BLOCK, BLOCK), 1)
    causal = col <= row
    one_where = lambda cond: jnp.where(cond, 1.0, 0.0).astype(BF16)
    tri = one_where(causal)
    tri_t = one_where(row <= col)
    heads = [iota((1, ML_WIDTH), 1) // HEAD_DIM == h for h in range(ML_HEADS)]
    same_head = iota((ML_WIDTH, ML_WIDTH), 0) // HEAD_DIM == iota((ML_WIDTH, ML_WIDTH), 1) // HEAD_DIM
    head_mean = jnp.where(same_head, 1.0 / HEAD_DIM, 0.0).astype(BF16)
    sel_tile = one_where(iota((BLOCK, ML_HEADS * BLOCK), 0) == iota((BLOCK, ML_HEADS * BLOCK), 1) // BLOCK)
    sel_head = one_where(iota((BLOCK, ML_WIDTH), 0) == iota((BLOCK, ML_WIDTH), 1) // HEAD_DIM)
    rows_of_head = one_where(iota((ML_HEADS * BLOCK, ML_WIDTH), 0) // BLOCK
                             == iota((ML_HEADS * BLOCK, ML_WIDTH), 1) // HEAD_DIM)

    def spread(tile, selector):
        live = iota(tile.shape, 1) < ML_HEADS
        return sum(_dot(p, selector) for p in _split_bf16(jnp.where(live, tile, 0.0), 3))

    gates = gates_ref[0] + gbias_row_ref[...]
    gates_t = gates_t_ref[0] + gbias_col_ref[...]
    b_cols = sum(_dot(tri, p) for p in _split_bf16(_log_sigmoid(gates), 3))
    b_rows = sum(_dot(p, tri_t) for p in _split_bf16(_log_sigmoid(gates_t), 3))
    b_tile = pltpu.roll(b_cols, BLOCK - ML_HEADS, axis=1)
    a_tile = gates - b_tile
    a_rows = gates_t[0:ML_HEADS, :] - b_rows[ML_HEADS:2 * ML_HEADS, :]

    run_max = a_tile
    shift = 1
    while shift < BLOCK:
        run_max = jnp.where(row >= shift, jnp.maximum(run_max, pltpu.roll(run_max, shift, axis=0)), run_max)
        shift *= 2
    m_prev = m_ref[0:1, :]
    u = jnp.maximum(m_prev, run_max)
    inter = jnp.exp(m_prev - u)
    floor = jnp.exp(-(b_tile + u))
    b_end = b_tile[BLOCK - 1:BLOCK, :]
    w_log = b_end - b_tile + gates
    m_new = jnp.maximum(b_end + m_prev, jnp.max(w_log, axis=0, keepdims=True))
    decay = jnp.exp(b_end + m_prev - m_new)
    w = jnp.exp(w_log - m_new)
    m_ref[...] = jnp.broadcast_to(m_new, m_ref.shape)

    neg_u = spread(-u, sel_tile)
    inter = spread(inter, sel_head)
    floor = spread(floor, sel_head)
    w = spread(w, sel_head)
    decay = spread(jnp.broadcast_to(decay, (8, BLOCK)), sel_head)[0:1, :]

    s_heads = []
    for h in range(ML_HEADS):
        log_d = jnp.where(causal, a_rows[h:h + 1, :] + neg_u[:, h * BLOCK:(h + 1) * BLOCK], NEG_BIG)
        qh = jnp.where(heads[h], qb, jnp.zeros_like(qb))
        s_heads.append((_dot_nt(qh, kb) * jnp.exp(log_d)).astype(BF16))
    s_cat = jnp.concatenate(s_heads, axis=1)
    v_heads = jnp.concatenate([jnp.where(head, v_all, jnp.zeros_like(v_all)) for head in heads], axis=0)

    c_prev = c_ref[...]
    n_prev = n_ref[...]
    num = inter * _dot(qb, c_prev.astype(BF16)) + _dot(s_cat, v_heads)
    den = inter * _dot(qb, n_prev.astype(BF16)) + _dot(s_cat, rows_of_head)
    hid = num / jnp.maximum(jnp.abs(den), floor)
    mu = sum(_dot(p, head_mean) for p in _split_bf16(hid, 2))
    dev = hid - mu
    var = sum(_dot(p, head_mean) for p in _split_bf16(dev * dev, 2))
    y = _sigmoid(og_ref[0]) * (dev * lax.rsqrt(var + LN_EPS) * normg_ref[...])
    y_ref[0] = y.astype(y_ref.dtype)

    wk = (w * k_all).astype(BF16)
    update = _dot_tn(wk, jnp.concatenate([v_all, jnp.ones_like(v_all)], axis=1))
    c_ref[...] = decay * c_prev + jnp.where(same_head, update[:, :ML_WIDTH], 0.0)
    n_ref[...] = decay * n_prev + jnp.where(same_head, update[:, ML_WIDTH:], 0.0)


def _mlstm_call(qk, v, og, gates, gates_t, conv_w, conv_b, gbias_row, gbias_col, norm_g):
    bsz, s_len, _ = qk.shape
    tok = lambda w: pl.BlockSpec((1, BLOCK, w), lambda b, c: (b, c, 0))
    return pl.pallas_call(
        _mlstm_kernel,
        grid=(bsz, s_len // BLOCK),
        in_specs=[tok(2 * ML_WIDTH), tok(ML_WIDTH), tok(ML_WIDTH), tok(GATE_PAD),
                  pl.BlockSpec((1, 8, BLOCK), lambda b, c: (b, 0, c)),
                  _resident(conv_w.shape), _resident(conv_b.shape), _resident(gbias_row.shape),
                  _resident(gbias_col.shape), _resident(norm_g.shape)],
        out_specs=tok(ML_WIDTH),
        out_shape=jax.ShapeDtypeStruct((bsz, s_len, ML_WIDTH), BF16),
        scratch_shapes=[pltpu.VMEM((8 + BLOCK, 2 * ML_WIDTH), F32),
                        pltpu.VMEM((ML_WIDTH, ML_WIDTH), F32),
                        pltpu.VMEM((ML_WIDTH, ML_WIDTH), F32),
                        pltpu.VMEM((8, 128), F32)],
        compiler_params=_params("parallel", "arbitrary"),
        name="mlstm",
    )(qk, v, og, gates, gates_t, conv_w, conv_b, gbias_row, gbias_col, norm_g)


def _outproj_kernel(x_ref, ysb_ref, ysw_ref, yml_ref, w_ref, g_ref, b_ref, o_ref):
    mix = (_dot(ysb_ref[...], w_ref[0:SB_WIDTH, :])
           + _dot(ysw_ref[...], w_ref[SB_WIDTH:SB_WIDTH + SWA_WIDTH, :])
           + _dot(yml_ref[...], w_ref[SB_WIDTH + SWA_WIDTH:, :]))
    o_ref[...] = _layer_norm(ALPHA * x_ref[...] + mix, g_ref[...], b_ref[...])


def _outproj_call(x, y_sb, y_sw, y_ml, w, g, b):
    n = x.shape[0]
    tm = min(TOKEN_TILE, n)
    row = lambda i: (i, 0)
    return pl.pallas_call(
        _outproj_kernel,
        grid=(n // tm,),
        in_specs=[pl.BlockSpec((tm, D_MODEL), row), pl.BlockSpec((tm, SB_WIDTH), row),
                  pl.BlockSpec((tm, SWA_WIDTH), row), pl.BlockSpec((tm, ML_WIDTH), row),
                  _resident(w.shape), _resident(g.shape), _resident(b.shape)],
        out_specs=pl.BlockSpec((tm, D_MODEL), row),
        out_shape=jax.ShapeDtypeStruct((n, D_MODEL), F32),
        compiler_params=_params("parallel"),
        name="mix_out_proj_ln",
    )(x, y_sb, y_sw, y_ml, w, g, b)


def _kv_kernel(mem_ref, w_ref, o_ref):
    o_ref[...] = _dot(mem_ref[...].astype(BF16), w_ref[...]).astype(o_ref.dtype)


def _kv_call(mem, w):
    n = mem.shape[0]
    tm = min(256, n)
    return pl.pallas_call(
        _kv_kernel,
        grid=(n // tm,),
        in_specs=[pl.BlockSpec((tm, D_MODEL), lambda i: (i, 0)), _resident(w.shape)],
        out_specs=pl.BlockSpec((tm, 2 * D_MODEL), lambda i: (i, 0)),
        out_shape=jax.ShapeDtypeStruct((n, 2 * D_MODEL), BF16),
        compiler_params=_params("parallel"),
        name="xattn_kv_proj",
    )(mem, w)


def _xattn_kernel(x_ref, kv_ref, wq_ref, wo_ref, g_ref, b_ref, o_ref):
    x = x_ref[0]
    q = _dot(x.astype(BF16), wq_ref[...]).astype(BF16)
    outs = []
    for h in range(XATTN_HEADS):
        lo = h * XATTN_HEAD_DIM
        kh = kv_ref[0, :, lo:lo + XATTN_HEAD_DIM]
        vh = kv_ref[0, :, D_MODEL + lo:D_MODEL + lo + XATTN_HEAD_DIM]
        s = _dot_nt(q[:, lo:lo + XATTN_HEAD_DIM], kh) * (XATTN_HEAD_DIM ** -0.5)
        p = jnp.exp(s - jnp.max(s, axis=-1, keepdims=True))
        denom = jnp.sum(p, axis=-1, keepdims=True)
        outs.append((_dot(p.astype(BF16), vh) / denom).astype(BF16))
    attn = jnp.concatenate(outs, axis=-1)
    y = ALPHA * x + _dot(attn, wo_ref[...])
    o_ref[0] = _layer_norm(y, g_ref[...], b_ref[...])


def _xattn_call(x, kv, w_q, w_o, g, b):
    bsz, s_len, _ = x.shape
    tm = min(TOKEN_TILE, s_len)
    mem_tokens = kv.shape[1]
    return pl.pallas_call(
        _xattn_kernel,
        grid=(bsz, s_len // tm),
        in_specs=[pl.BlockSpec((1, tm, D_MODEL), lambda bi, i: (bi, i, 0)),
                  pl.BlockSpec((1, mem_tokens, 2 * D_MODEL), lambda bi, i: (bi, 0, 0)),
                  _resident(w_q.shape), _resident(w_o.shape), _resident(g.shape), _resident(b.shape)],
        out_specs=pl.BlockSpec((1, tm, D_MODEL), lambda bi, i: (bi, i, 0)),
        out_shape=jax.ShapeDtypeStruct((bsz, s_len, D_MODEL), F32),
        compiler_params=_params("parallel", "parallel"),
        name="xattn_ln",
    )(x, kv, w_q, w_o, g, b)


def _hybrid_mixer(x, w_in, conv_w, conv_b, i_bias, f_bias, norm_g, sinks, bias, w_out, ln_g, ln_b):
    bsz, s_len, _ = x.shape
    n = bsz * s_len
    x2 = x.reshape(n, D_MODEL)
    w_perm = jnp.concatenate(
        [w_in[:, :_ML_I], w_in[:, _ML_O:], w_in[:, _ML_I:_ML_O],
         jnp.zeros((D_MODEL, GATE_PAD - 2 * ML_HEADS), w_in.dtype)], axis=1).astype(BF16)
    sb_q, sb_k, sb_v, sw_q, sw_kv, ml_qk, ml_v, ml_o, gates = _inproj_call(x2, w_perm)
    seq = lambda t: t.reshape(bsz, s_len, t.shape[-1])

    y_sb = _sb_call(seq(sb_q), seq(sb_k), seq(sb_v))
    y_sw = _swa_call(seq(sw_q), seq(sw_kv), bias, sinks)

    gates = seq(gates)
    gates_t = jnp.swapaxes(gates[..., :2 * ML_HEADS], 1, 2)
    gate_bias = jnp.concatenate([i_bias, f_bias])
    gbias_row = jnp.pad(gate_bias, (0, GATE_PAD - 2 * ML_HEADS))[None, :]
    y_ml = _mlstm_call(seq(ml_qk), seq(ml_v), seq(ml_o), gates, gates_t, conv_w, conv_b[None, :],
                       gbias_row, gate_bias[:, None], norm_g[None, :])

    out = _outproj_call(x2, y_sb.reshape(n, -1), y_sw.reshape(n, -1), y_ml.reshape(n, -1),
                        w_out.astype(BF16), ln_g[None, :], ln_b[None, :])
    return out.reshape(bsz, s_len, D_MODEL)


def kernel(x, mem, ffn1_w_in, ffn1_w_out, mix_w_in, ml_conv_w, ml_conv_b, ml_i_bias, ml_f_bias, ml_norm_g, swa_sinks, rel_bias, mix_w_out, xattn_w_q, xattn_w_kv, xattn_w_o, ffn2_w_in, ffn2_w_out, ln_g, ln_b):
    bsz, s_len, _ = x.shape
    n = bsz * s_len
    mem2 = mem.reshape(-1, D_MODEL)
    bias = _swa_bias(rel_bias)
    for l in range(DEPTH):
        g = lambda k: ln_g[l, k][None, :]
        b = lambda k: ln_b[l, k][None, :]
        x = _ffn_call(x.reshape(n, D_MODEL), ffn1_w_in[l].astype(BF16), ffn1_w_out[l].astype(BF16),
                      g(0), b(0)).reshape(bsz, s_len, D_MODEL)
        x = _hybrid_mixer(x, mix_w_in[l], ml_conv_w[l], ml_conv_b[l], ml_i_bias[l], ml_f_bias[l],
                          ml_norm_g[l], swa_sinks[l], bias, mix_w_out[l], ln_g[l, 1], ln_b[l, 1])
        kv = _kv_call(mem2, xattn_w_kv[l].astype(BF16)).reshape(bsz, -1, 2 * D_MODEL)
        x = _xattn_call(x, kv, xattn_w_q[l].astype(BF16), xattn_w_o[l].astype(BF16), g(2), b(2))
        x = _ffn_call(x.reshape(n, D_MODEL), ffn2_w_in[l].astype(BF16), ffn2_w_out[l].astype(BF16),
                      g(3), b(3)).reshape(bsz, s_len, D_MODEL)
    return x
```

```python
import functools

import numpy as np
import jax
import jax.numpy as jnp
from jax import lax
from jax.experimental import pallas as pl
from jax.experimental.pallas import tpu as pltpu

F32 = jnp.float32
BF16 = jnp.bfloat16

D_MODEL = 1024
DEPTH = 2
HEAD_DIM = 64
SB_HEADS = 4
SWA_HEADS = 8
SWA_KV_HEADS = 2
SWA_GROUP = SWA_HEADS // SWA_KV_HEADS
ML_HEADS = 4
SB_WIDTH = SB_HEADS * HEAD_DIM
SWA_WIDTH = SWA_HEADS * HEAD_DIM
SWA_KV_WIDTH = SWA_KV_HEADS * HEAD_DIM
ML_WIDTH = ML_HEADS * HEAD_DIM
BLOCK = 128
CONV_WIDTH = 4
NUM_BUCKETS = 32
MAX_DISTANCE = 128
XATTN_HEADS = 4
XATTN_HEAD_DIM = D_MODEL // XATTN_HEADS
D_FF = 2816
ALPHA = (2 * DEPTH) ** 0.25
LN_EPS = 1e-5
NEG_BIG = -1e30
SB_SKIP_BELOW = -104.0

_SB_Q, _SB_K, _SB_V = 0, 256, 512
_SW_Q, _SW_K, _SW_V = 768, 1280, 1408
_ML_QK, _ML_V, _ML_I, _ML_F, _ML_O = 1536, 2048, 2304, 2308, 2312
N_IN = 2568
GATE_PAD = 128

VMEM_LIMIT = 56 * 1024 * 1024

SEQ_BLOCKS_PER_STEP = 4
SB_ROWS_PER_STEP = 4
TOKEN_TILE = 512
FFN_TILE = 1024
FFN_SLAB = 512
FF_CHUNK = 256


def _params(*sem):
    return pltpu.CompilerParams(dimension_semantics=sem, vmem_limit_bytes=VMEM_LIMIT)


def _resident(shape):
    nd = len(shape)
    return pl.BlockSpec(shape, lambda *_: (0,) * nd, pipeline_mode=pl.Buffered(1))


def _layer_norm(y, g, b):
    mu = jnp.mean(y, axis=-1, keepdims=True)
    d = y - mu
    var = jnp.mean(d * d, axis=-1, keepdims=True)
    return d * lax.rsqrt(var + LN_EPS) * g + b


def _log_sigmoid(x):
    return jnp.minimum(x, 0.0) - jnp.log(1.0 + jnp.exp(-jnp.abs(x)))


def _sigmoid(x):
    return 1.0 / (1.0 + jnp.exp(-x))


def _dot(a, b):
    return jnp.dot(a, b, preferred_element_type=F32)


def _dot_nt(a, b):
    return lax.dot_general(a, b, (((1,), (1,)), ((), ())), preferred_element_type=F32)


def _dot_tn(a, b):
    return lax.dot_general(a, b, (((0,), (0,)), ((), ())), preferred_element_type=F32)


def _split_bf16(x, pieces):
    out = []
    r = x
    for _ in range(pieces - 1):
        p = r.astype(BF16)
        out.append(p)
        r = r - p.astype(F32)
    out.append(r.astype(BF16))
    return out


def _ffn_kernel(x_ref, win_ref, wout_ref, g_ref, b_ref, o_ref, acc_ref):
    for lo_row in range(0, x_ref.shape[0], FFN_SLAB):
        rows = slice(lo_row, lo_row + FFN_SLAB)
        xb = x_ref[rows, :].astype(BF16)
        for c in range(D_FF // FF_CHUNK):
            lo = c * FF_CHUNK
            a = _dot(xb, win_ref[:, lo:lo + FF_CHUNK])
            b = _dot(xb, win_ref[:, D_FF + lo:D_FF + lo + FF_CHUNK])
            h = (a * _sigmoid(a) * b).astype(BF16)
            part = _dot(h, wout_ref[lo:lo + FF_CHUNK, :])
            if c == 0:
                acc_ref[rows, :] = part
            else:
                acc_ref[rows, :] += part
        y = ALPHA * x_ref[rows, :] + 0.5 * acc_ref[rows, :]
        o_ref[rows, :] = _layer_norm(y, g_ref[...], b_ref[...])


def _ffn_call(x, w_in, w_out, g, b):
    n = x.shape[0]
    tm = min(FFN_TILE, n)
    row = lambda i: (i, 0)
    return pl.pallas_call(
        _ffn_kernel,
        grid=(n // tm,),
        in_specs=[pl.BlockSpec((tm, D_MODEL), row), _resident(w_in.shape), _resident(w_out.shape),
                  _resident(g.shape), _resident(b.shape)],
        out_specs=pl.BlockSpec((tm, D_MODEL), row),
        out_shape=jax.ShapeDtypeStruct((n, D_MODEL), F32),
        scratch_shapes=[pltpu.VMEM((tm, D_MODEL), F32)],
        compiler_params=_params("parallel"),
        name="ffn_ln",
    )(x, w_in, w_out, g, b)


_INPROJ_OUTS = (
    ("sb_q", 0, 256, BF16), ("sb_k", 256, 256, BF16), ("sb_v", 512, 256, BF16),
    ("sw_q", 768, 512, BF16), ("sw_kv", 1280, 256, BF16),
    ("ml_qk", 1536, 512, F32), ("ml_v", 2048, 256, BF16),
    ("ml_o", 2304, 256, F32), ("gates", 2560, GATE_PAD, F32),
)
N_IN_PADDED = 2688


def _inproj_kernel(x_ref, w_ref, *out_refs):
    xb = x_ref[...].astype(BF16)
    for (_, lo, width, dt), o_ref in zip(_INPROJ_OUTS, out_refs):
        o_ref[...] = _dot(xb, w_ref[:, lo:lo + width]).astype(dt)


def _inproj_call(x, w):
    n = x.shape[0]
    tm = min(TOKEN_TILE, n)
    row = lambda i: (i, 0)
    return pl.pallas_call(
        _inproj_kernel,
        grid=(n // tm,),
        in_specs=[pl.BlockSpec((tm, D_MODEL), row), _resident(w.shape)],
        out_specs=[pl.BlockSpec((tm, width), row) for _, _, width, _ in _INPROJ_OUTS],
        out_shape=[jax.ShapeDtypeStruct((n, width), dt) for _, _, width, dt in _INPROJ_OUTS],
        compiler_params=_params("parallel"),
        name="mix_in_proj",
    )(x, w)


def _sb_kernel(q_ref, k_ref, v_ref, o_ref, acc_ref, carry_ref):
    qi = pl.program_id(1)
    rows = q_ref.shape[0]
    lane_head = lax.broadcasted_iota(jnp.int32, (1, SB_WIDTH), 1) // HEAD_DIM
    heads = [lane_head == h for h in range(SB_HEADS)]
    q_heads = []
    for bi in range(rows):
        q = (q_ref[bi].astype(F32) * (HEAD_DIM ** -0.5)).astype(BF16)
        q_heads.append([jnp.where(head, q, jnp.zeros_like(q)) for head in heads])
    row = lax.broadcasted_iota(jnp.int32, (BLOCK, 2 * BLOCK), 0)
    col = lax.broadcasted_iota(jnp.int32, (BLOCK, 2 * BLOCK), 1)
    strict = (lax.broadcasted_iota(jnp.int32, (BLOCK, BLOCK), 1)
              < lax.broadcasted_iota(jnp.int32, (BLOCK, BLOCK), 0))
    suffix_and_total = jnp.where((row >= col) | (col >= BLOCK), 1.0, 0.0).astype(BF16)

    def step(j, diagonal):
        start = pl.multiple_of(j * BLOCK, BLOCK)
        largest = None
        for bi in range(rows):
            kb = k_ref[bi, pl.ds(start, BLOCK), :]
            vb = v_ref[bi, pl.ds(start, BLOCK), :]
            weights = []
            for h in range(SB_HEADS):
                z = _dot_nt(q_heads[bi][h], kb)
                lom = jnp.minimum(-z, 0.0) - jnp.log(1.0 + jnp.exp(-jnp.abs(z)))
                if diagonal:
                    lom = jnp.where(strict, lom, 0.0)
                hi, lo = _split_bf16(lom, 2)
                sums = _dot(hi, suffix_and_total) + _dot(lo, suffix_and_total)
                if diagonal:
                    w = jnp.where(strict, jnp.exp(z + sums[:, :BLOCK]), 0.0)
                    carry = sums[:, BLOCK:]
                else:
                    carry = carry_ref[bi * SB_HEADS + h]
                    w = jnp.exp(z + sums[:, :BLOCK] + carry)
                    carry = carry + sums[:, BLOCK:]
                carry_ref[bi * SB_HEADS + h] = carry
                largest = carry if largest is None else jnp.maximum(largest, carry)
                weights.append(w.astype(BF16))
            v_heads = jnp.concatenate([jnp.where(head, vb, jnp.zeros_like(vb)) for head in heads], axis=0)
            pv = _dot(jnp.concatenate(weights, axis=1), v_heads)
            if diagonal:
                acc_ref[bi] = pv
            else:
                acc_ref[bi] += pv
        return jnp.max(largest)

    def more_blocks(state):
        j, largest_carry = state
        return (j >= 0) & (largest_carry > SB_SKIP_BELOW)

    def next_block(state):
        j, _ = state
        return j - 1, step(j, False)

    lax.while_loop(more_blocks, next_block, (qi - 1, step(qi, True)))
    o_ref[...] = acc_ref[...].astype(o_ref.dtype)


def _sb_call(q, k, v):
    bsz, s_len, _ = q.shape
    rows = SB_ROWS_PER_STEP if bsz % SB_ROWS_PER_STEP == 0 else 1
    blk = pl.BlockSpec((rows, BLOCK, SB_WIDTH), lambda b, i: (b, i, 0))
    whole = pl.BlockSpec((rows, s_len, SB_WIDTH), lambda b, i: (b, 0, 0), pipeline_mode=pl.Buffered(1))
    return pl.pallas_call(
        _sb_kernel,
        grid=(bsz // rows, s_len // BLOCK),
        in_specs=[blk, whole, whole],
        out_specs=blk,
        out_shape=jax.ShapeDtypeStruct((bsz, s_len, SB_WIDTH), BF16),
        scratch_shapes=[pltpu.VMEM((rows, BLOCK, SB_WIDTH), F32),
                        pltpu.VMEM((rows * SB_HEADS, BLOCK, BLOCK), F32)],
        compiler_params=_params("parallel", "arbitrary"),
        name="stick_breaking",
    )(q, k, v)


def _swa_kernel(sinks_ref, q_ref, kvp_ref, kvc_ref, bias_ref, o_ref):
    first = jnp.minimum(pl.program_id(1), 1)
    for blk in range(q_ref.shape[1] // BLOCK):
        rows = slice(blk * BLOCK, (blk + 1) * BLOCK)
        q = (q_ref[0, rows, :].astype(F32) * (HEAD_DIM ** -0.5)).astype(BF16)
        if blk == 0:
            kv = jnp.concatenate([kvp_ref[0], kvc_ref[0, rows, :]], axis=0)
        else:
            kv = kvc_ref[0, (blk - 1) * BLOCK:(blk + 1) * BLOCK, :]
        outs = []
        for h in range(SWA_HEADS):
            g = h // SWA_GROUP
            qh = q[:, h * HEAD_DIM:(h + 1) * HEAD_DIM]
            kh = kv[:, g * HEAD_DIM:(g + 1) * HEAD_DIM]
            vh = kv[:, SWA_KV_WIDTH + g * HEAD_DIM:SWA_KV_WIDTH + (g + 1) * HEAD_DIM]
            logits = _dot_nt(qh, kh) + (bias_ref[first, h] if blk == 0 else bias_ref[1, h])
            sink = sinks_ref[h]
            m = jnp.maximum(jnp.max(logits, axis=-1, keepdims=True), sink)
            p = jnp.exp(logits - m)
            denom = jnp.sum(p, axis=-1, keepdims=True) + jnp.exp(sink - m)
            outs.append(_dot(p.astype(BF16), vh) / denom)
        o_ref[0, rows, :] = jnp.concatenate(outs, axis=-1).astype(o_ref.dtype)


def _swa_call(q, kv, bias, sinks):
    bsz, s_len, _ = q.shape
    nb = SEQ_BLOCKS_PER_STEP if (s_len // BLOCK) % SEQ_BLOCKS_PER_STEP == 0 else 1
    return pl.pallas_call(
        _swa_kernel,
        grid=(bsz, s_len // (nb * BLOCK)),
        in_specs=[pl.BlockSpec(memory_space=pltpu.SMEM),
                  pl.BlockSpec((1, nb * BLOCK, SWA_WIDTH), lambda b, i: (b, i, 0)),
                  pl.BlockSpec((1, BLOCK, 2 * SWA_KV_WIDTH), lambda b, i: (b, jnp.maximum(i * nb - 1, 0), 0)),
                  pl.BlockSpec((1, nb * BLOCK, 2 * SWA_KV_WIDTH), lambda b, i: (b, i, 0)),
                  _resident(bias.shape)],
        out_specs=pl.BlockSpec((1, nb * BLOCK, SWA_WIDTH), lambda b, i: (b, i, 0)),
        out_shape=jax.ShapeDtypeStruct((bsz, s_len, SWA_WIDTH), BF16),
        compiler_params=_params("parallel", "arbitrary"),
        name="sliding_window",
    )(sinks, q, kv, kv, bias)


def _t5_bucket(dist):
    max_exact = NUM_BUCKETS // 2
    d = np.maximum(dist, 1)
    large = max_exact + (np.log(d / max_exact) / np.log(MAX_DISTANCE / max_exact)
                         * (NUM_BUCKETS - max_exact)).astype(np.int32)
    large = np.minimum(large, NUM_BUCKETS - 1)
    return np.where(dist < max_exact, dist, large).astype(np.int32)


def _swa_bias(rel_bias):
    qi = np.arange(BLOCK)[:, None]
    kj = np.arange(2 * BLOCK)[None, :]
    dist = qi + BLOCK - kj
    bucket = jnp.asarray(_t5_bucket(np.clip(dist, 0, None)))[None]
    table = rel_bias.astype(F32)
    bias = jnp.zeros((SWA_HEADS, BLOCK, 2 * BLOCK), F32)
    for b in range(NUM_BUCKETS):
        bias = jnp.where(bucket == b, table[b][:, None, None], bias)
    in_window = (dist >= 0) & (dist < BLOCK)
    first = jnp.asarray(in_window & (kj >= BLOCK))[None]
    rest = jnp.asarray(in_window)[None]
    return jnp.stack([jnp.where(first, bias, NEG_BIG), jnp.where(rest, bias, NEG_BIG)])


def _mlstm_kernel(qk_ref, v_ref, og_ref, gates_ref, gates_t_ref, convw_ref, convb_ref,
                  gbias_row_ref, gbias_col_ref, normg_ref, y_ref,
                  xbuf_ref, c_ref, n_ref, m_ref):
    @pl.when(pl.program_id(1) == 0)
    def _():
        xbuf_ref[0:8, :] = jnp.zeros((8, 2 * ML_WIDTH), F32)
        c_ref[...] = jnp.zeros_like(c_ref)
        n_ref[...] = jnp.zeros_like(n_ref)
        m_ref[...] = jnp.zeros_like(m_ref)

    iota = lambda shape, axis: lax.broadcasted_iota(jnp.int32, shape, axis)
    row = iota((BLOCK, BLOCK), 0)
    col = iota((BLOCK, BLOCK), 1)
    causal = col <= row
    one_where = lambda cond: jnp.where(cond, 1.0, 0.0).astype(BF16)
    tri = one_where(causal)
    tri_t = one_where(row <= col)
    heads = [iota((1, ML_WIDTH), 1) // HEAD_DIM == h for h in range(ML_HEADS)]
    same_head = iota((ML_WIDTH, ML_WIDTH), 0) // HEAD_DIM == iota((ML_WIDTH, ML_WIDTH), 1) // HEAD_DIM
    head_mean = jnp.where(same_head, 1.0 / HEAD_DIM, 0.0).astype(BF16)
    sel_tile = one_where(iota((BLOCK, ML_HEADS * BLOCK), 0) == iota((BLOCK, ML_HEADS * BLOCK), 1) // BLOCK)
    sel_head = one_where(iota((BLOCK, ML_WIDTH), 0) == iota((BLOCK, ML_WIDTH), 1) // HEAD_DIM)
    rows_of_head = one_where(iota((ML_HEADS * BLOCK, ML_WIDTH), 0) // BLOCK
                             == iota((ML_HEADS * BLOCK, ML_WIDTH), 1) // HEAD_DIM)

    def spread(tile, selector):
        live = iota(tile.shape, 1) < ML_HEADS
        return sum(_dot(p, selector) for p in _split_bf16(jnp.where(live, tile, 0.0), 3))

    for blk in range(qk_ref.shape[1] // BLOCK):
        rows = slice(blk * BLOCK, (blk + 1) * BLOCK)
        xbuf_ref[8:8 + BLOCK, :] = qk_ref[0, rows, :]
        conv = convb_ref[...]
        for j in range(CONV_WIDTH):
            lo = 8 - (CONV_WIDTH - 1) + j
            conv = conv + convw_ref[j:j + 1, :] * xbuf_ref[lo:lo + BLOCK, :]
        xbuf_ref[0:8, :] = xbuf_ref[BLOCK:BLOCK + 8, :]
        act = conv * _sigmoid(conv)
        qb = act[:, :ML_WIDTH].astype(BF16)
        k_all = act[:, ML_WIDTH:] * (HEAD_DIM ** -0.5)
        kb = k_all.astype(BF16)
        v_all = v_ref[0, rows, :]

        gates = gates_ref[0, rows, :] + gbias_row_ref[...]
        gates_t = gates_t_ref[0, :, rows] + gbias_col_ref[...]
        b_cols = sum(_dot(tri, p) for p in _split_bf16(_log_sigmoid(gates), 3))
        b_rows = sum(_dot(p, tri_t) for p in _split_bf16(_log_sigmoid(gates_t), 3))
        b_tile = pltpu.roll(b_cols, BLOCK - ML_HEADS, axis=1)
        a_tile = gates - b_tile
        a_rows = gates_t[0:ML_HEADS, :] - b_rows[ML_HEADS:2 * ML_HEADS, :]

        run_max = a_tile
        shift = 1
        while shift < BLOCK:
            run_max = jnp.where(row >= shift, jnp.maximum(run_max, pltpu.roll(run_max, shift, axis=0)), run_max)
            shift *= 2
        m_prev = m_ref[0:1, :]
        u = jnp.maximum(m_prev, run_max)
        inter = jnp.exp(m_prev - u)
        floor = jnp.exp(-(b_tile + u))
        b_end = b_tile[BLOCK - 1:BLOCK, :]
        w_log = b_end - b_tile + gates
        m_new = jnp.maximum(b_end + m_prev, jnp.max(w_log, axis=0, keepdims=True))
        decay = jnp.exp(b_end + m_prev - m_new)
        w = jnp.exp(w_log - m_new)
        m_ref[...] = jnp.broadcast_to(m_new, m_ref.shape)

        neg_u = spread(-u, sel_tile)
        inter = spread(inter, sel_head)
        floor = spread(floor, sel_head)
        w = spread(w, sel_head)
        decay = spread(jnp.broadcast_to(decay, (8, BLOCK)), sel_head)[0:1, :]

        s_heads = []
        for h in range(ML_HEADS):
            log_d = jnp.where(causal, a_rows[h:h + 1, :] + neg_u[:, h * BLOCK:(h + 1) * BLOCK], NEG_BIG)
            qh = jnp.where(heads[h], qb, jnp.zeros_like(qb))
            s_heads.append((_dot_nt(qh, kb) * jnp.exp(log_d)).astype(BF16))
        s_cat = jnp.concatenate(s_heads, axis=1)
        v_heads = jnp.concatenate([jnp.where(head, v_all, jnp.zeros_like(v_all)) for head in heads], axis=0)

        c_prev = c_ref[...]
        n_prev = n_ref[...]
        num = inter * _dot(qb, c_prev.astype(BF16)) + _dot(s_cat, v_heads)
        den = inter * _dot(qb, n_prev.astype(BF16)) + _dot(s_cat, rows_of_head)
        hid = num / jnp.maximum(jnp.abs(den), floor)
        mu = sum(_dot(p, head_mean) for p in _split_bf16(hid, 2))
        dev = hid - mu
        var = sum(_dot(p, head_mean) for p in _split_bf16(dev * dev, 2))
        y = _sigmoid(og_ref[0, rows, :]) * (dev * lax.rsqrt(var + LN_EPS) * normg_ref[...])
        y_ref[0, rows, :] = y.astype(y_ref.dtype)

        wk = (w * k_all).astype(BF16)
        update = _dot_tn(wk, jnp.concatenate([v_all, jnp.ones_like(v_all)], axis=1))
        c_ref[...] = decay * c_prev + jnp.where(same_head, update[:, :ML_WIDTH], 0.0)
        n_ref[...] = decay * n_prev + jnp.where(same_head, update[:, ML_WIDTH:], 0.0)


def _mlstm_call(qk, v, og, gates, gates_t, conv_w, conv_b, gbias_row, gbias_col, norm_g):
    bsz, s_len, _ = qk.shape
    nb = SEQ_BLOCKS_PER_STEP if (s_len // BLOCK) % SEQ_BLOCKS_PER_STEP == 0 else 1
    tok = lambda w: pl.BlockSpec((1, nb * BLOCK, w), lambda b, c: (b, c, 0))
    return pl.pallas_call(
        _mlstm_kernel,
        grid=(bsz, s_len // (nb * BLOCK)),
        in_specs=[tok(2 * ML_WIDTH), tok(ML_WIDTH), tok(ML_WIDTH), tok(GATE_PAD),
                  pl.BlockSpec((1, 8, nb * BLOCK), lambda b, c: (b, 0, c)),
                  _resident(conv_w.shape), _resident(conv_b.shape), _resident(gbias_row.shape),
                  _resident(gbias_col.shape), _resident(norm_g.shape)],
        out_specs=tok(ML_WIDTH),
        out_shape=jax.ShapeDtypeStruct((bsz, s_len, ML_WIDTH), BF16),
        scratch_shapes=[pltpu.VMEM((8 + BLOCK, 2 * ML_WIDTH), F32),
                        pltpu.VMEM((ML_WIDTH, ML_WIDTH), F32),
                        pltpu.VMEM((ML_WIDTH, ML_WIDTH), F32),
                        pltpu.VMEM((8, 128), F32)],
        compiler_params=_params("parallel", "arbitrary"),
        name="mlstm",
    )(qk, v, og, gates, gates_t, conv_w, conv_b, gbias_row, gbias_col, norm_g)


def _kv_kernel(mem_ref, w_ref, o_ref):
    o_ref[...] = _dot(mem_ref[...].astype(BF16), w_ref[...]).astype(o_ref.dtype)


def _kv_call(mem, w):
    n = mem.shape[0]
    tm = min(256, n)
    return pl.pallas_call(
        _kv_kernel,
        grid=(n // tm,),
        in_specs=[pl.BlockSpec((tm, D_MODEL), lambda i: (i, 0)), _resident(w.shape)],
        out_specs=pl.BlockSpec((tm, 2 * D_MODEL), lambda i: (i, 0)),
        out_shape=jax.ShapeDtypeStruct((n, 2 * D_MODEL), BF16),
        compiler_params=_params("parallel"),
        name="xattn_kv_proj",
    )(mem, w)


def _mix_out_xattn_kernel(x_ref, ysb_ref, ysw_ref, yml_ref, wmix_ref, g1_ref, b1_ref,
                          kv_ref, wq_ref, wo_ref, g_ref, b_ref, o_ref):
    mix = (_dot(ysb_ref[0], wmix_ref[0:SB_WIDTH, :])
           + _dot(ysw_ref[0], wmix_ref[SB_WIDTH:SB_WIDTH + SWA_WIDTH, :])
           + _dot(yml_ref[0], wmix_ref[SB_WIDTH + SWA_WIDTH:, :]))
    x = _layer_norm(ALPHA * x_ref[0] + mix, g1_ref[...], b1_ref[...])
    q = _dot(x.astype(BF16), wq_ref[...]).astype(BF16)
    outs = []
    for h in range(XATTN_HEADS):
        lo = h * XATTN_HEAD_DIM
        kh = kv_ref[0, :, lo:lo + XATTN_HEAD_DIM]
        vh = kv_ref[0, :, D_MODEL + lo:D_MODEL + lo + XATTN_HEAD_DIM]
        s = _dot_nt(q[:, lo:lo + XATTN_HEAD_DIM], kh) * (XATTN_HEAD_DIM ** -0.5)
        p = jnp.exp(s - jnp.max(s, axis=-1, keepdims=True))
        denom = jnp.sum(p, axis=-1, keepdims=True)
        outs.append((_dot(p.astype(BF16), vh) / denom).astype(BF16))
    attn = jnp.concatenate(outs, axis=-1)
    y = ALPHA * x + _dot(attn, wo_ref[...])
    o_ref[0] = _layer_norm(y, g_ref[...], b_ref[...])


def _mix_out_xattn_call(x, y_sb, y_sw, y_ml, w_mix, g1, b1, kv, w_q, w_o, g2, b2):
    bsz, s_len, _ = x.shape
    tm = min(TOKEN_TILE, s_len)
    tok = lambda w: pl.BlockSpec((1, tm, w), lambda bi, i: (bi, i, 0))
    mix_params = [w_mix, g1, b1]
    xattn_params = [w_q, w_o, g2, b2]
    return pl.pallas_call(
        _mix_out_xattn_kernel,
        grid=(bsz, s_len // tm),
        in_specs=[tok(D_MODEL), tok(SB_WIDTH), tok(SWA_WIDTH), tok(ML_WIDTH)]
                 + [_resident(t.shape) for t in mix_params]
                 + [pl.BlockSpec((1, kv.shape[1], 2 * D_MODEL), lambda bi, i: (bi, 0, 0))]
                 + [_resident(t.shape) for t in xattn_params],
        out_specs=tok(D_MODEL),
        out_shape=jax.ShapeDtypeStruct((bsz, s_len, D_MODEL), F32),
        compiler_params=_params("parallel", "parallel"),
        name="mix_out_xattn_ln",
    )(x, y_sb, y_sw, y_ml, *mix_params, kv, *xattn_params)


def _mixer_heads(x, w_in, conv_w, conv_b, i_bias, f_bias, norm_g, sinks, bias):
    bsz, s_len, _ = x.shape
    n = bsz * s_len
    x2 = x.reshape(n, D_MODEL)
    w_perm = jnp.concatenate(
        [w_in[:, :_ML_I], w_in[:, _ML_O:], w_in[:, _ML_I:_ML_O],
         jnp.zeros((D_MODEL, GATE_PAD - 2 * ML_HEADS), w_in.dtype)], axis=1).astype(BF16)
    sb_q, sb_k, sb_v, sw_q, sw_kv, ml_qk, ml_v, ml_o, gates = _inproj_call(x2, w_perm)
    seq = lambda t: t.reshape(bsz, s_len, t.shape[-1])

    y_sb = _sb_call(seq(sb_q), seq(sb_k), seq(sb_v))
    y_sw = _swa_call(seq(sw_q), seq(sw_kv), bias, sinks)

    gates = seq(gates)
    gates_t = jnp.swapaxes(gates[..., :2 * ML_HEADS], 1, 2)
    gate_bias = jnp.concatenate([i_bias, f_bias])
    gbias_row = jnp.pad(gate_bias, (0, GATE_PAD - 2 * ML_HEADS))[None, :]
    y_ml = _mlstm_call(seq(ml_qk), seq(ml_v), seq(ml_o), gates, gates_t, conv_w, conv_b[None, :],
                       gbias_row, gate_bias[:, None], norm_g[None, :])
    return y_sb, y_sw, y_ml


def kernel(x, mem, ffn1_w_in, ffn1_w_out, mix_w_in, ml_conv_w, ml_conv_b, ml_i_bias, ml_f_bias, ml_norm_g, swa_sinks, rel_bias, mix_w_out, xattn_w_q, xattn_w_kv, xattn_w_o, ffn2_w_in, ffn2_w_out, ln_g, ln_b):
    bsz, s_len, _ = x.shape
    n = bsz * s_len
    mem2 = mem.reshape(-1, D_MODEL)
    bias = _swa_bias(rel_bias)
    for l in range(DEPTH):
        g = lambda k: ln_g[l, k][None, :]
        b = lambda k: ln_b[l, k][None, :]
        x = _ffn_call(x.reshape(n, D_MODEL), ffn1_w_in[l].astype(BF16), ffn1_w_out[l].astype(BF16),
                      g(0), b(0)).reshape(bsz, s_len, D_MODEL)
        y_sb, y_sw, y_ml = _mixer_heads(x, mix_w_in[l], ml_conv_w[l], ml_conv_b[l], ml_i_bias[l], ml_f_bias[l],
                                        ml_norm_g[l], swa_sinks[l], bias)
        kv = _kv_call(mem2, xattn_w_kv[l].astype(BF16)).reshape(bsz, -1, 2 * D_MODEL)
        x = _mix_out_xattn_call(x, y_sb, y_sw, y_ml, mix_w_out[l].astype(BF16), g(1), b(1),
                                kv, xattn_w_q[l].astype(BF16), xattn_w_o[l].astype(BF16), g(2), b(2))
        x = _ffn_call(x.reshape(n, D_MODEL), ffn2_w_in[l].astype(BF16), ffn2_w_out[l].astype(BF16),
                      g(3), b(3)).reshape(bsz, s_len, D_MODEL)
    return x
```

```python
import functools

import numpy as np
import jax
import jax.numpy as jnp
from jax import lax
from jax.experimental import pallas as pl
from jax.experimental.pallas import tpu as pltpu

F32 = jnp.float32
BF16 = jnp.bfloat16

D_MODEL = 1024
DEPTH = 2
HEAD_DIM = 64
SB_HEADS = 4
SWA_HEADS = 8
SWA_KV_HEADS = 2
SWA_GROUP = SWA_HEADS // SWA_KV_HEADS
ML_HEADS = 4
SB_WIDTH = SB_HEADS * HEAD_DIM
SWA_WIDTH = SWA_HEADS * HEAD_DIM
SWA_KV_WIDTH = SWA_KV_HEADS * HEAD_DIM
ML_WIDTH = ML_HEADS * HEAD_DIM
BLOCK = 128
CONV_WIDTH = 4
NUM_BUCKETS = 32
MAX_DISTANCE = 128
XATTN_HEADS = 4
XATTN_HEAD_DIM = D_MODEL // XATTN_HEADS
D_FF = 2816
ALPHA = (2 * DEPTH) ** 0.25
LN_EPS = 1e-5
NEG_BIG = -1e30
SB_SKIP_BELOW = -104.0

_SB_Q, _SB_K, _SB_V = 0, 256, 512
_SW_Q, _SW_K, _SW_V = 768, 1280, 1408
_ML_QK, _ML_V, _ML_I, _ML_F, _ML_O = 1536, 2048, 2304, 2308, 2312
N_IN = 2568
GATE_PAD = 128

VMEM_LIMIT = 56 * 1024 * 1024

SEQ_BLOCKS_PER_STEP = 4
SWA_BLOCKS_PER_STEP = 1
SB_ROWS_PER_STEP = 4
TOKEN_TILE = 512
FFN_TILE = 1024
FFN_SLAB = 512
FF_CHUNK = 256


def _params(*sem):
    return pltpu.CompilerParams(dimension_semantics=sem, vmem_limit_bytes=VMEM_LIMIT)


def _resident(shape):
    nd = len(shape)
    return pl.BlockSpec(shape, lambda *_: (0,) * nd, pipeline_mode=pl.Buffered(1))


def _layer_norm(y, g, b):
    mu = jnp.mean(y, axis=-1, keepdims=True)
    d = y - mu
    var = jnp.mean(d * d, axis=-1, keepdims=True)
    return d * lax.rsqrt(var + LN_EPS) * g + b


def _log_sigmoid(x):
    return jnp.minimum(x, 0.0) - jnp.log(1.0 + jnp.exp(-jnp.abs(x)))


def _sigmoid(x):
    return 1.0 / (1.0 + jnp.exp(-x))


def _dot(a, b):
    return jnp.dot(a, b, preferred_element_type=F32)


def _dot_nt(a, b):
    return lax.dot_general(a, b, (((1,), (1,)), ((), ())), preferred_element_type=F32)


def _dot_tn(a, b):
    return lax.dot_general(a, b, (((0,), (0,)), ((), ())), preferred_element_type=F32)


def _split_bf16(x, pieces):
    out = []
    r = x
    for _ in range(pieces - 1):
        p = r.astype(BF16)
        out.append(p)
        r = r - p.astype(F32)
    out.append(r.astype(BF16))
    return out


def _ffn_kernel(x_ref, win_ref, wout_ref, g_ref, b_ref, o_ref, acc_ref):
    for lo_row in range(0, x_ref.shape[0], FFN_SLAB):
        rows = slice(lo_row, lo_row + FFN_SLAB)
        xb = x_ref[rows, :].astype(BF16)
        for c in range(D_FF // FF_CHUNK):
            lo = c * FF_CHUNK
            a = _dot(xb, win_ref[:, lo:lo + FF_CHUNK])
            b = _dot(xb, win_ref[:, D_FF + lo:D_FF + lo + FF_CHUNK])
            h = (a * _sigmoid(a) * b).astype(BF16)
            part = _dot(h, wout_ref[lo:lo + FF_CHUNK, :])
            if c == 0:
                acc_ref[rows, :] = part
            else:
                acc_ref[rows, :] += part
        y = ALPHA * x_ref[rows, :] + 0.5 * acc_ref[rows, :]
        o_ref[rows, :] = _layer_norm(y, g_ref[...], b_ref[...])


def _ffn_call(x, w_in, w_out, g, b):
    n = x.shape[0]
    tm = min(FFN_TILE, n)
    row = lambda i: (i, 0)
    return pl.pallas_call(
        _ffn_kernel,
        grid=(n // tm,),
        in_specs=[pl.BlockSpec((tm, D_MODEL), row), _resident(w_in.shape), _resident(w_out.shape),
                  _resident(g.shape), _resident(b.shape)],
        out_specs=pl.BlockSpec((tm, D_MODEL), row),
        out_shape=jax.ShapeDtypeStruct((n, D_MODEL), F32),
        scratch_shapes=[pltpu.VMEM((tm, D_MODEL), F32)],
        compiler_params=_params("parallel"),
        name="ffn_ln",
    )(x, w_in, w_out, g, b)


_INPROJ_OUTS = (
    ("sb_q", 0, 256, BF16), ("sb_k", 256, 256, BF16), ("sb_v", 512, 256, BF16),
    ("sw_q", 768, 512, BF16), ("sw_kv", 1280, 256, BF16),
    ("ml_qk", 1536, 512, F32), ("ml_v", 2048, 256, BF16),
    ("ml_o", 2304, 256, F32), ("gates", 2560, GATE_PAD, F32),
)
N_IN_PADDED = 2688


def _inproj_kernel(x_ref, w_ref, *out_refs):
    xb = x_ref[...].astype(BF16)
    for (_, lo, width, dt), o_ref in zip(_INPROJ_OUTS, out_refs):
        o_ref[...] = _dot(xb, w_ref[:, lo:lo + width]).astype(dt)


def _inproj_call(x, w):
    n = x.shape[0]
    tm = min(TOKEN_TILE, n)
    row = lambda i: (i, 0)
    return pl.pallas_call(
        _inproj_kernel,
        grid=(n // tm,),
        in_specs=[pl.BlockSpec((tm, D_MODEL), row), _resident(w.shape)],
        out_specs=[pl.BlockSpec((tm, width), row) for _, _, width, _ in _INPROJ_OUTS],
        out_shape=[jax.ShapeDtypeStruct((n, width), dt) for _, _, width, dt in _INPROJ_OUTS],
        compiler_params=_params("parallel"),
        name="mix_in_proj",
    )(x, w)


def _sb_kernel(q_ref, k_ref, v_ref, o_ref, acc_ref, carry_ref):
    qi = pl.program_id(1)
    rows = q_ref.shape[0]
    lane_head = lax.broadcasted_iota(jnp.int32, (1, SB_WIDTH), 1) // HEAD_DIM
    heads = [lane_head == h for h in range(SB_HEADS)]
    q_stacks = []
    for bi in range(rows):
        q = (q_ref[bi].astype(F32) * (HEAD_DIM ** -0.5)).astype(BF16)
        q_stacks.append(jnp.concatenate([jnp.where(head, q, jnp.zeros_like(q)) for head in heads], axis=0))
    tall = rows * SB_HEADS * BLOCK
    strict = (lax.broadcasted_iota(jnp.int32, (tall, BLOCK), 1)
              < lax.broadcasted_iota(jnp.int32, (tall, BLOCK), 0) % BLOCK)
    row = lax.broadcasted_iota(jnp.int32, (2 * BLOCK, 2 * BLOCK), 0) % BLOCK
    col = lax.broadcasted_iota(jnp.int32, (2 * BLOCK, 2 * BLOCK), 1)
    suffix_and_total = jnp.where((row >= col) | (col >= BLOCK), 1.0, 0.0).astype(BF16)

    def step(j, diagonal):
        start = pl.multiple_of(j * BLOCK, BLOCK)
        z = jnp.concatenate([_dot_nt(q_stacks[bi], k_ref[bi, pl.ds(start, BLOCK), :]) for bi in range(rows)],
                            axis=0)
        lom = jnp.minimum(-z, 0.0) - jnp.log(1.0 + jnp.exp(-jnp.abs(z)))
        if diagonal:
            lom = jnp.where(strict, lom, 0.0)
        sums = _dot(jnp.concatenate(_split_bf16(lom, 2), axis=1), suffix_and_total)
        if diagonal:
            w = jnp.where(strict, jnp.exp(z + sums[:, :BLOCK]), 0.0)
            carry = sums[:, BLOCK:]
        else:
            carry = carry_ref[...]
            w = jnp.exp(z + sums[:, :BLOCK] + carry)
            carry = carry + sums[:, BLOCK:]
        carry_ref[...] = carry
        w = w.astype(BF16)
        for bi in range(rows):
            vb = v_ref[bi, pl.ds(start, BLOCK), :]
            v_heads = jnp.concatenate([jnp.where(head, vb, jnp.zeros_like(vb)) for head in heads], axis=0)
            tiles = [w[(bi * SB_HEADS + h) * BLOCK:(bi * SB_HEADS + h + 1) * BLOCK, :] for h in range(SB_HEADS)]
            pv = _dot(jnp.concatenate(tiles, axis=1), v_heads)
            if diagonal:
                acc_ref[bi] = pv
            else:
                acc_ref[bi] += pv
        return jnp.max(carry)

    def more_blocks(state):
        j, largest_carry = state
        return (j >= 0) & (largest_carry > SB_SKIP_BELOW)

    def next_block(state):
        j, _ = state
        return j - 1, step(j, False)

    lax.while_loop(more_blocks, next_block, (qi - 1, step(qi, True)))
    o_ref[...] = acc_ref[...].astype(o_ref.dtype)


def _sb_call(q, k, v):
    bsz, s_len, _ = q.shape
    rows = SB_ROWS_PER_STEP if bsz % SB_ROWS_PER_STEP == 0 else 1
    blk = pl.BlockSpec((rows, BLOCK, SB_WIDTH), lambda b, i: (b, i, 0))
    whole = pl.BlockSpec((rows, s_len, SB_WIDTH), lambda b, i: (b, 0, 0), pipeline_mode=pl.Buffered(1))
    return pl.pallas_call(
        _sb_kernel,
        grid=(bsz // rows, s_len // BLOCK),
        in_specs=[blk, whole, whole],
        out_specs=blk,
        out_shape=jax.ShapeDtypeStruct((bsz, s_len, SB_WIDTH), BF16),
        scratch_shapes=[pltpu.VMEM((rows, BLOCK, SB_WIDTH), F32),
                        pltpu.VMEM((rows * SB_HEADS * BLOCK, BLOCK), F32)],
        compiler_params=_params("parallel", "arbitrary"),
        name="stick_breaking",
    )(q, k, v)


def _swa_kernel(sinks_ref, q_ref, kvp_ref, kvc_ref, bias_ref, o_ref):
    first = jnp.minimum(pl.program_id(1), 1)
    for blk in range(q_ref.shape[1] // BLOCK):
        rows = slice(blk * BLOCK, (blk + 1) * BLOCK)
        q = (q_ref[0, rows, :].astype(F32) * (HEAD_DIM ** -0.5)).astype(BF16)
        if blk == 0:
            kv = jnp.concatenate([kvp_ref[0], kvc_ref[0, rows, :]], axis=0)
        else:
            kv = kvc_ref[0, (blk - 1) * BLOCK:(blk + 1) * BLOCK, :]
        outs = []
        for h in range(SWA_HEADS):
            g = h // SWA_GROUP
            qh = q[:, h * HEAD_DIM:(h + 1) * HEAD_DIM]
            kh = kv[:, g * HEAD_DIM:(g + 1) * HEAD_DIM]
            vh = kv[:, SWA_KV_WIDTH + g * HEAD_DIM:SWA_KV_WIDTH + (g + 1) * HEAD_DIM]
            logits = _dot_nt(qh, kh) + (bias_ref[first, h] if blk == 0 else bias_ref[1, h])
            sink = sinks_ref[h]
            m = jnp.maximum(jnp.max(logits, axis=-1, keepdims=True), sink)
            p = jnp.exp(logits - m)
            denom = jnp.sum(p, axis=-1, keepdims=True) + jnp.exp(sink - m)
            outs.append(_dot(p.astype(BF16), vh) / denom)
        o_ref[0, rows, :] = jnp.concatenate(outs, axis=-1).astype(o_ref.dtype)


def _swa_call(q, kv, bias, sinks):
    bsz, s_len, _ = q.shape
    nb = SWA_BLOCKS_PER_STEP
    return pl.pallas_call(
        _swa_kernel,
        grid=(bsz, s_len // (nb * BLOCK)),
        in_specs=[pl.BlockSpec(memory_space=pltpu.SMEM),
                  pl.BlockSpec((1, nb * BLOCK, SWA_WIDTH), lambda b, i: (b, i, 0)),
                  pl.BlockSpec((1, BLOCK, 2 * SWA_KV_WIDTH), lambda b, i: (b, jnp.maximum(i * nb - 1, 0), 0)),
                  pl.BlockSpec((1, nb * BLOCK, 2 * SWA_KV_WIDTH), lambda b, i: (b, i, 0)),
                  _resident(bias.shape)],
        out_specs=pl.BlockSpec((1, nb * BLOCK, SWA_WIDTH), lambda b, i: (b, i, 0)),
        out_shape=jax.ShapeDtypeStruct((bsz, s_len, SWA_WIDTH), BF16),
        compiler_params=_params("parallel", "arbitrary"),
        name="sliding_window",
    )(sinks, q, kv, kv, bias)


def _t5_bucket(dist):
    max_exact = NUM_BUCKETS // 2
    d = np.maximum(dist, 1)
    large = max_exact + (np.log(d / max_exact) / np.log(MAX_DISTANCE / max_exact)
                         * (NUM_BUCKETS - max_exact)).astype(np.int32)
    large = np.minimum(large, NUM_BUCKETS - 1)
    return np.where(dist < max_exact, dist, large).astype(np.int32)


def _swa_bias(rel_bias):
    qi = np.arange(BLOCK)[:, None]
    kj = np.arange(2 * BLOCK)[None, :]
    dist = qi + BLOCK - kj
    bucket = jnp.asarray(_t5_bucket(np.clip(dist, 0, None)))[None]
    table = rel_bias.astype(F32)
    bias = jnp.zeros((SWA_HEADS, BLOCK, 2 * BLOCK), F32)
    for b in range(NUM_BUCKETS):
        bias = jnp.where(bucket == b, table[b][:, None, None], bias)
    in_window = (dist >= 0) & (dist < BLOCK)
    first = jnp.asarray(in_window & (kj >= BLOCK))[None]
    rest = jnp.asarray(in_window)[None]
    return jnp.stack([jnp.where(first, bias, NEG_BIG), jnp.where(rest, bias, NEG_BIG)])


def _mlstm_kernel(qk_ref, v_ref, og_ref, gates_ref, gates_t_ref, convw_ref, convb_ref,
                  gbias_row_ref, gbias_col_ref, normg_ref, y_ref,
                  xbuf_ref, c_ref, n_ref, m_ref):
    @pl.when(pl.program_id(1) == 0)
    def _():
        xbuf_ref[0:8, :] = jnp.zeros((8, 2 * ML_WIDTH), F32)
        c_ref[...] = jnp.zeros_like(c_ref)
        n_ref[...] = jnp.zeros_like(n_ref)
        m_ref[...] = jnp.zeros_like(m_ref)

    iota = lambda shape, axis: lax.broadcasted_iota(jnp.int32, shape, axis)
    row = iota((BLOCK, BLOCK), 0)
    col = iota((BLOCK, BLOCK), 1)
    causal = col <= row
    one_where = lambda cond: jnp.where(cond, 1.0, 0.0).astype(BF16)
    tri = one_where(causal)
    tri_t = one_where(row <= col)
    heads = [iota((1, ML_WIDTH), 1) // HEAD_DIM == h for h in range(ML_HEADS)]
    same_head = iota((ML_WIDTH, ML_WIDTH), 0) // HEAD_DIM == iota((ML_WIDTH, ML_WIDTH), 1) // HEAD_DIM
    head_mean = jnp.where(same_head, 1.0 / HEAD_DIM, 0.0).astype(BF16)
    sel_tile = one_where(iota((BLOCK, ML_HEADS * BLOCK), 0) == iota((BLOCK, ML_HEADS * BLOCK), 1) // BLOCK)
    sel_head = one_where(iota((BLOCK, ML_WIDTH), 0) == iota((BLOCK, ML_WIDTH), 1) // HEAD_DIM)
    rows_of_head = one_where(iota((ML_HEADS * BLOCK, ML_WIDTH), 0) // BLOCK
                             == iota((ML_HEADS * BLOCK, ML_WIDTH), 1) // HEAD_DIM)

    def spread(tile, selector):
        live = iota(tile.shape, 1) < ML_HEADS
        return sum(_dot(p, selector) for p in _split_bf16(jnp.where(live, tile, 0.0), 3))

    for blk in range(qk_ref.shape[1] // BLOCK):
        rows = slice(blk * BLOCK, (blk + 1) * BLOCK)
        xbuf_ref[8:8 + BLOCK, :] = qk_ref[0, rows, :]
        conv = convb_ref[...]
        for j in range(CONV_WIDTH):
            lo = 8 - (CONV_WIDTH - 1) + j
            conv = conv + convw_ref[j:j + 1, :] * xbuf_ref[lo:lo + BLOCK, :]
        xbuf_ref[0:8, :] = xbuf_ref[BLOCK:BLOCK + 8, :]
        act = conv * _sigmoid(conv)
        qb = act[:, :ML_WIDTH].astype(BF16)
        k_all = act[:, ML_WIDTH:] * (HEAD_DIM ** -0.5)
        kb = k_all.astype(BF16)
        v_all = v_ref[0, rows, :]

        gates = gates_ref[0, rows, :] + gbias_row_ref[...]
        gates_t = gates_t_ref[0, :, rows] + gbias_col_ref[...]
        b_cols = sum(_dot(tri, p) for p in _split_bf16(_log_sigmoid(gates), 3))
        b_rows = sum(_dot(p, tri_t) for p in _split_bf16(_log_sigmoid(gates_t), 3))
        b_tile = pltpu.roll(b_cols, BLOCK - ML_HEADS, axis=1)
        a_tile = gates - b_tile
        a_rows = gates_t[0:ML_HEADS, :] - b_rows[ML_HEADS:2 * ML_HEADS, :]

        run_max = a_tile
        shift = 1
        while shift < BLOCK:
            run_max = jnp.where(row >= shift, jnp.maximum(run_max, pltpu.roll(run_max, shift, axis=0)), run_max)
            shift *= 2
        m_prev = m_ref[0:1, :]
        u = jnp.maximum(m_prev, run_max)
        inter = jnp.exp(m_prev - u)
        floor = jnp.exp(-(b_tile + u))
        b_end = b_tile[BLOCK - 1:BLOCK, :]
        w_log = b_end - b_tile + gates
        m_new = jnp.maximum(b_end + m_prev, jnp.max(w_log, axis=0, keepdims=True))
        decay = jnp.exp(b_end + m_prev - m_new)
        w = jnp.exp(w_log - m_new)
        m_ref[...] = jnp.broadcast_to(m_new, m_ref.shape)

        neg_u = spread(-u, sel_tile)
        inter = spread(inter, sel_head)
        floor = spread(floor, sel_head)
        w = spread(w, sel_head)
        decay = spread(jnp.broadcast_to(decay, (8, BLOCK)), sel_head)[0:1, :]

        s_heads = []
        for h in range(ML_HEADS):
            log_d = jnp.where(causal, a_rows[h:h + 1, :] + neg_u[:, h * BLOCK:(h + 1) * BLOCK], NEG_BIG)
            qh = jnp.where(heads[h], qb, jnp.zeros_like(qb))
            s_heads.append((_dot_nt(qh, kb) * jnp.exp(log_d)).astype(BF16))
        s_cat = jnp.concatenate(s_heads, axis=1)
        v_heads = jnp.concatenate([jnp.where(head, v_all, jnp.zeros_like(v_all)) for head in heads], axis=0)

        c_prev = c_ref[...]
        n_prev = n_ref[...]
        num = inter * _dot(qb, c_prev.astype(BF16)) + _dot(s_cat, v_heads)
        den = inter * _dot(qb, n_prev.astype(BF16)) + _dot(s_cat, rows_of_head)
        hid = num / jnp.maximum(jnp.abs(den), floor)
        mu = sum(_dot(p, head_mean) for p in _split_bf16(hid, 2))
        dev = hid - mu
        var = sum(_dot(p, head_mean) for p in _split_bf16(dev * dev, 2))
        y = _sigmoid(og_ref[0, rows, :]) * (dev * lax.rsqrt(var + LN_EPS) * normg_ref[...])
        y_ref[0, rows, :] = y.astype(y_ref.dtype)

        wk = (w * k_all).astype(BF16)
        update = _dot_tn(wk, jnp.concatenate([v_all, jnp.ones_like(v_all)], axis=1))
        c_ref[...] = decay * c_prev + jnp.where(same_head, update[:, :ML_WIDTH], 0.0)
        n_ref[...] = decay * n_prev + jnp.where(same_head, update[:, ML_WIDTH:], 0.0)


def _mlstm_call(qk, v, og, gates, gates_t, conv_w, conv_b, gbias_row, gbias_col, norm_g):
    bsz, s_len, _ = qk.shape
    nb = SEQ_BLOCKS_PER_STEP if (s_len // BLOCK) % SEQ_BLOCKS_PER_STEP == 0 else 1
    tok = lambda w: pl.BlockSpec((1, nb * BLOCK, w), lambda b, c: (b, c, 0))
    return pl.pallas_call(
        _mlstm_kernel,
        grid=(bsz, s_len // (nb * BLOCK)),
        in_specs=[tok(2 * ML_WIDTH), tok(ML_WIDTH), tok(ML_WIDTH), tok(GATE_PAD),
                  pl.BlockSpec((1, 8, nb * BLOCK), lambda b, c: (b, 0, c)),
                  _resident(conv_w.shape), _resident(conv_b.shape), _resident(gbias_row.shape),
                  _resident(gbias_col.shape), _resident(norm_g.shape)],
        out_specs=tok(ML_WIDTH),
        out_shape=jax.ShapeDtypeStruct((bsz, s_len, ML_WIDTH), BF16),
        scratch_shapes=[pltpu.VMEM((8 + BLOCK, 2 * ML_WIDTH), F32),
                        pltpu.VMEM((ML_WIDTH, ML_WIDTH), F32),
                        pltpu.VMEM((ML_WIDTH, ML_WIDTH), F32),
                        pltpu.VMEM((8, 128), F32)],
        compiler_params=_params("parallel", "arbitrary"),
        name="mlstm",
    )(qk, v, og, gates, gates_t, conv_w, conv_b, gbias_row, gbias_col, norm_g)


def _kv_kernel(mem_ref, w_ref, o_ref):
    o_ref[...] = _dot(mem_ref[...].astype(BF16), w_ref[...]).astype(o_ref.dtype)


def _kv_call(mem, w):
    n = mem.shape[0]
    tm = min(256, n)
    return pl.pallas_call(
        _kv_kernel,
        grid=(n // tm,),
        in_specs=[pl.BlockSpec((tm, D_MODEL), lambda i: (i, 0)), _resident(w.shape)],
        out_specs=pl.BlockSpec((tm, 2 * D_MODEL), lambda i: (i, 0)),
        out_shape=jax.ShapeDtypeStruct((n, 2 * D_MODEL), BF16),
        compiler_params=_params("parallel"),
        name="xattn_kv_proj",
    )(mem, w)


def _mix_out_xattn_kernel(x_ref, ysb_ref, ysw_ref, yml_ref, wmix_ref, g1_ref, b1_ref,
                          kv_ref, wq_ref, wo_ref, g_ref, b_ref, o_ref):
    mix = (_dot(ysb_ref[0], wmix_ref[0:SB_WIDTH, :])
           + _dot(ysw_ref[0], wmix_ref[SB_WIDTH:SB_WIDTH + SWA_WIDTH, :])
           + _dot(yml_ref[0], wmix_ref[SB_WIDTH + SWA_WIDTH:, :]))
    x = _layer_norm(ALPHA * x_ref[0] + mix, g1_ref[...], b1_ref[...])
    q = _dot(x.astype(BF16), wq_ref[...]).astype(BF16)
    outs = []
    for h in range(XATTN_HEADS):
        lo = h * XATTN_HEAD_DIM
        kh = kv_ref[0, :, lo:lo + XATTN_HEAD_DIM]
        vh = kv_ref[0, :, D_MODEL + lo:D_MODEL + lo + XATTN_HEAD_DIM]
        s = _dot_nt(q[:, lo:lo + XATTN_HEAD_DIM], kh) * (XATTN_HEAD_DIM ** -0.5)
        p = jnp.exp(s - jnp.max(s, axis=-1, keepdims=True))
        denom = jnp.sum(p, axis=-1, keepdims=True)
        outs.append((_dot(p.astype(BF16), vh) / denom).astype(BF16))
    attn = jnp.concatenate(outs, axis=-1)
    y = ALPHA * x + _dot(attn, wo_ref[...])
    o_ref[0] = _layer_norm(y, g_ref[...], b_ref[...])


def _mix_out_xattn_call(x, y_sb, y_sw, y_ml, w_mix, g1, b1, kv, w_q, w_o, g2, b2):
    bsz, s_len, _ = x.shape
    tm = min(TOKEN_TILE, s_len)
    tok = lambda w: pl.BlockSpec((1, tm, w), lambda bi, i: (bi, i, 0))
    mix_params = [w_mix, g1, b1]
    xattn_params = [w_q, w_o, g2, b2]
    return pl.pallas_call(
        _mix_out_xattn_kernel,
        grid=(bsz, s_len // tm),
        in_specs=[tok(D_MODEL), tok(SB_WIDTH), tok(SWA_WIDTH), tok(ML_WIDTH)]
                 + [_resident(t.shape) for t in mix_params]
                 + [pl.BlockSpec((1, kv.shape[1], 2 * D_MODEL), lambda bi, i: (bi, 0, 0))]
                 + [_resident(t.shape) for t in xattn_params],
        out_specs=tok(D_MODEL),
        out_shape=jax.ShapeDtypeStruct((bsz, s_len, D_MODEL), F32),
        compiler_params=_params("parallel", "parallel"),
        name="mix_out_xattn_ln",
    )(x, y_sb, y_sw, y_ml, *mix_params, kv, *xattn_params)


def _mixer_heads(x, w_in, conv_w, conv_b, i_bias, f_bias, norm_g, sinks, bias):
    bsz, s_len, _ = x.shape
    n = bsz * s_len
    x2 = x.reshape(n, D_MODEL)
    w_perm = jnp.concatenate(
        [w_in[:, :_ML_I], w_in[:, _ML_O:], w_in[:, _ML_I:_ML_O],
         jnp.zeros((D_MODEL, GATE_PAD - 2 * ML_HEADS), w_in.dtype)], axis=1).astype(BF16)
    sb_q, sb_k, sb_v, sw_q, sw_kv, ml_qk, ml_v, ml_o, gates = _inproj_call(x2, w_perm)
    seq = lambda t: t.reshape(bsz, s_len, t.shape[-1])

    y_sb = _sb_call(seq(sb_q), seq(sb_k), seq(sb_v))
    y_sw = _swa_call(seq(sw_q), seq(sw_kv), bias, sinks)

    gates = seq(gates)
    gates_t = jnp.swapaxes(gates[..., :2 * ML_HEADS], 1, 2)
    gate_bias = jnp.concatenate([i_bias, f_bias])
    gbias_row = jnp.pad(gate_bias, (0, GATE_PAD - 2 * ML_HEADS))[None, :]
    y_ml = _mlstm_call(seq(ml_qk), seq(ml_v), seq(ml_o), gates, gates_t, conv_w, conv_b[None, :],
                       gbias_row, gate_bias[:, None], norm_g[None, :])
    return y_sb, y_sw, y_ml


def kernel(x, mem, ffn1_w_in, ffn1_w_out, mix_w_in, ml_conv_w, ml_conv_b, ml_i_bias, ml_f_bias, ml_norm_g, swa_sinks, rel_bias, mix_w_out, xattn_w_q, xattn_w_kv, xattn_w_o, ffn2_w_in, ffn2_w_out, ln_g, ln_b):
    bsz, s_len, _ = x.shape
    n = bsz * s_len
    mem2 = mem.reshape(-1, D_MODEL)
    bias = _swa_bias(rel_bias)
    for l in range(DEPTH):
        g = lambda k: ln_g[l, k][None, :]
        b = lambda k: ln_b[l, k][None, :]
        x = _ffn_call(x.reshape(n, D_MODEL), ffn1_w_in[l].astype(BF16), ffn1_w_out[l].astype(BF16),
                      g(0), b(0)).reshape(bsz, s_len, D_MODEL)
        y_sb, y_sw, y_ml = _mixer_heads(x, mix_w_in[l], ml_conv_w[l], ml_conv_b[l], ml_i_bias[l], ml_f_bias[l],
                                        ml_norm_g[l], swa_sinks[l], bias)
        kv = _kv_call(mem2, xattn_w_kv[l].astype(BF16)).reshape(bsz, -1, 2 * D_MODEL)
        x = _mix_out_xattn_call(x, y_sb, y_sw, y_ml, mix_w_out[l].astype(BF16), g(1), b(1),
                                kv, xattn_w_q[l].astype(BF16), xattn_w_o[l].astype(BF16), g(2), b(2))
        x = _ffn_call(x.reshape(n, D_MODEL), ffn2_w_in[l].astype(BF16), ffn2_w_out[l].astype(BF16),
                      g(3), b(3)).reshape(bsz, s_len, D_MODEL)
    return x
```

```python
import functools

import numpy as np
import jax
import jax.numpy as jnp
from jax import lax
from jax.experimental import pallas as pl
from jax.experimental.pallas import tpu as pltpu

F32 = jnp.float32
BF16 = jnp.bfloat16

D_MODEL = 1024
DEPTH = 2
HEAD_DIM = 64
SB_HEADS = 4
SWA_HEADS = 8
SWA_KV_HEADS = 2
SWA_GROUP = SWA_HEADS // SWA_KV_HEADS
ML_HEADS = 4
SB_WIDTH = SB_HEADS * HEAD_DIM
SWA_WIDTH = SWA_HEADS * HEAD_DIM
SWA_KV_WIDTH = SWA_KV_HEADS * HEAD_DIM
ML_WIDTH = ML_HEADS * HEAD_DIM
BLOCK = 128
CONV_WIDTH = 4
NUM_BUCKETS = 32
MAX_DISTANCE = 128
XATTN_HEADS = 4
XATTN_HEAD_DIM = D_MODEL // XATTN_HEADS
D_FF = 2816
ALPHA = (2 * DEPTH) ** 0.25
LN_EPS = 1e-5
NEG_BIG = -1e30
SB_SKIP_BELOW = -104.0

_SB_Q, _SB_K, _SB_V = 0, 256, 512
_SW_Q, _SW_K, _SW_V = 768, 1280, 1408
_ML_QK, _ML_V, _ML_I, _ML_F, _ML_O = 1536, 2048, 2304, 2308, 2312
N_IN = 2568
GATE_PAD = 128

VMEM_LIMIT = 56 * 1024 * 1024

SEQ_BLOCKS_PER_STEP = 4
SB_ROWS_PER_STEP = 4
TOKEN_TILE = 512
FF_CHUNK = 256


def _params(*sem):
    return pltpu.CompilerParams(dimension_semantics=sem, vmem_limit_bytes=VMEM_LIMIT)


def _resident(shape):
    nd = len(shape)
    return pl.BlockSpec(shape, lambda *_: (0,) * nd, pipeline_mode=pl.Buffered(1))


def _layer_norm(y, g, b):
    mu = jnp.mean(y, axis=-1, keepdims=True)
    d = y - mu
    var = jnp.mean(d * d, axis=-1, keepdims=True)
    return d * lax.rsqrt(var + LN_EPS) * g + b


def _log_sigmoid(x):
    return jnp.minimum(x, 0.0) - jnp.log(1.0 + jnp.exp(-jnp.abs(x)))


def _sigmoid(x):
    return 1.0 / (1.0 + jnp.exp(-x))


def _dot(a, b):
    return jnp.dot(a, b, preferred_element_type=F32)


def _dot_nt(a, b):
    return lax.dot_general(a, b, (((1,), (1,)), ((), ())), preferred_element_type=F32)


def _dot_tn(a, b):
    return lax.dot_general(a, b, (((0,), (0,)), ((), ())), preferred_element_type=F32)


def _split_bf16(x, pieces):
    out = []
    r = x
    for _ in range(pieces - 1):
        p = r.astype(BF16)
        out.append(p)
        r = r - p.astype(F32)
    out.append(r.astype(BF16))
    return out


def _ffn_kernel(x_ref, win_ref, wout_ref, g_ref, b_ref, o_ref, acc_ref):
    xb = x_ref[...].astype(BF16)
    for c in range(D_FF // FF_CHUNK):
        lo = c * FF_CHUNK
        a = _dot(xb, win_ref[:, lo:lo + FF_CHUNK].astype(BF16))
        b = _dot(xb, win_ref[:, D_FF + lo:D_FF + lo + FF_CHUNK].astype(BF16))
        h = (a * _sigmoid(a) * b).astype(BF16)
        part = _dot(h, wout_ref[lo:lo + FF_CHUNK, :].astype(BF16))
        if c == 0:
            acc_ref[...] = part
        else:
            acc_ref[...] += part
    y = ALPHA * x_ref[...] + 0.5 * acc_ref[...]
    o_ref[...] = _layer_norm(y, g_ref[...], b_ref[...])


def _ffn_call(x, w_in, w_out, g, b):
    n = x.shape[0]
    tm = min(TOKEN_TILE, n)
    row = lambda i: (i, 0)
    return pl.pallas_call(
        _ffn_kernel,
        grid=(n // tm,),
        in_specs=[pl.BlockSpec((tm, D_MODEL), row), _resident(w_in.shape), _resident(w_out.shape),
                  _resident(g.shape), _resident(b.shape)],
        out_specs=pl.BlockSpec((tm, D_MODEL), row),
        out_shape=jax.ShapeDtypeStruct((n, D_MODEL), F32),
        scratch_shapes=[pltpu.VMEM((tm, D_MODEL), F32)],
        compiler_params=_params("parallel"),
        name="ffn_ln",
    )(x, w_in, w_out, g, b)


_INPROJ_OUTS = (
    ("sb_q", 0, 256, BF16), ("sb_k", 256, 256, BF16), ("sb_v", 512, 256, BF16),
    ("sw_q", 768, 512, BF16), ("sw_kv", 1280, 256, BF16),
    ("ml_qk", 1536, 512, F32), ("ml_v", 2048, 256, BF16),
    ("ml_o", 2304, 256, F32), ("gates", 2560, GATE_PAD, F32),
)
N_IN_PADDED = 2688


def _inproj_kernel(x_ref, w_ref, *out_refs):
    xb = x_ref[...].astype(BF16)
    for (_, lo, width, dt), o_ref in zip(_INPROJ_OUTS, out_refs):
        o_ref[...] = _dot(xb, w_ref[:, lo:lo + width]).astype(dt)


def _inproj_call(x, w):
    n = x.shape[0]
    tm = min(TOKEN_TILE, n)
    row = lambda i: (i, 0)
    return pl.pallas_call(
        _inproj_kernel,
        grid=(n // tm,),
        in_specs=[pl.BlockSpec((tm, D_MODEL), row), _resident(w.shape)],
        out_specs=[pl.BlockSpec((tm, width), row) for _, _, width, _ in _INPROJ_OUTS],
        out_shape=[jax.ShapeDtypeStruct((n, width), dt) for _, _, width, dt in _INPROJ_OUTS],
        compiler_params=_params("parallel"),
        name="mix_in_proj",
    )(x, w)


def _sb_kernel(q_ref, k_ref, v_ref, o_ref, acc_ref, carry_ref):
    qi = pl.program_id(1)
    rows = q_ref.shape[0]
    lane_head = lax.broadcasted_iota(jnp.int32, (1, SB_WIDTH), 1) // HEAD_DIM
    heads = [lane_head == h for h in range(SB_HEADS)]
    q_stacks = []
    for bi in range(rows):
        q = (q_ref[bi].astype(F32) * (HEAD_DIM ** -0.5)).astype(BF16)
        q_stacks.append(jnp.concatenate([jnp.where(head, q, jnp.zeros_like(q)) for head in heads], axis=0))
    tall = rows * SB_HEADS * BLOCK
    strict = (lax.broadcasted_iota(jnp.int32, (tall, BLOCK), 1)
              < lax.broadcasted_iota(jnp.int32, (tall, BLOCK), 0) % BLOCK)
    row = lax.broadcasted_iota(jnp.int32, (2 * BLOCK, 2 * BLOCK), 0) % BLOCK
    col = lax.broadcasted_iota(jnp.int32, (2 * BLOCK, 2 * BLOCK), 1)
    suffix_and_total = jnp.where((row >= col) | (col >= BLOCK), 1.0, 0.0).astype(BF16)

    def step(j, diagonal):
        start = pl.multiple_of(j * BLOCK, BLOCK)
        z = jnp.concatenate([_dot_nt(q_stacks[bi], k_ref[bi, pl.ds(start, BLOCK), :]) for bi in range(rows)],
                            axis=0)
        lom = jnp.minimum(-z, 0.0) - jnp.log(1.0 + jnp.exp(-jnp.abs(z)))
        if diagonal:
            lom = jnp.where(strict, lom, 0.0)
        sums = _dot(jnp.concatenate(_split_bf16(lom, 2), axis=1), suffix_and_total)
        if diagonal:
            w = jnp.where(strict, jnp.exp(z + sums[:, :BLOCK]), 0.0)
            carry = sums[:, BLOCK:]
        else:
            carry = carry_ref[...]
            w = jnp.exp(z + sums[:, :BLOCK] + carry)
            carry = carry + sums[:, BLOCK:]
        carry_ref[...] = carry
        w = w.astype(BF16)
        for bi in range(rows):
            vb = v_ref[bi, pl.ds(start, BLOCK), :]
            v_heads = jnp.concatenate([jnp.where(head, vb, jnp.zeros_like(vb)) for head in heads], axis=0)
            tiles = [w[(bi * SB_HEADS + h) * BLOCK:(bi * SB_HEADS + h + 1) * BLOCK, :] for h in range(SB_HEADS)]
            pv = _dot(jnp.concatenate(tiles, axis=1), v_heads)
            if diagonal:
                acc_ref[bi] = pv
            else:
                acc_ref[bi] += pv
        return jnp.max(carry)

    def more_blocks(state):
        j, largest_carry = state
        return (j >= 0) & (largest_carry > SB_SKIP_BELOW)

    def next_block(state):
        j, _ = state
        return j - 1, step(j, False)

    lax.while_loop(more_blocks, next_block, (qi - 1, step(qi, True)))
    o_ref[...] = acc_ref[...].astype(o_ref.dtype)


def _sb_call(q, k, v):
    bsz, s_len, _ = q.shape
    rows = SB_ROWS_PER_STEP if bsz % SB_ROWS_PER_STEP == 0 else 1
    blk = pl.BlockSpec((rows, BLOCK, SB_WIDTH), lambda b, i: (b, i, 0))
    whole = pl.BlockSpec((rows, s_len, SB_WIDTH), lambda b, i: (b, 0, 0), pipeline_mode=pl.Buffered(1))
    return pl.pallas_call(
        _sb_kernel,
        grid=(bsz // rows, s_len // BLOCK),
        in_specs=[blk, whole, whole],
        out_specs=blk,
        out_shape=jax.ShapeDtypeStruct((bsz, s_len, SB_WIDTH), BF16),
        scratch_shapes=[pltpu.VMEM((rows, BLOCK, SB_WIDTH), F32),
                        pltpu.VMEM((rows * SB_HEADS * BLOCK, BLOCK), F32)],
        compiler_params=_params("parallel", "arbitrary"),
        name="stick_breaking",
    )(q, k, v)


def _swa_kernel(sinks_ref, q_ref, kvp_ref, kvc_ref, bias_ref, o_ref):
    first = jnp.minimum(pl.program_id(1), 1)
    q = (q_ref[0].astype(F32) * (HEAD_DIM ** -0.5)).astype(BF16)
    kv = jnp.concatenate([kvp_ref[0], kvc_ref[0]], axis=0)
    outs = []
    for h in range(SWA_HEADS):
        g = h // SWA_GROUP
        qh = q[:, h * HEAD_DIM:(h + 1) * HEAD_DIM]
        kh = kv[:, g * HEAD_DIM:(g + 1) * HEAD_DIM]
        vh = kv[:, SWA_KV_WIDTH + g * HEAD_DIM:SWA_KV_WIDTH + (g + 1) * HEAD_DIM]
        logits = _dot_nt(qh, kh) + bias_ref[first, h]
        sink = sinks_ref[h]
        m = jnp.maximum(jnp.max(logits, axis=-1, keepdims=True), sink)
        p = jnp.exp(logits - m)
        denom = jnp.sum(p, axis=-1, keepdims=True) + jnp.exp(sink - m)
        outs.append(_dot(p.astype(BF16), vh) / denom)
    o_ref[0] = jnp.concatenate(outs, axis=-1).astype(o_ref.dtype)


def _swa_call(q, kv, bias, sinks):
    bsz, s_len, _ = q.shape
    return pl.pallas_call(
        _swa_kernel,
        grid=(bsz, s_len // BLOCK),
        in_specs=[pl.BlockSpec(memory_space=pltpu.SMEM),
                  pl.BlockSpec((1, BLOCK, SWA_WIDTH), lambda b, i: (b, i, 0)),
                  pl.BlockSpec((1, BLOCK, 2 * SWA_KV_WIDTH), lambda b, i: (b, jnp.maximum(i - 1, 0), 0)),
                  pl.BlockSpec((1, BLOCK, 2 * SWA_KV_WIDTH), lambda b, i: (b, i, 0)),
                  _resident(bias.shape)],
        out_specs=pl.BlockSpec((1, BLOCK, SWA_WIDTH), lambda b, i: (b, i, 0)),
        out_shape=jax.ShapeDtypeStruct((bsz, s_len, SWA_WIDTH), BF16),
        compiler_params=_params("parallel", "arbitrary"),
        name="sliding_window",
    )(sinks, q, kv, kv, bias)


def _t5_bucket(dist):
    max_exact = NUM_BUCKETS // 2
    d = np.maximum(dist, 1)
    large = max_exact + (np.log(d / max_exact) / np.log(MAX_DISTANCE / max_exact)
                         * (NUM_BUCKETS - max_exact)).astype(np.int32)
    large = np.minimum(large, NUM_BUCKETS - 1)
    return np.where(dist < max_exact, dist, large).astype(np.int32)


def _swa_bias(rel_bias):
    qi = np.arange(BLOCK)[:, None]
    kj = np.arange(2 * BLOCK)[None, :]
    dist = qi + BLOCK - kj
    bucket = jnp.asarray(_t5_bucket(np.clip(dist, 0, None)))[None]
    table = rel_bias.astype(F32)
    bias = jnp.zeros((SWA_HEADS, BLOCK, 2 * BLOCK), F32)
    for b in range(NUM_BUCKETS):
        bias = jnp.where(bucket == b, table[b][:, None, None], bias)
    in_window = (dist >= 0) & (dist < BLOCK)
    first = jnp.asarray(in_window & (kj >= BLOCK))[None]
    rest = jnp.asarray(in_window)[None]
    return jnp.stack([jnp.where(first, bias, NEG_BIG), jnp.where(rest, bias, NEG_BIG)])


def _mlstm_kernel(qk_ref, v_ref, og_ref, gates_ref, gates_t_ref, convw_ref, convb_ref,
                  gbias_row_ref, gbias_col_ref, normg_ref, y_ref,
                  xbuf_ref, c_ref, n_ref, m_ref):
    @pl.when(pl.program_id(1) == 0)
    def _():
        xbuf_ref[0:8, :] = jnp.zeros((8, 2 * ML_WIDTH), F32)
        c_ref[...] = jnp.zeros_like(c_ref)
        n_ref[...] = jnp.zeros_like(n_ref)
        m_ref[...] = jnp.zeros_like(m_ref)

    iota = lambda shape, axis: lax.broadcasted_iota(jnp.int32, shape, axis)
    row = iota((BLOCK, BLOCK), 0)
    col = iota((BLOCK, BLOCK), 1)
    causal = col <= row
    one_where = lambda cond: jnp.where(cond, 1.0, 0.0).astype(BF16)
    tri = one_where(causal)
    tri_t = one_where(row <= col)
    heads = [iota((1, ML_WIDTH), 1) // HEAD_DIM == h for h in range(ML_HEADS)]
    same_head = iota((ML_WIDTH, ML_WIDTH), 0) // HEAD_DIM == iota((ML_WIDTH, ML_WIDTH), 1) // HEAD_DIM
    head_mean = jnp.where(same_head, 1.0 / HEAD_DIM, 0.0).astype(BF16)
    sel_tile = one_where(iota((BLOCK, ML_HEADS * BLOCK), 0) == iota((BLOCK, ML_HEADS * BLOCK), 1) // BLOCK)
    sel_head = one_where(iota((BLOCK, ML_WIDTH), 0) == iota((BLOCK, ML_WIDTH), 1) // HEAD_DIM)
    rows_of_head = one_where(iota((ML_HEADS * BLOCK, ML_WIDTH), 0) // BLOCK
                             == iota((ML_HEADS * BLOCK, ML_WIDTH), 1) // HEAD_DIM)

    def spread(tile, selector):
        live = iota(tile.shape, 1) < ML_HEADS
        return sum(_dot(p, selector) for p in _split_bf16(jnp.where(live, tile, 0.0), 3))

    for blk in range(qk_ref.shape[1] // BLOCK):
        rows = slice(blk * BLOCK, (blk + 1) * BLOCK)
        xbuf_ref[8:8 + BLOCK, :] = qk_ref[0, rows, :]
        conv = convb_ref[...]
        for j in range(CONV_WIDTH):
            lo = 8 - (CONV_WIDTH - 1) + j
            conv = conv + convw_ref[j:j + 1, :] * xbuf_ref[lo:lo + BLOCK, :]
        xbuf_ref[0:8, :] = xbuf_ref[BLOCK:BLOCK + 8, :]
        act = conv * _sigmoid(conv)
        qb = act[:, :ML_WIDTH].astype(BF16)
        k_all = act[:, ML_WIDTH:] * (HEAD_DIM ** -0.5)
        kb = k_all.astype(BF16)
        v_all = v_ref[0, rows, :]

        gates = gates_ref[0, rows, :] + gbias_row_ref[...]
        gates_t = gates_t_ref[0, :, rows] + gbias_col_ref[...]
        b_cols = sum(_dot(tri, p) for p in _split_bf16(_log_sigmoid(gates), 3))
        b_rows = sum(_dot(p, tri_t) for p in _split_bf16(_log_sigmoid(gates_t), 3))
        b_tile = pltpu.roll(b_cols, BLOCK - ML_HEADS, axis=1)
        a_tile = gates - b_tile
        a_rows = gates_t[0:ML_HEADS, :] - b_rows[ML_HEADS:2 * ML_HEADS, :]

        run_max = a_tile
        shift = 1
        while shift < BLOCK:
            run_max = jnp.where(row >= shift, jnp.maximum(run_max, pltpu.roll(run_max, shift, axis=0)), run_max)
            shift *= 2
        m_prev = m_ref[0:1, :]
        u = jnp.maximum(m_prev, run_max)
        inter = jnp.exp(m_prev - u)
        floor = jnp.exp(-(b_tile + u))
        b_end = b_tile[BLOCK - 1:BLOCK, :]
        w_log = b_end - b_tile + gates
        m_new = jnp.maximum(b_end + m_prev, jnp.max(w_log, axis=0, keepdims=True))
        decay = jnp.exp(b_end + m_prev - m_new)
        w = jnp.exp(w_log - m_new)
        m_ref[...] = jnp.broadcast_to(m_new, m_ref.shape)

        neg_u = spread(-u, sel_tile)
        inter = spread(inter, sel_head)
        floor = spread(floor, sel_head)
        w = spread(w, sel_head)
        decay = spread(jnp.broadcast_to(decay, (8, BLOCK)), sel_head)[0:1, :]

        s_heads = []
        for h in range(ML_HEADS):
            log_d = jnp.where(causal, a_rows[h:h + 1, :] + neg_u[:, h * BLOCK:(h + 1) * BLOCK], NEG_BIG)
            qh = jnp.where(heads[h], qb, jnp.zeros_like(qb))
            s_heads.append((_dot_nt(qh, kb) * jnp.exp(log_d)).astype(BF16))
        s_cat = jnp.concatenate(s_heads, axis=1)
        v_heads = jnp.concatenate([jnp.where(head, v_all, jnp.zeros_like(v_all)) for head in heads], axis=0)

        c_prev = c_ref[...]
        n_prev = n_ref[...]
        num = inter * _dot(qb, c_prev.astype(BF16)) + _dot(s_cat, v_heads)
        den = inter * _dot(qb, n_prev.astype(BF16)) + _dot(s_cat, rows_of_head)
        hid = num / jnp.maximum(jnp.abs(den), floor)
        mu = sum(_dot(p, head_mean) for p in _split_bf16(hid, 2))
        dev = hid - mu
        var = sum(_dot(p, head_mean) for p in _split_bf16(dev * dev, 2))
        y = _sigmoid(og_ref[0, rows, :]) * (dev * lax.rsqrt(var + LN_EPS) * normg_ref[...])
        y_ref[0, rows, :] = y.astype(y_ref.dtype)

        wk = (w * k_all).astype(BF16)
        update = _dot_tn(wk, jnp.concatenate([v_all, jnp.ones_like(v_all)], axis=1))
        c_ref[...] = decay * c_prev + jnp.where(same_head, update[:, :ML_WIDTH], 0.0)
        n_ref[...] = decay * n_prev + jnp.where(same_head, update[:, ML_WIDTH:], 0.0)


def _mlstm_call(qk, v, og, gates, gates_t, conv_w, conv_b, gbias_row, gbias_col, norm_g):
    bsz, s_len, _ = qk.shape
    nb = SEQ_BLOCKS_PER_STEP if (s_len // BLOCK) % SEQ_BLOCKS_PER_STEP == 0 else 1
    tok = lambda w: pl.BlockSpec((1, nb * BLOCK, w), lambda b, c: (b, c, 0))
    return pl.pallas_call(
        _mlstm_kernel,
        grid=(bsz, s_len // (nb * BLOCK)),
        in_specs=[tok(2 * ML_WIDTH), tok(ML_WIDTH), tok(ML_WIDTH), tok(GATE_PAD),
                  pl.BlockSpec((1, 8, nb * BLOCK), lambda b, c: (b, 0, c)),
                  _resident(conv_w.shape), _resident(conv_b.shape), _resident(gbias_row.shape),
                  _resident(gbias_col.shape), _resident(norm_g.shape)],
        out_specs=tok(ML_WIDTH),
        out_shape=jax.ShapeDtypeStruct((bsz, s_len, ML_WIDTH), BF16),
        scratch_shapes=[pltpu.VMEM((8 + BLOCK, 2 * ML_WIDTH), F32),
                        pltpu.VMEM((ML_WIDTH, ML_WIDTH), F32),
                        pltpu.VMEM((ML_WIDTH, ML_WIDTH), F32),
                        pltpu.VMEM((8, 128), F32)],
        compiler_params=_params("parallel", "arbitrary"),
        name="mlstm",
    )(qk, v, og, gates, gates_t, conv_w, conv_b, gbias_row, gbias_col, norm_g)


def _kv_kernel(mem_ref, w_ref, o_ref):
    o_ref[...] = _dot(mem_ref[...].astype(BF16), w_ref[...].astype(BF16)).astype(o_ref.dtype)


def _kv_call(mem, w):
    n = mem.shape[0]
    tm = min(256, n)
    return pl.pallas_call(
        _kv_kernel,
        grid=(n // tm,),
        in_specs=[pl.BlockSpec((tm, D_MODEL), lambda i: (i, 0)), _resident(w.shape)],
        out_specs=pl.BlockSpec((tm, 2 * D_MODEL), lambda i: (i, 0)),
        out_shape=jax.ShapeDtypeStruct((n, 2 * D_MODEL), BF16),
        compiler_params=_params("parallel"),
        name="xattn_kv_proj",
    )(mem, w)


def _mix_out_xattn_kernel(x_ref, ysb_ref, ysw_ref, yml_ref, wmix_ref, g1_ref, b1_ref,
                          kv_ref, wq_ref, wo_ref, g_ref, b_ref, o_ref):
    mix = (_dot(ysb_ref[0], wmix_ref[0:SB_WIDTH, :].astype(BF16))
           + _dot(ysw_ref[0], wmix_ref[SB_WIDTH:SB_WIDTH + SWA_WIDTH, :].astype(BF16))
           + _dot(yml_ref[0], wmix_ref[SB_WIDTH + SWA_WIDTH:, :].astype(BF16)))
    x = _layer_norm(ALPHA * x_ref[0] + mix, g1_ref[...], b1_ref[...])
    q = _dot(x.astype(BF16), wq_ref[...].astype(BF16)).astype(BF16)
    outs = []
    for h in range(XATTN_HEADS):
        lo = h * XATTN_HEAD_DIM
        kh = kv_ref[0, :, lo:lo + XATTN_HEAD_DIM]
        vh = kv_ref[0, :, D_MODEL + lo:D_MODEL + lo + XATTN_HEAD_DIM]
        s = _dot_nt(q[:, lo:lo + XATTN_HEAD_DIM], kh) * (XATTN_HEAD_DIM ** -0.5)
        p = jnp.exp(s - jnp.max(s, axis=-1, keepdims=True))
        denom = jnp.sum(p, axis=-1, keepdims=True)
        outs.append((_dot(p.astype(BF16), vh) / denom).astype(BF16))
    attn = jnp.concatenate(outs, axis=-1)
    y = ALPHA * x + _dot(attn, wo_ref[...].astype(BF16))
    o_ref[0] = _layer_norm(y, g_ref[...], b_ref[...])


def _mix_out_xattn_call(x, y_sb, y_sw, y_ml, w_mix, g1, b1, kv, w_q, w_o, g2, b2):
    bsz, s_len, _ = x.shape
    tm = min(TOKEN_TILE, s_len)
    tok = lambda w: pl.BlockSpec((1, tm, w), lambda bi, i: (bi, i, 0))
    mix_params = [w_mix, g1, b1]
    xattn_params = [w_q, w_o, g2, b2]
    return pl.pallas_call(
        _mix_out_xattn_kernel,
        grid=(bsz, s_len // tm),
        in_specs=[tok(D_MODEL), tok(SB_WIDTH), tok(SWA_WIDTH), tok(ML_WIDTH)]
                 + [_resident(t.shape) for t in mix_params]
                 + [pl.BlockSpec((1, kv.shape[1], 2 * D_MODEL), lambda bi, i: (bi, 0, 0))]
                 + [_resident(t.shape) for t in xattn_params],
        out_specs=tok(D_MODEL),
        out_shape=jax.ShapeDtypeStruct((bsz, s_len, D_MODEL), F32),
        compiler_params=_params("parallel", "parallel"),
        name="mix_out_xattn_ln",
    )(x, y_sb, y_sw, y_ml, *mix_params, kv, *xattn_params)


def _mixer_heads(x, w_in, conv_w, conv_b, i_bias, f_bias, norm_g, sinks, bias):
    bsz, s_len, _ = x.shape
    n = bsz * s_len
    x2 = x.reshape(n, D_MODEL)
    w_perm = jnp.concatenate(
        [w_in[:, :_ML_I], w_in[:, _ML_O:], w_in[:, _ML_I:_ML_O],
         jnp.zeros((D_MODEL, GATE_PAD - 2 * ML_HEADS), w_in.dtype)], axis=1).astype(BF16)
    sb_q, sb_k, sb_v, sw_q, sw_kv, ml_qk, ml_v, ml_o, gates = _inproj_call(x2, w_perm)
    seq = lambda t: t.reshape(bsz, s_len, t.shape[-1])

    y_sb = _sb_call(seq(sb_q), seq(sb_k), seq(sb_v))
    y_sw = _swa_call(seq(sw_q), seq(sw_kv), bias, sinks)

    gates = seq(gates)
    gates_t = jnp.swapaxes(gates[..., :2 * ML_HEADS], 1, 2)
    gate_bias = jnp.concatenate([i_bias, f_bias])
    gbias_row = jnp.pad(gate_bias, (0, GATE_PAD - 2 * ML_HEADS))[None, :]
    y_ml = _mlstm_call(seq(ml_qk), seq(ml_v), seq(ml_o), gates, gates_t, conv_w, conv_b[None, :],
                       gbias_row, gate_bias[:, None], norm_g[None, :])
    return y_sb, y_sw, y_ml


def kernel(x, mem, ffn1_w_in, ffn1_w_out, mix_w_in, ml_conv_w, ml_conv_b, ml_i_bias, ml_f_bias, ml_norm_g, swa_sinks, rel_bias, mix_w_out, xattn_w_q, xattn_w_kv, xattn_w_o, ffn2_w_in, ffn2_w_out, ln_g, ln_b):
    bsz, s_len, _ = x.shape
    n = bsz * s_len
    mem2 = mem.reshape(-1, D_MODEL)
    bias = _swa_bias(rel_bias)
    for l in range(DEPTH):
        g = lambda k: ln_g[l, k][None, :]
        b = lambda k: ln_b[l, k][None, :]
        x = _ffn_call(x.reshape(n, D_MODEL), ffn1_w_in[l], ffn1_w_out[l],
                      g(0), b(0)).reshape(bsz, s_len, D_MODEL)
        y_sb, y_sw, y_ml = _mixer_heads(x, mix_w_in[l], ml_conv_w[l], ml_conv_b[l], ml_i_bias[l], ml_f_bias[l],
                                        ml_norm_g[l], swa_sinks[l], bias)
        kv = _kv_call(mem2, xattn_w_kv[l]).reshape(bsz, -1, 2 * D_MODEL)
        x = _mix_out_xattn_call(x, y_sb, y_sw, y_ml, mix_w_out[l], g(1), b(1),
                                kv, xattn_w_q[l], xattn_w_o[l], g(2), b(2))
        x = _ffn_call(x.reshape(n, D_MODEL), ffn2_w_in[l], ffn2_w_out[l],
                      g(3), b(3)).reshape(bsz, s_len, D_MODEL)
    return x
```

```python
import functools

import numpy as np
import jax
import jax.numpy as jnp
from jax import lax
from jax.experimental import pallas as pl
from jax.experimental.pallas import tpu as pltpu

F32 = jnp.float32
BF16 = jnp.bfloat16

D_MODEL = 1024
DEPTH = 2
HEAD_DIM = 64
SB_HEADS = 4
SWA_HEADS = 8
SWA_KV_HEADS = 2
SWA_GROUP = SWA_HEADS // SWA_KV_HEADS
ML_HEADS = 4
SB_WIDTH = SB_HEADS * HEAD_DIM
SWA_WIDTH = SWA_HEADS * HEAD_DIM
SWA_KV_WIDTH = SWA_KV_HEADS * HEAD_DIM
ML_WIDTH = ML_HEADS * HEAD_DIM
BLOCK = 128
CONV_WIDTH = 4
NUM_BUCKETS = 32
MAX_DISTANCE = 128
XATTN_HEADS = 4
XATTN_HEAD_DIM = D_MODEL // XATTN_HEADS
D_FF = 2816
ALPHA = (2 * DEPTH) ** 0.25
LN_EPS = 1e-5
NEG_BIG = -1e30
SB_SKIP_BELOW = -104.0

_SB_Q, _SB_K, _SB_V = 0, 256, 512
_SW_Q, _SW_K, _SW_V = 768, 1280, 1408
_ML_QK, _ML_V, _ML_I, _ML_F, _ML_O = 1536, 2048, 2304, 2308, 2312
N_IN = 2568
GATE_PAD = 128

VMEM_LIMIT = 56 * 1024 * 1024

SEQ_BLOCKS_PER_STEP = 4
SB_ROWS_PER_STEP = 4
TOKEN_TILE = 512
FF_CHUNK = 256


def _params(*sem):
    return pltpu.CompilerParams(dimension_semantics=sem, vmem_limit_bytes=VMEM_LIMIT)


def _resident(shape):
    nd = len(shape)
    return pl.BlockSpec(shape, lambda *_: (0,) * nd, pipeline_mode=pl.Buffered(1))


def _layer_slab(stacked_shape, index):
    rest = tuple(stacked_shape[1:])
    return pl.BlockSpec((None,) + rest, lambda *_: (index,) + (0,) * len(rest), pipeline_mode=pl.Buffered(1))


def _layer_norm(y, g, b):
    mu = jnp.mean(y, axis=-1, keepdims=True)
    d = y - mu
    var = jnp.mean(d * d, axis=-1, keepdims=True)
    return d * lax.rsqrt(var + LN_EPS) * g + b


def _log_sigmoid(x):
    return jnp.minimum(x, 0.0) - jnp.log(1.0 + jnp.exp(-jnp.abs(x)))


def _sigmoid(x):
    return 1.0 / (1.0 + jnp.exp(-x))


def _dot(a, b):
    return jnp.dot(a, b, preferred_element_type=F32)


def _dot_nt(a, b):
    return lax.dot_general(a, b, (((1,), (1,)), ((), ())), preferred_element_type=F32)


def _dot_tn(a, b):
    return lax.dot_general(a, b, (((0,), (0,)), ((), ())), preferred_element_type=F32)


def _split_bf16(x, pieces):
    out = []
    r = x
    for _ in range(pieces - 1):
        p = r.astype(BF16)
        out.append(p)
        r = r - p.astype(F32)
    out.append(r.astype(BF16))
    return out


def _ffn_kernel(x_ref, win_ref, wout_ref, g_ref, b_ref, o_ref, acc_ref):
    xb = x_ref[...].astype(BF16)
    for c in range(D_FF // FF_CHUNK):
        lo = c * FF_CHUNK
        a = _dot(xb, win_ref[:, lo:lo + FF_CHUNK].astype(BF16))
        b = _dot(xb, win_ref[:, D_FF + lo:D_FF + lo + FF_CHUNK].astype(BF16))
        h = (a * _sigmoid(a) * b).astype(BF16)
        part = _dot(h, wout_ref[lo:lo + FF_CHUNK, :].astype(BF16))
        if c == 0:
            acc_ref[...] = part
        else:
            acc_ref[...] += part
    y = ALPHA * x_ref[...] + 0.5 * acc_ref[...]
    o_ref[...] = _layer_norm(y, g_ref[...], b_ref[...])


def _ffn_call(x, w_in, w_out, layer, ln_g, ln_b, ln_index):
    n = x.shape[0]
    tm = min(TOKEN_TILE, n)
    row = lambda i: (i, 0)
    return pl.pallas_call(
        _ffn_kernel,
        grid=(n // tm,),
        in_specs=[pl.BlockSpec((tm, D_MODEL), row), _layer_slab(w_in.shape, layer), _layer_slab(w_out.shape, layer),
                  _layer_slab(ln_g.shape, ln_index), _layer_slab(ln_b.shape, ln_index)],
        out_specs=pl.BlockSpec((tm, D_MODEL), row),
        out_shape=jax.ShapeDtypeStruct((n, D_MODEL), F32),
        scratch_shapes=[pltpu.VMEM((tm, D_MODEL), F32)],
        compiler_params=_params("parallel"),
        name="ffn_ln",
    )(x, w_in, w_out, ln_g, ln_b)


_INPROJ_OUTS = (
    ("sb_q", 0, 256, BF16), ("sb_k", 256, 256, BF16), ("sb_v", 512, 256, BF16),
    ("sw_q", 768, 512, BF16), ("sw_kv", 1280, 256, BF16),
    ("ml_qk", 1536, 512, F32), ("ml_v", 2048, 256, BF16),
    ("ml_o", 2304, 256, F32), ("gates", 2560, GATE_PAD, F32),
)
N_IN_PADDED = 2688


def _inproj_kernel(x_ref, w_ref, *out_refs):
    xb = x_ref[...].astype(BF16)
    for (_, lo, width, dt), o_ref in zip(_INPROJ_OUTS, out_refs):
        o_ref[...] = _dot(xb, w_ref[:, lo:lo + width]).astype(dt)


def _inproj_call(x, w):
    n = x.shape[0]
    tm = min(TOKEN_TILE, n)
    row = lambda i: (i, 0)
    return pl.pallas_call(
        _inproj_kernel,
        grid=(n // tm,),
        in_specs=[pl.BlockSpec((tm, D_MODEL), row), _resident(w.shape)],
        out_specs=[pl.BlockSpec((tm, width), row) for _, _, width, _ in _INPROJ_OUTS],
        out_shape=[jax.ShapeDtypeStruct((n, width), dt) for _, _, width, dt in _INPROJ_OUTS],
        compiler_params=_params("parallel"),
        name="mix_in_proj",
    )(x, w)


def _sb_kernel(q_ref, k_ref, v_ref, o_ref, acc_ref, carry_ref):
    qi = pl.program_id(1)
    rows = q_ref.shape[0]
    lane_head = lax.broadcasted_iota(jnp.int32, (1, SB_WIDTH), 1) // HEAD_DIM
    heads = [lane_head == h for h in range(SB_HEADS)]
    q_stacks = []
    for bi in range(rows):
        q = (q_ref[bi].astype(F32) * (HEAD_DIM ** -0.5)).astype(BF16)
        q_stacks.append(jnp.concatenate([jnp.where(head, q, jnp.zeros_like(q)) for head in heads], axis=0))
    tall = rows * SB_HEADS * BLOCK
    strict = (lax.broadcasted_iota(jnp.int32, (tall, BLOCK), 1)
              < lax.broadcasted_iota(jnp.int32, (tall, BLOCK), 0) % BLOCK)
    row = lax.broadcasted_iota(jnp.int32, (2 * BLOCK, 2 * BLOCK), 0) % BLOCK
    col = lax.broadcasted_iota(jnp.int32, (2 * BLOCK, 2 * BLOCK), 1)
    suffix_and_total = jnp.where((row >= col) | (col >= BLOCK), 1.0, 0.0).astype(BF16)

    def step(j, diagonal):
        start = pl.multiple_of(j * BLOCK, BLOCK)
        z = jnp.concatenate([_dot_nt(q_stacks[bi], k_ref[bi, pl.ds(start, BLOCK), :]) for bi in range(rows)],
                            axis=0)
        lom = jnp.minimum(-z, 0.0) - jnp.log(1.0 + jnp.exp(-jnp.abs(z)))
        if diagonal:
            lom = jnp.where(strict, lom, 0.0)
        sums = _dot(jnp.concatenate(_split_bf16(lom, 2), axis=1), suffix_and_total)
        if diagonal:
            w = jnp.where(strict, jnp.exp(z + sums[:, :BLOCK]), 0.0)
            carry = sums[:, BLOCK:]
        else:
            carry = carry_ref[...]
            w = jnp.exp(z + sums[:, :BLOCK] + carry)
            carry = carry + sums[:, BLOCK:]
        carry_ref[...] = carry
        w = w.astype(BF16)
        for bi in range(rows):
            vb = v_ref[bi, pl.ds(start, BLOCK), :]
            v_heads = jnp.concatenate([jnp.where(head, vb, jnp.zeros_like(vb)) for head in heads], axis=0)
            tiles = [w[(bi * SB_HEADS + h) * BLOCK:(bi * SB_HEADS + h + 1) * BLOCK, :] for h in range(SB_HEADS)]
            pv = _dot(jnp.concatenate(tiles, axis=1), v_heads)
            if diagonal:
                acc_ref[bi] = pv
            else:
                acc_ref[bi] += pv
        return jnp.max(carry)

    def more_blocks(state):
        j, largest_carry = state
        return (j >= 0) & (largest_carry > SB_SKIP_BELOW)

    def next_block(state):
        j, _ = state
        return j - 1, step(j, False)

    lax.while_loop(more_blocks, next_block, (qi - 1, step(qi, True)))
    o_ref[...] = acc_ref[...].astype(o_ref.dtype)


def _sb_call(q, k, v):
    bsz, s_len, _ = q.shape
    rows = SB_ROWS_PER_STEP if bsz % SB_ROWS_PER_STEP == 0 else 1
    blk = pl.BlockSpec((rows, BLOCK, SB_WIDTH), lambda b, i: (b, i, 0))
    whole = pl.BlockSpec((rows, s_len, SB_WIDTH), lambda b, i: (b, 0, 0), pipeline_mode=pl.Buffered(1))
    return pl.pallas_call(
        _sb_kernel,
        grid=(bsz // rows, s_len // BLOCK),
        in_specs=[blk, whole, whole],
        out_specs=blk,
        out_shape=jax.ShapeDtypeStruct((bsz, s_len, SB_WIDTH), BF16),
        scratch_shapes=[pltpu.VMEM((rows, BLOCK, SB_WIDTH), F32),
                        pltpu.VMEM((rows * SB_HEADS * BLOCK, BLOCK), F32)],
        compiler_params=_params("parallel", "arbitrary"),
        name="stick_breaking",
    )(q, k, v)


def _swa_kernel(sinks_ref, q_ref, kvp_ref, kvc_ref, bias_ref, o_ref):
    first = jnp.minimum(pl.program_id(1), 1)
    q = (q_ref[0].astype(F32) * (HEAD_DIM ** -0.5)).astype(BF16)
    kv = jnp.concatenate([kvp_ref[0], kvc_ref[0]], axis=0)
    outs = []
    for h in range(SWA_HEADS):
        g = h // SWA_GROUP
        qh = q[:, h * HEAD_DIM:(h + 1) * HEAD_DIM]
        kh = kv[:, g * HEAD_DIM:(g + 1) * HEAD_DIM]
        vh = kv[:, SWA_KV_WIDTH + g * HEAD_DIM:SWA_KV_WIDTH + (g + 1) * HEAD_DIM]
        logits = _dot_nt(qh, kh) + bias_ref[first, h]
        sink = sinks_ref[h]
        m = jnp.maximum(jnp.max(logits, axis=-1, keepdims=True), sink)
        p = jnp.exp(logits - m)
        denom = jnp.sum(p, axis=-1, keepdims=True) + jnp.exp(sink - m)
        outs.append(_dot(p.astype(BF16), vh) / denom)
    o_ref[0] = jnp.concatenate(outs, axis=-1).astype(o_ref.dtype)


def _swa_call(q, kv, bias, sinks):
    bsz, s_len, _ = q.shape
    return pl.pallas_call(
        _swa_kernel,
        grid=(bsz, s_len // BLOCK),
        in_specs=[pl.BlockSpec(memory_space=pltpu.SMEM),
                  pl.BlockSpec((1, BLOCK, SWA_WIDTH), lambda b, i: (b, i, 0)),
                  pl.BlockSpec((1, BLOCK, 2 * SWA_KV_WIDTH), lambda b, i: (b, jnp.maximum(i - 1, 0), 0)),
                  pl.BlockSpec((1, BLOCK, 2 * SWA_KV_WIDTH), lambda b, i: (b, i, 0)),
                  _resident(bias.shape)],
        out_specs=pl.BlockSpec((1, BLOCK, SWA_WIDTH), lambda b, i: (b, i, 0)),
        out_shape=jax.ShapeDtypeStruct((bsz, s_len, SWA_WIDTH), BF16),
        compiler_params=_params("parallel", "arbitrary"),
        name="sliding_window",
    )(sinks, q, kv, kv, bias)


def _t5_bucket(dist):
    max_exact = NUM_BUCKETS // 2
    d = np.maximum(dist, 1)
    large = max_exact + (np.log(d / max_exact) / np.log(MAX_DISTANCE / max_exact)
                         * (NUM_BUCKETS - max_exact)).astype(np.int32)
    large = np.minimum(large, NUM_BUCKETS - 1)
    return np.where(dist < max_exact, dist, large).astype(np.int32)


def _swa_bias(rel_bias):
    qi = np.arange(BLOCK)[:, None]
    kj = np.arange(2 * BLOCK)[None, :]
    dist = qi + BLOCK - kj
    bucket = jnp.asarray(_t5_bucket(np.clip(dist, 0, None)))[None]
    table = rel_bias.astype(F32)
    bias = jnp.zeros((SWA_HEADS, BLOCK, 2 * BLOCK), F32)
    for b in range(NUM_BUCKETS):
        bias = jnp.where(bucket == b, table[b][:, None, None], bias)
    in_window = (dist >= 0) & (dist < BLOCK)
    first = jnp.asarray(in_window & (kj >= BLOCK))[None]
    rest = jnp.asarray(in_window)[None]
    return jnp.stack([jnp.where(first, bias, NEG_BIG), jnp.where(rest, bias, NEG_BIG)])


def _mlstm_kernel(qk_ref, v_ref, og_ref, gates_ref, gates_t_ref, convw_ref, convb_ref,
                  gbias_row_ref, gbias_col_ref, normg_ref, y_ref,
                  xbuf_ref, c_ref, n_ref, m_ref):
    @pl.when(pl.program_id(1) == 0)
    def _():
        xbuf_ref[0:8, :] = jnp.zeros((8, 2 * ML_WIDTH), F32)
        c_ref[...] = jnp.zeros_like(c_ref)
        n_ref[...] = jnp.zeros_like(n_ref)
        m_ref[...] = jnp.zeros_like(m_ref)

    iota = lambda shape, axis: lax.broadcasted_iota(jnp.int32, shape, axis)
    row = iota((BLOCK, BLOCK), 0)
    col = iota((BLOCK, BLOCK), 1)
    causal = col <= row
    one_where = lambda cond: jnp.where(cond, 1.0, 0.0).astype(BF16)
    tri = one_where(causal)
    tri_t = one_where(row <= col)
    heads = [iota((1, ML_WIDTH), 1) // HEAD_DIM == h for h in range(ML_HEADS)]
    same_head = iota((ML_WIDTH, ML_WIDTH), 0) // HEAD_DIM == iota((ML_WIDTH, ML_WIDTH), 1) // HEAD_DIM
    head_mean = jnp.where(same_head, 1.0 / HEAD_DIM, 0.0).astype(BF16)
    sel_tile = one_where(iota((BLOCK, ML_HEADS * BLOCK), 0) == iota((BLOCK, ML_HEADS * BLOCK), 1) // BLOCK)
    sel_head = one_where(iota((BLOCK, ML_WIDTH), 0) == iota((BLOCK, ML_WIDTH), 1) // HEAD_DIM)
    rows_of_head = one_where(iota((ML_HEADS * BLOCK, ML_WIDTH), 0) // BLOCK
                             == iota((ML_HEADS * BLOCK, ML_WIDTH), 1) // HEAD_DIM)

    def spread(tile, selector):
        live = iota(tile.shape, 1) < ML_HEADS
        return sum(_dot(p, selector) for p in _split_bf16(jnp.where(live, tile, 0.0), 3))

    for blk in range(qk_ref.shape[1] // BLOCK):
        rows = slice(blk * BLOCK, (blk + 1) * BLOCK)
        xbuf_ref[8:8 + BLOCK, :] = qk_ref[0, rows, :]
        conv = convb_ref[...]
        for j in range(CONV_WIDTH):
            lo = 8 - (CONV_WIDTH - 1) + j
            conv = conv + convw_ref[j:j + 1, :] * xbuf_ref[lo:lo + BLOCK, :]
        xbuf_ref[0:8, :] = xbuf_ref[BLOCK:BLOCK + 8, :]
        act = conv * _sigmoid(conv)
        qb = act[:, :ML_WIDTH].astype(BF16)
        k_all = act[:, ML_WIDTH:] * (HEAD_DIM ** -0.5)
        kb = k_all.astype(BF16)
        v_all = v_ref[0, rows, :]

        gates = gates_ref[0, rows, :] + gbias_row_ref[...]
        gates_t = gates_t_ref[0, :, rows] + gbias_col_ref[...]
        b_cols = sum(_dot(tri, p) for p in _split_bf16(_log_sigmoid(gates), 3))
        b_rows = sum(_dot(p, tri_t) for p in _split_bf16(_log_sigmoid(gates_t), 3))
        b_tile = pltpu.roll(b_cols, BLOCK - ML_HEADS, axis=1)
        a_tile = gates - b_tile
        a_rows = gates_t[0:ML_HEADS, :] - b_rows[ML_HEADS:2 * ML_HEADS, :]

        run_max = a_tile
        shift = 1
        while shift < BLOCK:
            run_max = jnp.where(row >= shift, jnp.maximum(run_max, pltpu.roll(run_max, shift, axis=0)), run_max)
            shift *= 2
        m_prev = m_ref[0:1, :]
        u = jnp.maximum(m_prev, run_max)
        inter = jnp.exp(m_prev - u)
        floor = jnp.exp(-(b_tile + u))
        b_end = b_tile[BLOCK - 1:BLOCK, :]
        w_log = b_end - b_tile + gates
        m_new = jnp.maximum(b_end + m_prev, jnp.max(w_log, axis=0, keepdims=True))
        decay = jnp.exp(b_end + m_prev - m_new)
        w = jnp.exp(w_log - m_new)
        m_ref[...] = jnp.broadcast_to(m_new, m_ref.shape)

        neg_u = spread(-u, sel_tile)
        inter = spread(inter, sel_head)
        floor = spread(floor, sel_head)
        w = spread(w, sel_head)
        decay = spread(jnp.broadcast_to(decay, (8, BLOCK)), sel_head)[0:1, :]

        s_heads = []
        for h in range(ML_HEADS):
            log_d = jnp.where(causal, a_rows[h:h + 1, :] + neg_u[:, h * BLOCK:(h + 1) * BLOCK], NEG_BIG)
            qh = jnp.where(heads[h], qb, jnp.zeros_like(qb))
            s_heads.append((_dot_nt(qh, kb) * jnp.exp(log_d)).astype(BF16))
        s_cat = jnp.concatenate(s_heads, axis=1)
        v_heads = jnp.concatenate([jnp.where(head, v_all, jnp.zeros_like(v_all)) for head in heads], axis=0)

        c_prev = c_ref[...]
        n_prev = n_ref[...]
        num = inter * _dot(qb, c_prev.astype(BF16)) + _dot(s_cat, v_heads)
        den = inter * _dot(qb, n_prev.astype(BF16)) + _dot(s_cat, rows_of_head)
        hid = num / jnp.maximum(jnp.abs(den), floor)
        mu = sum(_dot(p, head_mean) for p in _split_bf16(hid, 2))
        dev = hid - mu
        var = sum(_dot(p, head_mean) for p in _split_bf16(dev * dev, 2))
        y = _sigmoid(og_ref[0, rows, :]) * (dev * lax.rsqrt(var + LN_EPS) * normg_ref[...])
        y_ref[0, rows, :] = y.astype(y_ref.dtype)

        wk = (w * k_all).astype(BF16)
        update = _dot_tn(wk, jnp.concatenate([v_all, jnp.ones_like(v_all)], axis=1))
        c_ref[...] = decay * c_prev + jnp.where(same_head, update[:, :ML_WIDTH], 0.0)
        n_ref[...] = decay * n_prev + jnp.where(same_head, update[:, ML_WIDTH:], 0.0)


def _mlstm_call(qk, v, og, gates, gates_t, conv_w, conv_b, gbias_row, gbias_col, norm_g):
    bsz, s_len, _ = qk.shape
    nb = SEQ_BLOCKS_PER_STEP if (s_len // BLOCK) % SEQ_BLOCKS_PER_STEP == 0 else 1
    tok = lambda w: pl.BlockSpec((1, nb * BLOCK, w), lambda b, c: (b, c, 0))
    return pl.pallas_call(
        _mlstm_kernel,
        grid=(bsz, s_len // (nb * BLOCK)),
        in_specs=[tok(2 * ML_WIDTH), tok(ML_WIDTH), tok(ML_WIDTH), tok(GATE_PAD),
                  pl.BlockSpec((1, 8, nb * BLOCK), lambda b, c: (b, 0, c)),
                  _resident(conv_w.shape), _resident(conv_b.shape), _resident(gbias_row.shape),
                  _resident(gbias_col.shape), _resident(norm_g.shape)],
        out_specs=tok(ML_WIDTH),
        out_shape=jax.ShapeDtypeStruct((bsz, s_len, ML_WIDTH), BF16),
        scratch_shapes=[pltpu.VMEM((8 + BLOCK, 2 * ML_WIDTH), F32),
                        pltpu.VMEM((ML_WIDTH, ML_WIDTH), F32),
                        pltpu.VMEM((ML_WIDTH, ML_WIDTH), F32),
                        pltpu.VMEM((8, 128), F32)],
        compiler_params=_params("parallel", "arbitrary"),
        name="mlstm",
    )(qk, v, og, gates, gates_t, conv_w, conv_b, gbias_row, gbias_col, norm_g)


def _kv_kernel(mem_ref, w_ref, o_ref):
    o_ref[...] = _dot(mem_ref[...].astype(BF16), w_ref[...].astype(BF16)).astype(o_ref.dtype)


def _kv_call(mem, w, layer):
    n = mem.shape[0]
    tm = min(256, n)
    return pl.pallas_call(
        _kv_kernel,
        grid=(n // tm,),
        in_specs=[pl.BlockSpec((tm, D_MODEL), lambda i: (i, 0)), _layer_slab(w.shape, layer)],
        out_specs=pl.BlockSpec((tm, 2 * D_MODEL), lambda i: (i, 0)),
        out_shape=jax.ShapeDtypeStruct((n, 2 * D_MODEL), BF16),
        compiler_params=_params("parallel"),
        name="xattn_kv_proj",
    )(mem, w)


def _mix_out_xattn_kernel(x_ref, ysb_ref, ysw_ref, yml_ref, wmix_ref, g1_ref, b1_ref,
                          kv_ref, wq_ref, wo_ref, g_ref, b_ref, o_ref):
    mix = (_dot(ysb_ref[0], wmix_ref[0:SB_WIDTH, :].astype(BF16))
           + _dot(ysw_ref[0], wmix_ref[SB_WIDTH:SB_WIDTH + SWA_WIDTH, :].astype(BF16))
           + _dot(yml_ref[0], wmix_ref[SB_WIDTH + SWA_WIDTH:, :].astype(BF16)))
    x = _layer_norm(ALPHA * x_ref[0] + mix, g1_ref[...], b1_ref[...])
    q = _dot(x.astype(BF16), wq_ref[...].astype(BF16)).astype(BF16)
    outs = []
    for h in range(XATTN_HEADS):
        lo = h * XATTN_HEAD_DIM
        kh = kv_ref[0, :, lo:lo + XATTN_HEAD_DIM]
        vh = kv_ref[0, :, D_MODEL + lo:D_MODEL + lo + XATTN_HEAD_DIM]
        s = _dot_nt(q[:, lo:lo + XATTN_HEAD_DIM], kh) * (XATTN_HEAD_DIM ** -0.5)
        p = jnp.exp(s - jnp.max(s, axis=-1, keepdims=True))
        denom = jnp.sum(p, axis=-1, keepdims=True)
        outs.append((_dot(p.astype(BF16), vh) / denom).astype(BF16))
    attn = jnp.concatenate(outs, axis=-1)
    y = ALPHA * x + _dot(attn, wo_ref[...].astype(BF16))
    o_ref[0] = _layer_norm(y, g_ref[...], b_ref[...])


def _mix_out_xattn_call(x, y_sb, y_sw, y_ml, w_mix, kv, w_q, w_o, layer, ln_g, ln_b):
    bsz, s_len, _ = x.shape
    tm = min(TOKEN_TILE, s_len)
    tok = lambda w: pl.BlockSpec((1, tm, w), lambda bi, i: (bi, i, 0))
    ln = lambda t, k: _layer_slab(t.shape, 4 * layer + k)
    return pl.pallas_call(
        _mix_out_xattn_kernel,
        grid=(bsz, s_len // tm),
        in_specs=[tok(D_MODEL), tok(SB_WIDTH), tok(SWA_WIDTH), tok(ML_WIDTH),
                  _layer_slab(w_mix.shape, layer), ln(ln_g, 1), ln(ln_b, 1),
                  pl.BlockSpec((1, kv.shape[1], 2 * D_MODEL), lambda bi, i: (bi, 0, 0)),
                  _layer_slab(w_q.shape, layer), _layer_slab(w_o.shape, layer), ln(ln_g, 2), ln(ln_b, 2)],
        out_specs=tok(D_MODEL),
        out_shape=jax.ShapeDtypeStruct((bsz, s_len, D_MODEL), F32),
        compiler_params=_params("parallel", "parallel"),
        name="mix_out_xattn_ln",
    )(x, y_sb, y_sw, y_ml, w_mix, ln_g, ln_b, kv, w_q, w_o, ln_g, ln_b)


def _mixer_heads(x, w_in, conv_w, conv_b, i_bias, f_bias, norm_g, sinks, bias):
    bsz, s_len, _ = x.shape
    n = bsz * s_len
    x2 = x.reshape(n, D_MODEL)
    w_perm = jnp.concatenate(
        [w_in[:, :_ML_I], w_in[:, _ML_O:], w_in[:, _ML_I:_ML_O],
         jnp.zeros((D_MODEL, GATE_PAD - 2 * ML_HEADS), w_in.dtype)], axis=1).astype(BF16)
    sb_q, sb_k, sb_v, sw_q, sw_kv, ml_qk, ml_v, ml_o, gates = _inproj_call(x2, w_perm)
    seq = lambda t: t.reshape(bsz, s_len, t.shape[-1])

    y_sb = _sb_call(seq(sb_q), seq(sb_k), seq(sb_v))
    y_sw = _swa_call(seq(sw_q), seq(sw_kv), bias, sinks)

    gates = seq(gates)
    gates_t = jnp.swapaxes(gates[..., :2 * ML_HEADS], 1, 2)
    gate_bias = jnp.concatenate([i_bias, f_bias])
    gbias_row = jnp.pad(gate_bias, (0, GATE_PAD - 2 * ML_HEADS))[None, :]
    y_ml = _mlstm_call(seq(ml_qk), seq(ml_v), seq(ml_o), gates, gates_t, conv_w, conv_b[None, :],
                       gbias_row, gate_bias[:, None], norm_g[None, :])
    return y_sb, y_sw, y_ml


def kernel(x, mem, ffn1_w_in, ffn1_w_out, mix_w_in, ml_conv_w, ml_conv_b, ml_i_bias, ml_f_bias, ml_norm_g, swa_sinks, rel_bias, mix_w_out, xattn_w_q, xattn_w_kv, xattn_w_o, ffn2_w_in, ffn2_w_out, ln_g, ln_b):
    bsz, s_len, _ = x.shape
    n = bsz * s_len
    mem2 = mem.reshape(-1, D_MODEL)
    bias = _swa_bias(rel_bias)
    ln_g = ln_g.reshape(DEPTH * 4, 1, D_MODEL)
    ln_b = ln_b.reshape(DEPTH * 4, 1, D_MODEL)
    for l in range(DEPTH):
        x = _ffn_call(x.reshape(n, D_MODEL), ffn1_w_in, ffn1_w_out, l, ln_g, ln_b, 4 * l
                      ).reshape(bsz, s_len, D_MODEL)
        y_sb, y_sw, y_ml = _mixer_heads(x, mix_w_in[l], ml_conv_w[l], ml_conv_b[l], ml_i_bias[l], ml_f_bias[l],
                                        ml_norm_g[l], swa_sinks[l], bias)
        kv = _kv_call(mem2, xattn_w_kv, l).reshape(bsz, -1, 2 * D_MODEL)
        x = _mix_out_xattn_call(x, y_sb, y_sw, y_ml, mix_w_out, kv, xattn_w_q, xattn_w_o, l, ln_g, ln_b)
        x = _ffn_call(x.reshape(n, D_MODEL), ffn2_w_in, ffn2_w_out, l, ln_g, ln_b, 4 * l + 3
                      ).reshape(bsz, s_len, D_MODEL)
    return x
```

```python
import functools

import numpy as np
import jax
import jax.numpy as jnp
from jax import lax
from jax.experimental import pallas as pl
from jax.experimental.pallas import tpu as pltpu

F32 = jnp.float32
BF16 = jnp.bfloat16

D_MODEL = 1024
DEPTH = 2
HEAD_DIM = 64
SB_HEADS = 4
SWA_HEADS = 8
SWA_KV_HEADS = 2
SWA_GROUP = SWA_HEADS // SWA_KV_HEADS
ML_HEADS = 4
SB_WIDTH = SB_HEADS * HEAD_DIM
SWA_WIDTH = SWA_HEADS * HEAD_DIM
SWA_KV_WIDTH = SWA_KV_HEADS * HEAD_DIM
ML_WIDTH = ML_HEADS * HEAD_DIM
BLOCK = 128
CONV_WIDTH = 4
NUM_BUCKETS = 32
MAX_DISTANCE = 128
XATTN_HEADS = 4
XATTN_HEAD_DIM = D_MODEL // XATTN_HEADS
D_FF = 2816
ALPHA = (2 * DEPTH) ** 0.25
LN_EPS = 1e-5
NEG_BIG = -1e30
SB_SKIP_BELOW = -104.0

_SB_Q, _SB_K, _SB_V = 0, 256, 512
_SW_Q, _SW_K, _SW_V = 768, 1280, 1408
_ML_QK, _ML_V, _ML_I, _ML_F, _ML_O = 1536, 2048, 2304, 2308, 2312
N_IN = 2568
GATE_PAD = 128

VMEM_LIMIT = 56 * 1024 * 1024

SEQ_BLOCKS_PER_STEP = 4
SB_ROWS_PER_STEP = 4
TOKEN_TILE = 512
FF_CHUNK = 256


def _params(*sem):
    return pltpu.CompilerParams(dimension_semantics=sem, vmem_limit_bytes=VMEM_LIMIT)


def _resident(shape):
    nd = len(shape)
    return pl.BlockSpec(shape, lambda *_: (0,) * nd, pipeline_mode=pl.Buffered(1))


def _layer_slab(stacked_shape, index):
    rest = tuple(stacked_shape[1:])
    return pl.BlockSpec((None,) + rest, lambda *_: (index,) + (0,) * len(rest), pipeline_mode=pl.Buffered(1))


def _layer_norm(y, g, b):
    mu = jnp.mean(y, axis=-1, keepdims=True)
    d = y - mu
    var = jnp.mean(d * d, axis=-1, keepdims=True)
    return d * lax.rsqrt(var + LN_EPS) * g + b


def _log_sigmoid(x):
    return jnp.minimum(x, 0.0) - jnp.log(1.0 + jnp.exp(-jnp.abs(x)))


def _sigmoid(x):
    return 1.0 / (1.0 + jnp.exp(-x))


def _dot(a, b):
    return jnp.dot(a, b, preferred_element_type=F32)


def _dot_nt(a, b):
    return lax.dot_general(a, b, (((1,), (1,)), ((), ())), preferred_element_type=F32)


def _dot_tn(a, b):
    return lax.dot_general(a, b, (((0,), (0,)), ((), ())), preferred_element_type=F32)


def _split_bf16(x, pieces):
    out = []
    r = x
    for _ in range(pieces - 1):
        p = r.astype(BF16)
        out.append(p)
        r = r - p.astype(F32)
    out.append(r.astype(BF16))
    return out


def _ffn_kernel(x_ref, win_ref, wout_ref, g_ref, b_ref, o_ref, acc_ref):
    xb = x_ref[...].astype(BF16)
    for c in range(D_FF // FF_CHUNK):
        lo = c * FF_CHUNK
        a = _dot(xb, win_ref[:, lo:lo + FF_CHUNK].astype(BF16))
        b = _dot(xb, win_ref[:, D_FF + lo:D_FF + lo + FF_CHUNK].astype(BF16))
        h = (a * _sigmoid(a) * b).astype(BF16)
        part = _dot(h, wout_ref[lo:lo + FF_CHUNK, :].astype(BF16))
        if c == 0:
            acc_ref[...] = part
        else:
            acc_ref[...] += part
    y = ALPHA * x_ref[...] + 0.5 * acc_ref[...]
    o_ref[...] = _layer_norm(y, g_ref[...], b_ref[...])


def _ffn_call(x, w_in, w_out, layer, ln_g, ln_b, ln_index):
    n = x.shape[0]
    tm = min(TOKEN_TILE, n)
    row = lambda i: (i, 0)
    return pl.pallas_call(
        _ffn_kernel,
        grid=(n // tm,),
        in_specs=[pl.BlockSpec((tm, D_MODEL), row), _layer_slab(w_in.shape, layer), _layer_slab(w_out.shape, layer),
                  _layer_slab(ln_g.shape, ln_index), _layer_slab(ln_b.shape, ln_index)],
        out_specs=pl.BlockSpec((tm, D_MODEL), row),
        out_shape=jax.ShapeDtypeStruct((n, D_MODEL), F32),
        scratch_shapes=[pltpu.VMEM((tm, D_MODEL), F32)],
        compiler_params=_params("parallel"),
        name="ffn_ln",
    )(x, w_in, w_out, ln_g, ln_b)


_INPROJ_OUTS = (
    ("sb_q", 0, 256, BF16), ("sb_k", 256, 256, BF16), ("sb_v", 512, 256, BF16),
    ("sw_q", 768, 512, BF16), ("sw_kv", 1280, 256, BF16),
    ("ml_v", 2048, 256, BF16), ("ml_o", 2304, 256, F32), ("gates", 2560, GATE_PAD, F32),
)
_ML_QK_PERMUTED = 1536
N_IN_PADDED = 2688


def _inproj_kernel(tiles_per_seq, x_ref, w_ref, convw_ref, convb_ref, *refs):
    out_refs, (mlq_ref, mlk_ref, gates_t_ref, tail_ref) = refs[:len(_INPROJ_OUTS)], refs[len(_INPROJ_OUTS):]
    tm = x_ref.shape[0]
    xb = x_ref[...].astype(BF16)

    @pl.when(pl.program_id(0) % tiles_per_seq == 0)
    def _():
        tail_ref[...] = jnp.zeros_like(tail_ref)

    qk = _dot(xb, w_ref[:, _ML_QK_PERMUTED:_ML_QK_PERMUTED + 2 * ML_WIDTH])
    tail = tail_ref[...]
    tail_ref[...] = qk[tm - 8:, :]
    first_rows = lax.broadcasted_iota(jnp.int32, tail.shape, 0)
    conv = convb_ref[...] + convw_ref[CONV_WIDTH - 1:CONV_WIDTH, :] * qk
    for back in range(1, CONV_WIDTH):
        rolled = pltpu.roll(qk, back, axis=0)
        head = jnp.where(first_rows < back, pltpu.roll(tail, back, axis=0), rolled[0:8, :])
        shifted = jnp.concatenate([head, rolled[8:, :]], axis=0)
        conv = conv + convw_ref[CONV_WIDTH - 1 - back:CONV_WIDTH - back, :] * shifted
    act = conv * _sigmoid(conv)
    mlq_ref[...] = act[:, :ML_WIDTH].astype(mlq_ref.dtype)
    mlk_ref[...] = (act[:, ML_WIDTH:] * (HEAD_DIM ** -0.5)).astype(mlk_ref.dtype)

    for (name, lo, width, dt), o_ref in zip(_INPROJ_OUTS, out_refs):
        res = _dot(xb, w_ref[:, lo:lo + width])
        o_ref[...] = res.astype(dt)
        if name == "gates":
            gates_t_ref[...] = res.T[0:gates_t_ref.shape[0], :]


def _inproj_call(x, w, conv_w, conv_b, s_len):
    n = x.shape[0]
    tm = min(TOKEN_TILE, s_len)
    assert s_len % tm == 0
    row = lambda i: (i, 0)
    widths = [width for _, _, width, _ in _INPROJ_OUTS] + [ML_WIDTH, ML_WIDTH]
    dtypes = [dt for _, _, _, dt in _INPROJ_OUTS] + [BF16, BF16]
    return pl.pallas_call(
        functools.partial(_inproj_kernel, s_len // tm),
        grid=(n // tm,),
        in_specs=[pl.BlockSpec((tm, D_MODEL), row), _resident(w.shape), _resident(conv_w.shape),
                  _resident(conv_b.shape)],
        out_specs=[pl.BlockSpec((tm, width), row) for width in widths]
                  + [pl.BlockSpec((2 * ML_HEADS, tm), lambda i: (0, i))],
        out_shape=[jax.ShapeDtypeStruct((n, width), dt) for width, dt in zip(widths, dtypes)]
                  + [jax.ShapeDtypeStruct((2 * ML_HEADS, n), F32)],
        scratch_shapes=[pltpu.VMEM((8, 2 * ML_WIDTH), F32)],
        compiler_params=_params("arbitrary"),
        name="mix_in_proj",
    )(x, w, conv_w, conv_b)


def _sb_kernel(q_ref, k_ref, v_ref, o_ref, acc_ref, carry_ref):
    qi = pl.program_id(1)
    rows = q_ref.shape[0]
    lane_head = lax.broadcasted_iota(jnp.int32, (1, SB_WIDTH), 1) // HEAD_DIM
    heads = [lane_head == h for h in range(SB_HEADS)]
    q_stacks = []
    for bi in range(rows):
        q = (q_ref[bi].astype(F32) * (HEAD_DIM ** -0.5)).astype(BF16)
        q_stacks.append(jnp.concatenate([jnp.where(head, q, jnp.zeros_like(q)) for head in heads], axis=0))
    tall = rows * SB_HEADS * BLOCK
    strict = (lax.broadcasted_iota(jnp.int32, (tall, BLOCK), 1)
              < lax.broadcasted_iota(jnp.int32, (tall, BLOCK), 0) % BLOCK)
    row = lax.broadcasted_iota(jnp.int32, (2 * BLOCK, 2 * BLOCK), 0) % BLOCK
    col = lax.broadcasted_iota(jnp.int32, (2 * BLOCK, 2 * BLOCK), 1)
    suffix_and_total = jnp.where((row >= col) | (col >= BLOCK), 1.0, 0.0).astype(BF16)

    def step(j, diagonal):
        start = pl.multiple_of(j * BLOCK, BLOCK)
        z = jnp.concatenate([_dot_nt(q_stacks[bi], k_ref[bi, pl.ds(start, BLOCK), :]) for bi in range(rows)],
                            axis=0)
        lom = jnp.minimum(-z, 0.0) - jnp.log(1.0 + jnp.exp(-jnp.abs(z)))
        if diagonal:
            lom = jnp.where(strict, lom, 0.0)
        sums = _dot(jnp.concatenate(_split_bf16(lom, 2), axis=1), suffix_and_total)
        if diagonal:
            w = jnp.where(strict, jnp.exp(z + sums[:, :BLOCK]), 0.0)
            carry = sums[:, BLOCK:]
        else:
            carry = carry_ref[...]
            w = jnp.exp(z + sums[:, :BLOCK] + carry)
            carry = carry + sums[:, BLOCK:]
        carry_ref[...] = carry
        w = w.astype(BF16)
        for bi in range(rows):
            vb = v_ref[bi, pl.ds(start, BLOCK), :]
            v_heads = jnp.concatenate([jnp.where(head, vb, jnp.zeros_like(vb)) for head in heads], axis=0)
            tiles = [w[(bi * SB_HEADS + h) * BLOCK:(bi * SB_HEADS + h + 1) * BLOCK, :] for h in range(SB_HEADS)]
            pv = _dot(jnp.concatenate(tiles, axis=1), v_heads)
            if diagonal:
                acc_ref[bi] = pv
            else:
                acc_ref[bi] += pv
        return jnp.max(carry)

    def more_blocks(state):
        j, largest_carry = state
        return (j >= 0) & (largest_carry > SB_SKIP_BELOW)

    def next_block(state):
        j, _ = state
        return j - 1, step(j, False)

    lax.while_loop(more_blocks, next_block, (qi - 1, step(qi, True)))
    o_ref[...] = acc_ref[...].astype(o_ref.dtype)


def _sb_call(q, k, v):
    bsz, s_len, _ = q.shape
    rows = SB_ROWS_PER_STEP if bsz % SB_ROWS_PER_STEP == 0 else 1
    blk = pl.BlockSpec((rows, BLOCK, SB_WIDTH), lambda b, i: (b, i, 0))
    whole = pl.BlockSpec((rows, s_len, SB_WIDTH), lambda b, i: (b, 0, 0), pipeline_mode=pl.Buffered(1))
    return pl.pallas_call(
        _sb_kernel,
        grid=(bsz // rows, s_len // BLOCK),
        in_specs=[blk, whole, whole],
        out_specs=blk,
        out_shape=jax.ShapeDtypeStruct((bsz, s_len, SB_WIDTH), BF16),
        scratch_shapes=[pltpu.VMEM((rows, BLOCK, SB_WIDTH), F32),
                        pltpu.VMEM((rows * SB_HEADS * BLOCK, BLOCK), F32)],
        compiler_params=_params("parallel", "arbitrary"),
        name="stick_breaking",
    )(q, k, v)


def _swa_kernel(sinks_ref, q_ref, kvp_ref, kvc_ref, bias_ref, o_ref):
    first = jnp.minimum(pl.program_id(1), 1)
    q = (q_ref[0].astype(F32) * (HEAD_DIM ** -0.5)).astype(BF16)
    kv = jnp.concatenate([kvp_ref[0], kvc_ref[0]], axis=0)
    outs = []
    for h in range(SWA_HEADS):
        g = h // SWA_GROUP
        qh = q[:, h * HEAD_DIM:(h + 1) * HEAD_DIM]
        kh = kv[:, g * HEAD_DIM:(g + 1) * HEAD_DIM]
        vh = kv[:, SWA_KV_WIDTH + g * HEAD_DIM:SWA_KV_WIDTH + (g + 1) * HEAD_DIM]
        logits = _dot_nt(qh, kh) + bias_ref[first, h]
        sink = sinks_ref[h]
        m = jnp.maximum(jnp.max(logits, axis=-1, keepdims=True), sink)
        p = jnp.exp(logits - m)
        denom = jnp.sum(p, axis=-1, keepdims=True) + jnp.exp(sink - m)
        outs.append(_dot(p.astype(BF16), vh) / denom)
    o_ref[0] = jnp.concatenate(outs, axis=-1).astype(o_ref.dtype)


def _swa_call(q, kv, bias, sinks):
    bsz, s_len, _ = q.shape
    return pl.pallas_call(
        _swa_kernel,
        grid=(bsz, s_len // BLOCK),
        in_specs=[pl.BlockSpec(memory_space=pltpu.SMEM),
                  pl.BlockSpec((1, BLOCK, SWA_WIDTH), lambda b, i: (b, i, 0)),
                  pl.BlockSpec((1, BLOCK, 2 * SWA_KV_WIDTH), lambda b, i: (b, jnp.maximum(i - 1, 0), 0)),
                  pl.BlockSpec((1, BLOCK, 2 * SWA_KV_WIDTH), lambda b, i: (b, i, 0)),
                  _resident(bias.shape)],
        out_specs=pl.BlockSpec((1, BLOCK, SWA_WIDTH), lambda b, i: (b, i, 0)),
        out_shape=jax.ShapeDtypeStruct((bsz, s_len, SWA_WIDTH), BF16),
        compiler_params=_params("parallel", "arbitrary"),
        name="sliding_window",
    )(sinks, q, kv, kv, bias)


def _t5_bucket(dist):
    max_exact = NUM_BUCKETS // 2
    d = np.maximum(dist, 1)
    large = max_exact + (np.log(d / max_exact) / np.log(MAX_DISTANCE / max_exact)
                         * (NUM_BUCKETS - max_exact)).astype(np.int32)
    large = np.minimum(large, NUM_BUCKETS - 1)
    return np.where(dist < max_exact, dist, large).astype(np.int32)


def _swa_bias(rel_bias):
    qi = np.arange(BLOCK)[:, None]
    kj = np.arange(2 * BLOCK)[None, :]
    dist = qi + BLOCK - kj
    bucket = jnp.asarray(_t5_bucket(np.clip(dist, 0, None)))[None]
    table = rel_bias.astype(F32)
    bias = jnp.zeros((SWA_HEADS, BLOCK, 2 * BLOCK), F32)
    for b in range(NUM_BUCKETS):
        bias = jnp.where(bucket == b, table[b][:, None, None], bias)
    in_window = (dist >= 0) & (dist < BLOCK)
    first = jnp.asarray(in_window & (kj >= BLOCK))[None]
    rest = jnp.asarray(in_window)[None]
    return jnp.stack([jnp.where(first, bias, NEG_BIG), jnp.where(rest, bias, NEG_BIG)])


def _mlstm_kernel(q_ref, k_ref, v_ref, og_ref, gates_ref, gates_t_ref,
                  gbias_row_ref, gbias_col_ref, normg_ref, y_ref,
                  c_ref, n_ref, m_ref):
    @pl.when(pl.program_id(1) == 0)
    def _():
        c_ref[...] = jnp.zeros_like(c_ref)
        n_ref[...] = jnp.zeros_like(n_ref)
        m_ref[...] = jnp.zeros_like(m_ref)

    iota = lambda shape, axis: lax.broadcasted_iota(jnp.int32, shape, axis)
    row = iota((BLOCK, BLOCK), 0)
    col = iota((BLOCK, BLOCK), 1)
    causal = col <= row
    one_where = lambda cond: jnp.where(cond, 1.0, 0.0).astype(BF16)
    tri = one_where(causal)
    tri_t = one_where(row <= col)
    heads = [iota((1, ML_WIDTH), 1) // HEAD_DIM == h for h in range(ML_HEADS)]
    same_head = iota((ML_WIDTH, ML_WIDTH), 0) // HEAD_DIM == iota((ML_WIDTH, ML_WIDTH), 1) // HEAD_DIM
    head_mean = jnp.where(same_head, 1.0 / HEAD_DIM, 0.0).astype(BF16)
    sel_tile = one_where(iota((BLOCK, ML_HEADS * BLOCK), 0) == iota((BLOCK, ML_HEADS * BLOCK), 1) // BLOCK)
    sel_head = one_where(iota((BLOCK, ML_WIDTH), 0) == iota((BLOCK, ML_WIDTH), 1) // HEAD_DIM)
    rows_of_head = one_where(iota((ML_HEADS * BLOCK, ML_WIDTH), 0) // BLOCK
                             == iota((ML_HEADS * BLOCK, ML_WIDTH), 1) // HEAD_DIM)

    def spread(tile, selector):
        live = iota(tile.shape, 1) < ML_HEADS
        return sum(_dot(p, selector) for p in _split_bf16(jnp.where(live, tile, 0.0), 3))

    for blk in range(q_ref.shape[1] // BLOCK):
        rows = slice(blk * BLOCK, (blk + 1) * BLOCK)
        qb = q_ref[0, rows, :]
        kb = k_ref[0, rows, :]
        v_all = v_ref[0, rows, :]

        gates = gates_ref[0, rows, :] + gbias_row_ref[...]
        gates_t = gates_t_ref[:, rows] + gbias_col_ref[...]
        b_cols = sum(_dot(tri, p) for p in _split_bf16(_log_sigmoid(gates), 3))
        b_rows = sum(_dot(p, tri_t) for p in _split_bf16(_log_sigmoid(gates_t), 3))
        b_tile = pltpu.roll(b_cols, BLOCK - ML_HEADS, axis=1)
        a_tile = gates - b_tile
        a_rows = gates_t[0:ML_HEADS, :] - b_rows[ML_HEADS:2 * ML_HEADS, :]

        run_max = a_tile
        shift = 1
        while shift < BLOCK:
            run_max = jnp.where(row >= shift, jnp.maximum(run_max, pltpu.roll(run_max, shift, axis=0)), run_max)
            shift *= 2
        m_prev = m_ref[0:1, :]
        u = jnp.maximum(m_prev, run_max)
        inter = jnp.exp(m_prev - u)
        floor = jnp.exp(-(b_tile + u))
        b_end = b_tile[BLOCK - 1:BLOCK, :]
        w_log = b_end - b_tile + gates
        m_new = jnp.maximum(b_end + m_prev, jnp.max(w_log, axis=0, keepdims=True))
        decay = jnp.exp(b_end + m_prev - m_new)
        w = jnp.exp(w_log - m_new)
        m_ref[...] = jnp.broadcast_to(m_new, m_ref.shape)

        neg_u = spread(-u, sel_tile)
        inter = spread(inter, sel_head)
        floor = spread(floor, sel_head)
        w = spread(w, sel_head)
        decay = spread(jnp.broadcast_to(decay, (8, BLOCK)), sel_head)[0:1, :]

        s_heads = []
        for h in range(ML_HEADS):
            log_d = jnp.where(causal, a_rows[h:h + 1, :] + neg_u[:, h * BLOCK:(h + 1) * BLOCK], NEG_BIG)
            qh = jnp.where(heads[h], qb, jnp.zeros_like(qb))
            s_heads.append((_dot_nt(qh, kb) * jnp.exp(log_d)).astype(BF16))
        s_cat = jnp.concatenate(s_heads, axis=1)
        v_heads = jnp.concatenate([jnp.where(head, v_all, jnp.zeros_like(v_all)) for head in heads], axis=0)

        c_prev = c_ref[...]
        n_prev = n_ref[...]
        num = inter * _dot(qb, c_prev.astype(BF16)) + _dot(s_cat, v_heads)
        den = inter * _dot(qb, n_prev.astype(BF16)) + _dot(s_cat, rows_of_head)
        hid = num / jnp.maximum(jnp.abs(den), floor)
        mu = sum(_dot(p, head_mean) for p in _split_bf16(hid, 2))
        dev = hid - mu
        var = sum(_dot(p, head_mean) for p in _split_bf16(dev * dev, 2))
        y = _sigmoid(og_ref[0, rows, :]) * (dev * lax.rsqrt(var + LN_EPS) * normg_ref[...])
        y_ref[0, rows, :] = y.astype(y_ref.dtype)

        wk = (w * kb.astype(F32)).astype(BF16)
        update = _dot_tn(wk, jnp.concatenate([v_all, jnp.ones_like(v_all)], axis=1))
        c_ref[...] = decay * c_prev + jnp.where(same_head, update[:, :ML_WIDTH], 0.0)
        n_ref[...] = decay * n_prev + jnp.where(same_head, update[:, ML_WIDTH:], 0.0)


def _mlstm_call(q, k, v, og, gates, gates_t, gbias_row, gbias_col, norm_g):
    bsz, s_len, _ = q.shape
    nb = SEQ_BLOCKS_PER_STEP if (s_len // BLOCK) % SEQ_BLOCKS_PER_STEP == 0 else 1
    tok = lambda w: pl.BlockSpec((1, nb * BLOCK, w), lambda b, c: (b, c, 0))
    return pl.pallas_call(
        _mlstm_kernel,
        grid=(bsz, s_len // (nb * BLOCK)),
        in_specs=[tok(ML_WIDTH), tok(ML_WIDTH), tok(ML_WIDTH), tok(ML_WIDTH), tok(GATE_PAD),
                  pl.BlockSpec((2 * ML_HEADS, nb * BLOCK), lambda b, c: (0, b * (s_len // (nb * BLOCK)) + c)),
                  _resident(gbias_row.shape), _resident(gbias_col.shape), _resident(norm_g.shape)],
        out_specs=tok(ML_WIDTH),
        out_shape=jax.ShapeDtypeStruct((bsz, s_len, ML_WIDTH), BF16),
        scratch_shapes=[pltpu.VMEM((ML_WIDTH, ML_WIDTH), F32),
                        pltpu.VMEM((ML_WIDTH, ML_WIDTH), F32),
                        pltpu.VMEM((8, 128), F32)],
        compiler_params=_params("parallel", "arbitrary"),
        name="mlstm",
    )(q, k, v, og, gates, gates_t, gbias_row, gbias_col, norm_g)


def _kv_kernel(mem_ref, w_ref, o_ref):
    o_ref[...] = _dot(mem_ref[...].astype(BF16), w_ref[...].astype(BF16)).astype(o_ref.dtype)


def _kv_call(mem, w, layer):
    n = mem.shape[0]
    tm = min(256, n)
    return pl.pallas_call(
        _kv_kernel,
        grid=(n // tm,),
        in_specs=[pl.BlockSpec((tm, D_MODEL), lambda i: (i, 0)), _layer_slab(w.shape, layer)],
        out_specs=pl.BlockSpec((tm, 2 * D_MODEL), lambda i: (i, 0)),
        out_shape=jax.ShapeDtypeStruct((n, 2 * D_MODEL), BF16),
        compiler_params=_params("parallel"),
        name="xattn_kv_proj",
    )(mem, w)


def _mix_out_xattn_kernel(x_ref, ysb_ref, ysw_ref, yml_ref, wmix_ref, g1_ref, b1_ref,
                          kv_ref, wq_ref, wo_ref, g_ref, b_ref, o_ref):
    mix = (_dot(ysb_ref[0], wmix_ref[0:SB_WIDTH, :].astype(BF16))
           + _dot(ysw_ref[0], wmix_ref[SB_WIDTH:SB_WIDTH + SWA_WIDTH, :].astype(BF16))
           + _dot(yml_ref[0], wmix_ref[SB_WIDTH + SWA_WIDTH:, :].astype(BF16)))
    x = _layer_norm(ALPHA * x_ref[0] + mix, g1_ref[...], b1_ref[...])
    q = _dot(x.astype(BF16), wq_ref[...].astype(BF16)).astype(BF16)
    outs = []
    for h in range(XATTN_HEADS):
        lo = h * XATTN_HEAD_DIM
        kh = kv_ref[0, :, lo:lo + XATTN_HEAD_DIM]
        vh = kv_ref[0, :, D_MODEL + lo:D_MODEL + lo + XATTN_HEAD_DIM]
        s = _dot_nt(q[:, lo:lo + XATTN_HEAD_DIM], kh) * (XATTN_HEAD_DIM ** -0.5)
        p = jnp.exp(s - jnp.max(s, axis=-1, keepdims=True))
        denom = jnp.sum(p, axis=-1, keepdims=True)
        outs.append((_dot(p.astype(BF16), vh) / denom).astype(BF16))
    attn = jnp.concatenate(outs, axis=-1)
    y = ALPHA * x + _dot(attn, wo_ref[...].astype(BF16))
    o_ref[0] = _layer_norm(y, g_ref[...], b_ref[...])


def _mix_out_xattn_call(x, y_sb, y_sw, y_ml, w_mix, kv, w_q, w_o, layer, ln_g, ln_b):
    bsz, s_len, _ = x.shape
    tm = min(TOKEN_TILE, s_len)
    tok = lambda w: pl.BlockSpec((1, tm, w), lambda bi, i: (bi, i, 0))
    ln = lambda t, k: _layer_slab(t.shape, 4 * layer + k)
    return pl.pallas_call(
        _mix_out_xattn_kernel,
        grid=(bsz, s_len // tm),
        in_specs=[tok(D_MODEL), tok(SB_WIDTH), tok(SWA_WIDTH), tok(ML_WIDTH),
                  _layer_slab(w_mix.shape, layer), ln(ln_g, 1), ln(ln_b, 1),
                  pl.BlockSpec((1, kv.shape[1], 2 * D_MODEL), lambda bi, i: (bi, 0, 0)),
                  _layer_slab(w_q.shape, layer), _layer_slab(w_o.shape, layer), ln(ln_g, 2), ln(ln_b, 2)],
        out_specs=tok(D_MODEL),
        out_shape=jax.ShapeDtypeStruct((bsz, s_len, D_MODEL), F32),
        compiler_params=_params("parallel", "parallel"),
        name="mix_out_xattn_ln",
    )(x, y_sb, y_sw, y_ml, w_mix, ln_g, ln_b, kv, w_q, w_o, ln_g, ln_b)


def _mixer_heads(x, w_in, conv_w, conv_b, i_bias, f_bias, norm_g, sinks, bias):
    bsz, s_len, _ = x.shape
    n = bsz * s_len
    x2 = x.reshape(n, D_MODEL)
    w_perm = jnp.concatenate(
        [w_in[:, :_ML_I], w_in[:, _ML_O:], w_in[:, _ML_I:_ML_O],
         jnp.zeros((D_MODEL, GATE_PAD - 2 * ML_HEADS), w_in.dtype)], axis=1).astype(BF16)
    sb_q, sb_k, sb_v, sw_q, sw_kv, ml_v, ml_o, gates, ml_q, ml_k, gates_t = _inproj_call(
        x2, w_perm, conv_w, conv_b[None, :], s_len)
    seq = lambda t: t.reshape(bsz, s_len, t.shape[-1])

    y_sb = _sb_call(seq(sb_q), seq(sb_k), seq(sb_v))
    y_sw = _swa_call(seq(sw_q), seq(sw_kv), bias, sinks)

    gate_bias = jnp.concatenate([i_bias, f_bias])
    gbias_row = jnp.pad(gate_bias, (0, GATE_PAD - 2 * ML_HEADS))[None, :]
    y_ml = _mlstm_call(seq(ml_q), seq(ml_k), seq(ml_v), seq(ml_o), seq(gates), gates_t,
                       gbias_row, gate_bias[:, None], norm_g[None, :])
    return y_sb, y_sw, y_ml


def kernel(x, mem, ffn1_w_in, ffn1_w_out, mix_w_in, ml_conv_w, ml_conv_b, ml_i_bias, ml_f_bias, ml_norm_g, swa_sinks, rel_bias, mix_w_out, xattn_w_q, xattn_w_kv, xattn_w_o, ffn2_w_in, ffn2_w_out, ln_g, ln_b):
    bsz, s_len, _ = x.shape
    n = bsz * s_len
    mem2 = mem.reshape(-1, D_MODEL)
    bias = _swa_bias(rel_bias)
    ln_g = ln_g.reshape(DEPTH * 4, 1, D_MODEL)
    ln_b = ln_b.reshape(DEPTH * 4, 1, D_MODEL)
    for l in range(DEPTH):
        x = _ffn_call(x.reshape(n, D_MODEL), ffn1_w_in, ffn1_w_out, l, ln_g, ln_b, 4 * l
                      ).reshape(bsz, s_len, D_MODEL)
        y_sb, y_sw, y_ml = _mixer_heads(x, mix_w_in[l], ml_conv_w[l], ml_conv_b[l], ml_i_bias[l], ml_f_bias[l],
                                        ml_norm_g[l], swa_sinks[l], bias)
        kv = _kv_call(mem2, xattn_w_kv, l).reshape(bsz, -1, 2 * D_MODEL)
        x = _mix_out_xattn_call(x, y_sb, y_sw, y_ml, mix_w_out, kv, xattn_w_q, xattn_w_o, l, ln_g, ln_b)
        x = _ffn_call(x.reshape(n, D_MODEL), ffn2_w_in, ffn2_w_out, l, ln_g, ln_b, 4 * l + 3
                      ).reshape(bsz, s_len, D_MODEL)
    return x
```

```python
import functools

import numpy as np
import jax
import jax.numpy as jnp
from jax import lax
from jax.experimental import pallas as pl
from jax.experimental.pallas import tpu as pltpu

F32 = jnp.float32
BF16 = jnp.bfloat16

D_MODEL = 1024
DEPTH = 2
HEAD_DIM = 64
SB_HEADS = 4
SWA_HEADS = 8
SWA_KV_HEADS = 2
SWA_GROUP = SWA_HEADS // SWA_KV_HEADS
ML_HEADS = 4
SB_WIDTH = SB_HEADS * HEAD_DIM
SWA_WIDTH = SWA_HEADS * HEAD_DIM
SWA_KV_WIDTH = SWA_KV_HEADS * HEAD_DIM
ML_WIDTH = ML_HEADS * HEAD_DIM
BLOCK = 128
CONV_WIDTH = 4
NUM_BUCKETS = 32
MAX_DISTANCE = 128
XATTN_HEADS = 4
XATTN_HEAD_DIM = D_MODEL // XATTN_HEADS
D_FF = 2816
ALPHA = (2 * DEPTH) ** 0.25
LN_EPS = 1e-5
NEG_BIG = -1e30
SB_SKIP_BELOW = -104.0

_SB_Q, _SB_K, _SB_V = 0, 256, 512
_SW_Q, _SW_K, _SW_V = 768, 1280, 1408
_ML_QK, _ML_V, _ML_I, _ML_F, _ML_O = 1536, 2048, 2304, 2308, 2312
N_IN = 2568
GATE_PAD = 128

VMEM_LIMIT = 56 * 1024 * 1024

SEQ_BLOCKS_PER_STEP = 4
SB_ROWS_PER_STEP = 4
TOKEN_TILE = 512
FF_CHUNK = 256


def _params(*sem):
    return pltpu.CompilerParams(dimension_semantics=sem, vmem_limit_bytes=VMEM_LIMIT)


def _resident(shape):
    nd = len(shape)
    return pl.BlockSpec(shape, lambda *_: (0,) * nd, pipeline_mode=pl.Buffered(1))


def _layer_slab(stacked_shape, index):
    rest = tuple(stacked_shape[1:])
    return pl.BlockSpec((None,) + rest, lambda *_: (index,) + (0,) * len(rest), pipeline_mode=pl.Buffered(1))


def _layer_norm(y, g, b):
    mu = jnp.mean(y, axis=-1, keepdims=True)
    d = y - mu
    var = jnp.mean(d * d, axis=-1, keepdims=True)
    return d * lax.rsqrt(var + LN_EPS) * g + b


def _log_sigmoid(x):
    return jnp.minimum(x, 0.0) - jnp.log(1.0 + jnp.exp(-jnp.abs(x)))


def _sigmoid(x):
    return 1.0 / (1.0 + jnp.exp(-x))


def _dot(a, b):
    return jnp.dot(a, b, preferred_element_type=F32)


def _dot_nt(a, b):
    return lax.dot_general(a, b, (((1,), (1,)), ((), ())), preferred_element_type=F32)


def _dot_tn(a, b):
    return lax.dot_general(a, b, (((0,), (0,)), ((), ())), preferred_element_type=F32)


def _split_bf16(x, pieces):
    out = []
    r = x
    for _ in range(pieces - 1):
        p = r.astype(BF16)
        out.append(p)
        r = r - p.astype(F32)
    out.append(r.astype(BF16))
    return out


def _ffn_kernel(x_ref, win_ref, wout_ref, g_ref, b_ref, o_ref, acc_ref):
    xb = x_ref[...].astype(BF16)
    for c in range(D_FF // FF_CHUNK):
        lo = c * FF_CHUNK
        a = _dot(xb, win_ref[:, lo:lo + FF_CHUNK].astype(BF16))
        b = _dot(xb, win_ref[:, D_FF + lo:D_FF + lo + FF_CHUNK].astype(BF16))
        h = (a * _sigmoid(a) * b).astype(BF16)
        part = _dot(h, wout_ref[lo:lo + FF_CHUNK, :].astype(BF16))
        if c == 0:
            acc_ref[...] = part
        else:
            acc_ref[...] += part
    y = ALPHA * x_ref[...] + 0.5 * acc_ref[...]
    o_ref[...] = _layer_norm(y, g_ref[...], b_ref[...])


def _ffn_call(x, w_in, w_out, layer, ln_g, ln_b, ln_index):
    n = x.shape[0]
    tm = min(TOKEN_TILE, n)
    row = lambda i: (i, 0)
    return pl.pallas_call(
        _ffn_kernel,
        grid=(n // tm,),
        in_specs=[pl.BlockSpec((tm, D_MODEL), row), _layer_slab(w_in.shape, layer), _layer_slab(w_out.shape, layer),
                  _layer_slab(ln_g.shape, ln_index), _layer_slab(ln_b.shape, ln_index)],
        out_specs=pl.BlockSpec((tm, D_MODEL), row),
        out_shape=jax.ShapeDtypeStruct((n, D_MODEL), F32),
        scratch_shapes=[pltpu.VMEM((tm, D_MODEL), F32)],
        compiler_params=_params("parallel"),
        name="ffn_ln",
    )(x, w_in, w_out, ln_g, ln_b)


_INPROJ_OUTS = (
    ("sb_q", 0, 256, BF16), ("sb_k", 256, 256, BF16), ("sb_v", 512, 256, BF16),
    ("sw_q", 768, 512, BF16), ("sw_kv", 1280, 256, BF16),
    ("ml_v", 2048, 256, BF16), ("ml_o", 2304, 256, F32), ("gates", 2560, GATE_PAD, F32),
)
_ML_QK_PERMUTED = 1536
N_IN_PADDED = 2688


def _inproj_kernel(tiles_per_seq, x_ref, w_ref, convw_ref, convb_ref, *refs):
    out_refs, (mlq_ref, mlk_ref, gates_t_ref, tail_ref) = refs[:len(_INPROJ_OUTS)], refs[len(_INPROJ_OUTS):]
    tm = x_ref.shape[0]
    xb = x_ref[...].astype(BF16)

    @pl.when(pl.program_id(0) % tiles_per_seq == 0)
    def _():
        tail_ref[...] = jnp.zeros_like(tail_ref)

    qk = _dot(xb, w_ref[:, _ML_QK_PERMUTED:_ML_QK_PERMUTED + 2 * ML_WIDTH])
    tail = tail_ref[...]
    tail_ref[...] = qk[tm - 8:, :]
    first_rows = lax.broadcasted_iota(jnp.int32, tail.shape, 0)
    conv = convb_ref[...] + convw_ref[CONV_WIDTH - 1:CONV_WIDTH, :] * qk
    for back in range(1, CONV_WIDTH):
        rolled = pltpu.roll(qk, back, axis=0)
        head = jnp.where(first_rows < back, pltpu.roll(tail, back, axis=0), rolled[0:8, :])
        shifted = jnp.concatenate([head, rolled[8:, :]], axis=0)
        conv = conv + convw_ref[CONV_WIDTH - 1 - back:CONV_WIDTH - back, :] * shifted
    act = conv * _sigmoid(conv)
    mlq_ref[...] = act[:, :ML_WIDTH].astype(mlq_ref.dtype)
    mlk_ref[...] = (act[:, ML_WIDTH:] * (HEAD_DIM ** -0.5)).astype(mlk_ref.dtype)

    for (name, lo, width, dt), o_ref in zip(_INPROJ_OUTS, out_refs):
        res = _dot(xb, w_ref[:, lo:lo + width])
        o_ref[...] = res.astype(dt)
        if name == "gates":
            gates_t_ref[...] = res.T[0:gates_t_ref.shape[0], :]


def _inproj_call(x, w, conv_w, conv_b, s_len):
    n = x.shape[0]
    tm = min(TOKEN_TILE, s_len)
    assert s_len % tm == 0
    row = lambda i: (i, 0)
    widths = [width for _, _, width, _ in _INPROJ_OUTS] + [ML_WIDTH, ML_WIDTH]
    dtypes = [dt for _, _, _, dt in _INPROJ_OUTS] + [BF16, BF16]
    return pl.pallas_call(
        functools.partial(_inproj_kernel, s_len // tm),
        grid=(n // tm,),
        in_specs=[pl.BlockSpec((tm, D_MODEL), row), _resident(w.shape), _resident(conv_w.shape),
                  _resident(conv_b.shape)],
        out_specs=[pl.BlockSpec((tm, width), row) for width in widths]
                  + [pl.BlockSpec((2 * ML_HEADS, tm), lambda i: (0, i))],
        out_shape=[jax.ShapeDtypeStruct((n, width), dt) for width, dt in zip(widths, dtypes)]
                  + [jax.ShapeDtypeStruct((2 * ML_HEADS, n), F32)],
        scratch_shapes=[pltpu.VMEM((8, 2 * ML_WIDTH), F32)],
        compiler_params=_params("arbitrary"),
        name="mix_in_proj",
    )(x, w, conv_w, conv_b)


def _sb_kernel(q_ref, k_ref, v_ref, o_ref, acc_ref, carry_ref):
    qi = pl.program_id(1)
    rows = q_ref.shape[0]
    lane_head = lax.broadcasted_iota(jnp.int32, (1, SB_WIDTH), 1) // HEAD_DIM
    heads = [lane_head == h for h in range(SB_HEADS)]
    q_stacks = []
    for bi in range(rows):
        q = (q_ref[bi].astype(F32) * (HEAD_DIM ** -0.5)).astype(BF16)
        q_stacks.append(jnp.concatenate([jnp.where(head, q, jnp.zeros_like(q)) for head in heads], axis=0))
    tall = rows * SB_HEADS * BLOCK
    strict = (lax.broadcasted_iota(jnp.int32, (tall, BLOCK), 1)
              < lax.broadcasted_iota(jnp.int32, (tall, BLOCK), 0) % BLOCK)
    row = lax.broadcasted_iota(jnp.int32, (2 * BLOCK, 2 * BLOCK), 0) % BLOCK
    col = lax.broadcasted_iota(jnp.int32, (2 * BLOCK, 2 * BLOCK), 1)
    suffix_and_total = jnp.where((row >= col) | (col >= BLOCK), 1.0, 0.0).astype(BF16)

    def step(j, diagonal):
        start = pl.multiple_of(j * BLOCK, BLOCK)
        z = jnp.concatenate([_dot_nt(q_stacks[bi], k_ref[bi, pl.ds(start, BLOCK), :]) for bi in range(rows)],
                            axis=0)
        lom = jnp.minimum(-z, 0.0) - jnp.log(1.0 + jnp.exp(-jnp.abs(z)))
        if diagonal:
            lom = jnp.where(strict, lom, 0.0)
        sums = _dot(jnp.concatenate(_split_bf16(lom, 2), axis=1), suffix_and_total)
        if diagonal:
            w = jnp.where(strict, jnp.exp(z + sums[:, :BLOCK]), 0.0)
            carry = sums[:, BLOCK:]
        else:
            carry = carry_ref[...]
            w = jnp.exp(z + sums[:, :BLOCK] + carry)
            carry = carry + sums[:, BLOCK:]
        carry_ref[...] = carry
        w = w.astype(BF16)
        for bi in range(rows):
            vb = v_ref[bi, pl.ds(start, BLOCK), :]
            v_heads = jnp.concatenate([jnp.where(head, vb, jnp.zeros_like(vb)) for head in heads], axis=0)
            tiles = [w[(bi * SB_HEADS + h) * BLOCK:(bi * SB_HEADS + h + 1) * BLOCK, :] for h in range(SB_HEADS)]
            pv = _dot(jnp.concatenate(tiles, axis=1), v_heads)
            if diagonal:
                acc_ref[bi] = pv
            else:
                acc_ref[bi] += pv
        return jnp.max(carry)

    def more_blocks(state):
        j, largest_carry = state
        return (j >= 0) & (largest_carry > SB_SKIP_BELOW)

    def next_block(state):
        j, _ = state
        return j - 1, step(j, False)

    lax.while_loop(more_blocks, next_block, (qi - 1, step(qi, True)))
    o_ref[...] = acc_ref[...].astype(o_ref.dtype)


def _sb_call(q, k, v):
    bsz, s_len, _ = q.shape
    rows = SB_ROWS_PER_STEP if bsz % SB_ROWS_PER_STEP == 0 else 1
    blk = pl.BlockSpec((rows, BLOCK, SB_WIDTH), lambda b, i: (b, i, 0))
    whole = pl.BlockSpec((rows, s_len, SB_WIDTH), lambda b, i: (b, 0, 0), pipeline_mode=pl.Buffered(1))
    return pl.pallas_call(
        _sb_kernel,
        grid=(bsz // rows, s_len // BLOCK),
        in_specs=[blk, whole, whole],
        out_specs=blk,
        out_shape=jax.ShapeDtypeStruct((bsz, s_len, SB_WIDTH), BF16),
        scratch_shapes=[pltpu.VMEM((rows, BLOCK, SB_WIDTH), F32),
                        pltpu.VMEM((rows * SB_HEADS * BLOCK, BLOCK), F32)],
        compiler_params=_params("parallel", "arbitrary"),
        name="stick_breaking",
    )(q, k, v)


def _swa_kernel(sinks_ref, q_ref, kvp_ref, kvc_ref, bias_ref, o_ref):
    first = jnp.minimum(pl.program_id(1), 1)
    q = (q_ref[0].astype(F32) * (HEAD_DIM ** -0.5)).astype(BF16)
    kv = jnp.concatenate([kvp_ref[0], kvc_ref[0]], axis=0)
    outs = []
    for h in range(SWA_HEADS):
        g = h // SWA_GROUP
        qh = q[:, h * HEAD_DIM:(h + 1) * HEAD_DIM]
        kh = kv[:, g * HEAD_DIM:(g + 1) * HEAD_DIM]
        vh = kv[:, SWA_KV_WIDTH + g * HEAD_DIM:SWA_KV_WIDTH + (g + 1) * HEAD_DIM]
        logits = _dot_nt(qh, kh) + bias_ref[first, h]
        sink = sinks_ref[h]
        m = jnp.maximum(jnp.max(logits, axis=-1, keepdims=True), sink)
        p = jnp.exp(logits - m)
        denom = jnp.sum(p, axis=-1, keepdims=True) + jnp.exp(sink - m)
        outs.append(_dot(p.astype(BF16), vh) / denom)
    o_ref[0] = jnp.concatenate(outs, axis=-1).astype(o_ref.dtype)


def _swa_call(q, kv, bias, sinks):
    bsz, s_len, _ = q.shape
    return pl.pallas_call(
        _swa_kernel,
        grid=(bsz, s_len // BLOCK),
        in_specs=[pl.BlockSpec(memory_space=pltpu.SMEM),
                  pl.BlockSpec((1, BLOCK, SWA_WIDTH), lambda b, i: (b, i, 0)),
                  pl.BlockSpec((1, BLOCK, 2 * SWA_KV_WIDTH), lambda b, i: (b, jnp.maximum(i - 1, 0), 0)),
                  pl.BlockSpec((1, BLOCK, 2 * SWA_KV_WIDTH), lambda b, i: (b, i, 0)),
                  _resident(bias.shape)],
        out_specs=pl.BlockSpec((1, BLOCK, SWA_WIDTH), lambda b, i: (b, i, 0)),
        out_shape=jax.ShapeDtypeStruct((bsz, s_len, SWA_WIDTH), BF16),
        compiler_params=_params("parallel", "arbitrary"),
        name="sliding_window",
    )(sinks, q, kv, kv, bias)


def _t5_bucket(dist):
    max_exact = NUM_BUCKETS // 2
    d = np.maximum(dist, 1)
    large = max_exact + (np.log(d / max_exact) / np.log(MAX_DISTANCE / max_exact)
                         * (NUM_BUCKETS - max_exact)).astype(np.int32)
    large = np.minimum(large, NUM_BUCKETS - 1)
    return np.where(dist < max_exact, dist, large).astype(np.int32)


def _swa_bias(rel_bias):
    qi = np.arange(BLOCK)[:, None]
    kj = np.arange(2 * BLOCK)[None, :]
    dist = qi + BLOCK - kj
    bucket = jnp.asarray(_t5_bucket(np.clip(dist, 0, None)))[None]
    table = rel_bias.astype(F32)
    bias = jnp.zeros((SWA_HEADS, BLOCK, 2 * BLOCK), F32)
    for b in range(NUM_BUCKETS):
        bias = jnp.where(bucket == b, table[b][:, None, None], bias)
    in_window = (dist >= 0) & (dist < BLOCK)
    first = jnp.asarray(in_window & (kj >= BLOCK))[None]
    rest = jnp.asarray(in_window)[None]
    return jnp.stack([jnp.where(first, bias, NEG_BIG), jnp.where(rest, bias, NEG_BIG)])


def _mlstm_kernel(q_ref, k_ref, v_ref, og_ref, gates_ref, gates_t_ref,
                  gbias_row_ref, gbias_col_ref, normg_ref, y_ref,
                  c_ref, n_ref, m_ref):
    @pl.when(pl.program_id(1) == 0)
    def _():
        c_ref[...] = jnp.zeros_like(c_ref)
        n_ref[...] = jnp.zeros_like(n_ref)
        m_ref[...] = jnp.zeros_like(m_ref)

    iota = lambda shape, axis: lax.broadcasted_iota(jnp.int32, shape, axis)
    row = iota((BLOCK, BLOCK), 0)
    col = iota((BLOCK, BLOCK), 1)
    causal = col <= row
    one_where = lambda cond: jnp.where(cond, 1.0, 0.0).astype(BF16)
    tri = one_where(causal)
    tri_t = one_where(row <= col)
    heads = [iota((1, ML_WIDTH), 1) // HEAD_DIM == h for h in range(ML_HEADS)]
    same_head = iota((ML_WIDTH, ML_WIDTH), 0) // HEAD_DIM == iota((ML_WIDTH, ML_WIDTH), 1) // HEAD_DIM
    head_mean = jnp.where(same_head, 1.0 / HEAD_DIM, 0.0).astype(BF16)
    sel_tile = one_where(iota((BLOCK, ML_HEADS * BLOCK), 0) == iota((BLOCK, ML_HEADS * BLOCK), 1) // BLOCK)
    sel_head = one_where(iota((BLOCK, ML_WIDTH), 0) == iota((BLOCK, ML_WIDTH), 1) // HEAD_DIM)
    rows_of_head = one_where(iota((ML_HEADS * BLOCK, ML_WIDTH), 0) // BLOCK
                             == iota((ML_HEADS * BLOCK, ML_WIDTH), 1) // HEAD_DIM)

    def spread(tile, selector):
        live = iota(tile.shape, 1) < ML_HEADS
        return sum(_dot(p, selector) for p in _split_bf16(jnp.where(live, tile, 0.0), 2))

    total = q_ref.shape[1]
    chunks = [slice(c * BLOCK, (c + 1) * BLOCK) for c in range(total // BLOCK)]
    row_in_chunk = iota((total, BLOCK), 0) % BLOCK
    causal_heads = iota((ML_HEADS * BLOCK, BLOCK), 1) <= iota((ML_HEADS * BLOCK, BLOCK), 0) % BLOCK
    qb = q_ref[0]
    kb = k_ref[0]
    v_all = v_ref[0]

    gates = gates_ref[0] + gbias_row_ref[...]
    gates_t = gates_t_ref[...] + gbias_col_ref[...]
    lsf = _split_bf16(_log_sigmoid(gates), 3)
    b_cols = jnp.concatenate([sum(_dot(tri, p[rows, :]) for p in lsf) for rows in chunks], axis=0)
    lsf_t = _log_sigmoid(gates_t)
    lsf_t = jnp.concatenate([lsf_t[:, rows] for rows in chunks], axis=0)
    b_rows = sum(_dot(p, tri_t) for p in _split_bf16(lsf_t, 3))
    b_tile = pltpu.roll(b_cols, BLOCK - ML_HEADS, axis=1)
    a_tile = gates - b_tile

    run_max = a_tile
    shift = 1
    while shift < BLOCK:
        run_max = jnp.where(row_in_chunk >= shift,
                            jnp.maximum(run_max, pltpu.roll(run_max, shift, axis=0)), run_max)
        shift *= 2

    m_prev = m_ref[0:1, :]
    m_prevs, m_news, decays, w_logs = [], [], [], []
    for c, rows in enumerate(chunks):
        b_end = b_tile[(c + 1) * BLOCK - 1:(c + 1) * BLOCK, :]
        w_log = b_end - b_tile[rows, :] + gates[rows, :]
        m_new = jnp.maximum(b_end + m_prev, jnp.max(w_log, axis=0, keepdims=True))
        decays.append(jnp.broadcast_to(jnp.exp(b_end + m_prev - m_new), (8, BLOCK)))
        m_prevs.append(jnp.broadcast_to(m_prev, (BLOCK, BLOCK)))
        m_news.append(jnp.broadcast_to(m_new, (BLOCK, BLOCK)))
        w_logs.append(w_log)
        m_prev = m_new
    m_ref[...] = jnp.broadcast_to(m_prev, m_ref.shape)
    m_before = jnp.concatenate(m_prevs, axis=0)
    u = jnp.maximum(m_before, run_max)
    neg_u = spread(-u, sel_tile)
    inter = spread(jnp.exp(m_before - u), sel_head)
    floor = spread(jnp.exp(-(b_tile + u)), sel_head)
    w = spread(jnp.exp(jnp.concatenate(w_logs, axis=0) - jnp.concatenate(m_news, axis=0)), sel_head)
    decay = spread(jnp.concatenate(decays, axis=0), sel_head)

    s_cat = []
    for c, rows in enumerate(chunks):
        q_c = qb[rows, :]
        q_stack = jnp.concatenate([jnp.where(head, q_c, jnp.zeros_like(q_c)) for head in heads], axis=0)
        a_rows = gates_t[0:ML_HEADS, rows] - b_rows[8 * c + ML_HEADS:8 * c + 2 * ML_HEADS, :]
        log_d = jnp.concatenate([a_rows[h:h + 1, :] + neg_u[rows, h * BLOCK:(h + 1) * BLOCK]
                                 for h in range(ML_HEADS)], axis=0)
        s = (_dot_nt(q_stack, kb[rows, :]) * jnp.exp(jnp.where(causal_heads, log_d, NEG_BIG))).astype(BF16)
        s_cat.append(jnp.concatenate([s[h * BLOCK:(h + 1) * BLOCK, :] for h in range(ML_HEADS)], axis=1))
    s_cat = jnp.concatenate(s_cat, axis=0)
    den_within = _dot(s_cat, rows_of_head)
    wk = (w * kb.astype(F32)).astype(BF16)
    v_ones = jnp.concatenate([v_all, jnp.ones_like(v_all)], axis=1)

    c_state = c_ref[...]
    n_state = n_ref[...]
    nums, dens = [], []
    for c, rows in enumerate(chunks):
        v_c = v_all[rows, :]
        v_heads = jnp.concatenate([jnp.where(head, v_c, jnp.zeros_like(v_c)) for head in heads], axis=0)
        carried = _dot(qb[rows, :], jnp.concatenate([c_state, n_state], axis=1).astype(BF16))
        nums.append(inter[rows, :] * carried[:, :ML_WIDTH] + _dot(s_cat[rows, :], v_heads))
        dens.append(inter[rows, :] * carried[:, ML_WIDTH:] + den_within[rows, :])
        update = _dot_tn(wk[rows, :], v_ones[rows, :])
        decay_c = decay[8 * c:8 * c + 1, :]
        c_state = decay_c * c_state + jnp.where(same_head, update[:, :ML_WIDTH], 0.0)
        n_state = decay_c * n_state + jnp.where(same_head, update[:, ML_WIDTH:], 0.0)
    c_ref[...] = c_state
    n_ref[...] = n_state

    hid = jnp.concatenate(nums, axis=0) / jnp.maximum(jnp.abs(jnp.concatenate(dens, axis=0)), floor)
    mu = sum(_dot(p, head_mean) for p in _split_bf16(hid, 2))
    dev = hid - mu
    var = sum(_dot(p, head_mean) for p in _split_bf16(dev * dev, 2))
    y = _sigmoid(og_ref[0]) * (dev * lax.rsqrt(var + LN_EPS) * normg_ref[...])
    y_ref[0] = y.astype(y_ref.dtype)


def _mlstm_call(q, k, v, og, gates, gates_t, gbias_row, gbias_col, norm_g):
    bsz, s_len, _ = q.shape
    nb = SEQ_BLOCKS_PER_STEP if (s_len // BLOCK) % SEQ_BLOCKS_PER_STEP == 0 else 1
    tok = lambda w: pl.BlockSpec((1, nb * BLOCK, w), lambda b, c: (b, c, 0))
    return pl.pallas_call(
        _mlstm_kernel,
        grid=(bsz, s_len // (nb * BLOCK)),
        in_specs=[tok(ML_WIDTH), tok(ML_WIDTH), tok(ML_WIDTH), tok(ML_WIDTH), tok(GATE_PAD),
                  pl.BlockSpec((2 * ML_HEADS, nb * BLOCK), lambda b, c: (0, b * (s_len // (nb * BLOCK)) + c)),
                  _resident(gbias_row.shape), _resident(gbias_col.shape), _resident(norm_g.shape)],
        out_specs=tok(ML_WIDTH),
        out_shape=jax.ShapeDtypeStruct((bsz, s_len, ML_WIDTH), BF16),
        scratch_shapes=[pltpu.VMEM((ML_WIDTH, ML_WIDTH), F32),
                        pltpu.VMEM((ML_WIDTH, ML_WIDTH), F32),
                        pltpu.VMEM((8, 128), F32)],
        compiler_params=_params("parallel", "arbitrary"),
        name="mlstm",
    )(q, k, v, og, gates, gates_t, gbias_row, gbias_col, norm_g)


def _kv_kernel(mem_ref, w_ref, o_ref):
    o_ref[...] = _dot(mem_ref[...].astype(BF16), w_ref[...].astype(BF16)).astype(o_ref.dtype)


def _kv_call(mem, w, layer):
    n = mem.shape[0]
    tm = min(256, n)
    return pl.pallas_call(
        _kv_kernel,
        grid=(n // tm,),
        in_specs=[pl.BlockSpec((tm, D_MODEL), lambda i: (i, 0)), _layer_slab(w.shape, layer)],
        out_specs=pl.BlockSpec((tm, 2 * D_MODEL), lambda i: (i, 0)),
        out_shape=jax.ShapeDtypeStruct((n, 2 * D_MODEL), BF16),
        compiler_params=_params("parallel"),
        name="xattn_kv_proj",
    )(mem, w)


def _mix_out_xattn_kernel(x_ref, ysb_ref, ysw_ref, yml_ref, wmix_ref, g1_ref, b1_ref,
                          kv_ref, wq_ref, wo_ref, g_ref, b_ref, o_ref):
    mix = (_dot(ysb_ref[0], wmix_ref[0:SB_WIDTH, :].astype(BF16))
           + _dot(ysw_ref[0], wmix_ref[SB_WIDTH:SB_WIDTH + SWA_WIDTH, :].astype(BF16))
           + _dot(yml_ref[0], wmix_ref[SB_WIDTH + SWA_WIDTH:, :].astype(BF16)))
    x = _layer_norm(ALPHA * x_ref[0] + mix, g1_ref[...], b1_ref[...])
    q = _dot(x.astype(BF16), wq_ref[...].astype(BF16)).astype(BF16)
    outs = []
    for h in range(XATTN_HEADS):
        lo = h * XATTN_HEAD_DIM
        kh = kv_ref[0, :, lo:lo + XATTN_HEAD_DIM]
        vh = kv_ref[0, :, D_MODEL + lo:D_MODEL + lo + XATTN_HEAD_DIM]
        s = _dot_nt(q[:, lo:lo + XATTN_HEAD_DIM], kh) * (XATTN_HEAD_DIM ** -0.5)
        p = jnp.exp(s - jnp.max(s, axis=-1, keepdims=True))
        denom = jnp.sum(p, axis=-1, keepdims=True)
        outs.append((_dot(p.astype(BF16), vh) / denom).astype(BF16))
    attn = jnp.concatenate(outs, axis=-1)
    y = ALPHA * x + _dot(attn, wo_ref[...].astype(BF16))
    o_ref[0] = _layer_norm(y, g_ref[...], b_ref[...])


def _mix_out_xattn_call(x, y_sb, y_sw, y_ml, w_mix, kv, w_q, w_o, layer, ln_g, ln_b):
    bsz, s_len, _ = x.shape
    tm = min(TOKEN_TILE, s_len)
    tok = lambda w: pl.BlockSpec((1, tm, w), lambda bi, i: (bi, i, 0))
    ln = lambda t, k: _layer_slab(t.shape, 4 * layer + k)
    return pl.pallas_call(
        _mix_out_xattn_kernel,
        grid=(bsz, s_len // tm),
        in_specs=[tok(D_MODEL), tok(SB_WIDTH), tok(SWA_WIDTH), tok(ML_WIDTH),
                  _layer_slab(w_mix.shape, layer), ln(ln_g, 1), ln(ln_b, 1),
                  pl.BlockSpec((1, kv.shape[1], 2 * D_MODEL), lambda bi, i: (bi, 0, 0)),
                  _layer_slab(w_q.shape, layer), _layer_slab(w_o.shape, layer), ln(ln_g, 2), ln(ln_b, 2)],
        out_specs=tok(D_MODEL),
        out_shape=jax.ShapeDtypeStruct((bsz, s_len, D_MODEL), F32),
        compiler_params=_params("parallel", "parallel"),
        name="mix_out_xattn_ln",
    )(x, y_sb, y_sw, y_ml, w_mix, ln_g, ln_b, kv, w_q, w_o, ln_g, ln_b)


def _mixer_heads(x, w_in, conv_w, conv_b, i_bias, f_bias, norm_g, sinks, bias):
    bsz, s_len, _ = x.shape
    n = bsz * s_len
    x2 = x.reshape(n, D_MODEL)
    w_perm = jnp.concatenate(
        [w_in[:, :_ML_I], w_in[:, _ML_O:], w_in[:, _ML_I:_ML_O],
         jnp.zeros((D_MODEL, GATE_PAD - 2 * ML_HEADS), w_in.dtype)], axis=1).astype(BF16)
    sb_q, sb_k, sb_v, sw_q, sw_kv, ml_v, ml_o, gates, ml_q, ml_k, gates_t = _inproj_call(
        x2, w_perm, conv_w, conv_b[None, :], s_len)
    seq = lambda t: t.reshape(bsz, s_len, t.shape[-1])

    y_sb = _sb_call(seq(sb_q), seq(sb_k), seq(sb_v))
    y_sw = _swa_call(seq(sw_q), seq(sw_kv), bias, sinks)

    gate_bias = jnp.concatenate([i_bias, f_bias])
    gbias_row = jnp.pad(gate_bias, (0, GATE_PAD - 2 * ML_HEADS))[None, :]
    y_ml = _mlstm_call(seq(ml_q), seq(ml_k), seq(ml_v), seq(ml_o), seq(gates), gates_t,
                       gbias_row, gate_bias[:, None], norm_g[None, :])
    return y_sb, y_sw, y_ml


def kernel(x, mem, ffn1_w_in, ffn1_w_out, mix_w_in, ml_conv_w, ml_conv_b, ml_i_bias, ml_f_bias, ml_norm_g, swa_sinks, rel_bias, mix_w_out, xattn_w_q, xattn_w_kv, xattn_w_o, ffn2_w_in, ffn2_w_out, ln_g, ln_b):
    bsz, s_len, _ = x.shape
    n = bsz * s_len
    mem2 = mem.reshape(-1, D_MODEL)
    bias = _swa_bias(rel_bias)
    ln_g = ln_g.reshape(DEPTH * 4, 1, D_MODEL)
    ln_b = ln_b.reshape(DEPTH * 4, 1, D_MODEL)
    for l in range(DEPTH):
        x = _ffn_call(x.reshape(n, D_MODEL), ffn1_w_in, ffn1_w_out, l, ln_g, ln_b, 4 * l
                      ).reshape(bsz, s_len, D_MODEL)
        y_sb, y_sw, y_ml = _mixer_heads(x, mix_w_in[l], ml_conv_w[l], ml_conv_b[l], ml_i_bias[l], ml_f_bias[l],
                                        ml_norm_g[l], swa_sinks[l], bias)
        kv = _kv_call(mem2, xattn_w_kv, l).reshape(bsz, -1, 2 * D_MODEL)
        x = _mix_out_xattn_call(x, y_sb, y_sw, y_ml, mix_w_out, kv, xattn_w_q, xattn_w_o, l, ln_g, ln_b)
        x = _ffn_call(x.reshape(n, D_MODEL), ffn2_w_in, ffn2_w_out, l, ln_g, ln_b, 4 * l + 3
                      ).reshape(bsz, s_len, D_MODEL)
    return x
```

```python
import functools

import numpy as np
import jax
import jax.numpy as jnp
from jax import lax
from jax.experimental import pallas as pl
from jax.experimental.pallas import tpu as pltpu

F32 = jnp.float32
BF16 = jnp.bfloat16

D_MODEL = 1024
DEPTH = 2
HEAD_DIM = 64
SB_HEADS = 4
SWA_HEADS = 8
SWA_KV_HEADS = 2
SWA_GROUP = SWA_HEADS // SWA_KV_HEADS
ML_HEADS = 4
SB_WIDTH = SB_HEADS * HEAD_DIM
SWA_WIDTH = SWA_HEADS * HEAD_DIM
SWA_KV_WIDTH = SWA_KV_HEADS * HEAD_DIM
ML_WIDTH = ML_HEADS * HEAD_DIM
BLOCK = 128
CONV_WIDTH = 4
NUM_BUCKETS = 32
MAX_DISTANCE = 128
XATTN_HEADS = 4
XATTN_HEAD_DIM = D_MODEL // XATTN_HEADS
D_FF = 2816
ALPHA = (2 * DEPTH) ** 0.25
LN_EPS = 1e-5
NEG_BIG = -1e30
SB_SKIP_BELOW = -104.0

_SB_Q, _SB_K, _SB_V = 0, 256, 512
_SW_Q, _SW_K, _SW_V = 768, 1280, 1408
_ML_QK, _ML_V, _ML_I, _ML_F, _ML_O = 1536, 2048, 2304, 2308, 2312
N_IN = 2568
GATE_PAD = 128

VMEM_LIMIT = 56 * 1024 * 1024

SEQ_BLOCKS_PER_STEP = 8
INPROJ_TILE = 1024
SB_ROWS_PER_STEP = 4
TOKEN_TILE = 512
FF_CHUNK = 256


def _params(*sem):
    return pltpu.CompilerParams(dimension_semantics=sem, vmem_limit_bytes=VMEM_LIMIT)


def _resident(shape):
    nd = len(shape)
    return pl.BlockSpec(shape, lambda *_: (0,) * nd, pipeline_mode=pl.Buffered(1))


def _layer_slab(stacked_shape, index):
    rest = tuple(stacked_shape[1:])
    return pl.BlockSpec((None,) + rest, lambda *_: (index,) + (0,) * len(rest), pipeline_mode=pl.Buffered(1))


def _layer_norm(y, g, b):
    mu = jnp.mean(y, axis=-1, keepdims=True)
    d = y - mu
    var = jnp.mean(d * d, axis=-1, keepdims=True)
    return d * lax.rsqrt(var + LN_EPS) * g + b


def _log_sigmoid(x):
    return jnp.minimum(x, 0.0) - jnp.log(1.0 + jnp.exp(-jnp.abs(x)))


def _sigmoid(x):
    return 1.0 / (1.0 + jnp.exp(-x))


def _dot(a, b):
    return jnp.dot(a, b, preferred_element_type=F32)


def _dot_nt(a, b):
    return lax.dot_general(a, b, (((1,), (1,)), ((), ())), preferred_element_type=F32)


def _dot_tn(a, b):
    return lax.dot_general(a, b, (((0,), (0,)), ((), ())), preferred_element_type=F32)


def _split_bf16(x, pieces):
    out = []
    r = x
    for _ in range(pieces - 1):
        p = r.astype(BF16)
        out.append(p)
        r = r - p.astype(F32)
    out.append(r.astype(BF16))
    return out


def _ffn_kernel(x_ref, win_ref, wout_ref, g_ref, b_ref, o_ref, acc_ref):
    xb = x_ref[...].astype(BF16)
    for c in range(D_FF // FF_CHUNK):
        lo = c * FF_CHUNK
        a = _dot(xb, win_ref[:, lo:lo + FF_CHUNK].astype(BF16))
        b = _dot(xb, win_ref[:, D_FF + lo:D_FF + lo + FF_CHUNK].astype(BF16))
        h = (a * _sigmoid(a) * b).astype(BF16)
        part = _dot(h, wout_ref[lo:lo + FF_CHUNK, :].astype(BF16))
        if c == 0:
            acc_ref[...] = part
        else:
            acc_ref[...] += part
    y = ALPHA * x_ref[...] + 0.5 * acc_ref[...]
    o_ref[...] = _layer_norm(y, g_ref[...], b_ref[...])


def _ffn_call(x, w_in, w_out, layer, ln_g, ln_b, ln_index):
    n = x.shape[0]
    tm = min(TOKEN_TILE, n)
    row = lambda i: (i, 0)
    return pl.pallas_call(
        _ffn_kernel,
        grid=(n // tm,),
        in_specs=[pl.BlockSpec((tm, D_MODEL), row), _layer_slab(w_in.shape, layer), _layer_slab(w_out.shape, layer),
                  _layer_slab(ln_g.shape, ln_index), _layer_slab(ln_b.shape, ln_index)],
        out_specs=pl.BlockSpec((tm, D_MODEL), row),
        out_shape=jax.ShapeDtypeStruct((n, D_MODEL), F32),
        scratch_shapes=[pltpu.VMEM((tm, D_MODEL), F32)],
        compiler_params=_params("parallel"),
        name="ffn_ln",
    )(x, w_in, w_out, ln_g, ln_b)


_INPROJ_OUTS = (
    ("sb_q", 0, 256, BF16), ("sb_k", 256, 256, BF16), ("sb_v", 512, 256, BF16),
    ("sw_q", 768, 512, BF16), ("sw_kv", 1280, 256, BF16),
    ("ml_v", 2048, 256, BF16), ("ml_o", 2304, 256, F32), ("gates", 2560, GATE_PAD, F32),
)
_ML_QK_PERMUTED = 1536
N_IN_PADDED = 2688


def _inproj_kernel(tiles_per_seq, x_ref, w_ref, convw_ref, convb_ref, *refs):
    out_refs, (mlq_ref, mlk_ref, gates_t_ref, tail_ref) = refs[:len(_INPROJ_OUTS)], refs[len(_INPROJ_OUTS):]
    tm = x_ref.shape[0]
    xb = x_ref[...].astype(BF16)

    @pl.when(pl.program_id(0) % tiles_per_seq == 0)
    def _():
        tail_ref[...] = jnp.zeros_like(tail_ref)

    qk = _dot(xb, w_ref[:, _ML_QK_PERMUTED:_ML_QK_PERMUTED + 2 * ML_WIDTH])
    tail = tail_ref[...]
    tail_ref[...] = qk[tm - 8:, :]
    first_rows = lax.broadcasted_iota(jnp.int32, tail.shape, 0)
    conv = convb_ref[...] + convw_ref[CONV_WIDTH - 1:CONV_WIDTH, :] * qk
    for back in range(1, CONV_WIDTH):
        rolled = pltpu.roll(qk, back, axis=0)
        head = jnp.where(first_rows < back, pltpu.roll(tail, back, axis=0), rolled[0:8, :])
        shifted = jnp.concatenate([head, rolled[8:, :]], axis=0)
        conv = conv + convw_ref[CONV_WIDTH - 1 - back:CONV_WIDTH - back, :] * shifted
    act = conv * _sigmoid(conv)
    mlq_ref[...] = act[:, :ML_WIDTH].astype(mlq_ref.dtype)
    mlk_ref[...] = (act[:, ML_WIDTH:] * (HEAD_DIM ** -0.5)).astype(mlk_ref.dtype)

    for (name, lo, width, dt), o_ref in zip(_INPROJ_OUTS, out_refs):
        res = _dot(xb, w_ref[:, lo:lo + width])
        o_ref[...] = res.astype(dt)
        if name == "gates":
            gates_t_ref[...] = res.T[0:gates_t_ref.shape[0], :]


def _inproj_call(x, w, conv_w, conv_b, s_len):
    n = x.shape[0]
    tm = min(INPROJ_TILE, s_len)
    assert s_len % tm == 0
    row = lambda i: (i, 0)
    widths = [width for _, _, width, _ in _INPROJ_OUTS] + [ML_WIDTH, ML_WIDTH]
    dtypes = [dt for _, _, _, dt in _INPROJ_OUTS] + [BF16, BF16]
    return pl.pallas_call(
        functools.partial(_inproj_kernel, s_len // tm),
        grid=(n // tm,),
        in_specs=[pl.BlockSpec((tm, D_MODEL), row), _resident(w.shape), _resident(conv_w.shape),
                  _resident(conv_b.shape)],
        out_specs=[pl.BlockSpec((tm, width), row) for width in widths]
                  + [pl.BlockSpec((2 * ML_HEADS, tm), lambda i: (0, i))],
        out_shape=[jax.ShapeDtypeStruct((n, width), dt) for width, dt in zip(widths, dtypes)]
                  + [jax.ShapeDtypeStruct((2 * ML_HEADS, n), F32)],
        scratch_shapes=[pltpu.VMEM((8, 2 * ML_WIDTH), F32)],
        compiler_params=_params("arbitrary"),
        name="mix_in_proj",
    )(x, w, conv_w, conv_b)


def _sb_kernel(q_ref, k_ref, v_ref, o_ref, acc_ref, carry_ref):
    qi = pl.program_id(1)
    rows = q_ref.shape[0]
    lane_head = lax.broadcasted_iota(jnp.int32, (1, SB_WIDTH), 1) // HEAD_DIM
    heads = [lane_head == h for h in range(SB_HEADS)]
    q_stacks = []
    for bi in range(rows):
        q = (q_ref[bi].astype(F32) * (HEAD_DIM ** -0.5)).astype(BF16)
        q_stacks.append(jnp.concatenate([jnp.where(head, q, jnp.zeros_like(q)) for head in heads], axis=0))
    tall = rows * SB_HEADS * BLOCK
    strict = (lax.broadcasted_iota(jnp.int32, (tall, BLOCK), 1)
              < lax.broadcasted_iota(jnp.int32, (tall, BLOCK), 0) % BLOCK)
    row = lax.broadcasted_iota(jnp.int32, (2 * BLOCK, 2 * BLOCK), 0) % BLOCK
    col = lax.broadcasted_iota(jnp.int32, (2 * BLOCK, 2 * BLOCK), 1)
    suffix_and_total = jnp.where((row >= col) | (col >= BLOCK), 1.0, 0.0).astype(BF16)

    def step(j, diagonal):
        start = pl.multiple_of(j * BLOCK, BLOCK)
        z = jnp.concatenate([_dot_nt(q_stacks[bi], k_ref[bi, pl.ds(start, BLOCK), :]) for bi in range(rows)],
                            axis=0)
        lom = jnp.minimum(-z, 0.0) - jnp.log(1.0 + jnp.exp(-jnp.abs(z)))
        if diagonal:
            lom = jnp.where(strict, lom, 0.0)
        sums = _dot(jnp.concatenate(_split_bf16(lom, 2), axis=1), suffix_and_total)
        if diagonal:
            w = jnp.where(strict, jnp.exp(z + sums[:, :BLOCK]), 0.0)
            carry = sums[:, BLOCK:]
        else:
            carry = carry_ref[...]
            w = jnp.exp(z + sums[:, :BLOCK] + carry)
            carry = carry + sums[:, BLOCK:]
        carry_ref[...] = carry
        w = w.astype(BF16)
        for bi in range(rows):
            vb = v_ref[bi, pl.ds(start, BLOCK), :]
            v_heads = jnp.concatenate([jnp.where(head, vb, jnp.zeros_like(vb)) for head in heads], axis=0)
            tiles = [w[(bi * SB_HEADS + h) * BLOCK:(bi * SB_HEADS + h + 1) * BLOCK, :] for h in range(SB_HEADS)]
            pv = _dot(jnp.concatenate(tiles, axis=1), v_heads)
            if diagonal:
                acc_ref[bi] = pv
            else:
                acc_ref[bi] += pv
        return jnp.max(carry)

    def more_blocks(state):
        j, largest_carry = state
        return (j >= 0) & (largest_carry > SB_SKIP_BELOW)

    def next_block(state):
        j, _ = state
        return j - 1, step(j, False)

    lax.while_loop(more_blocks, next_block, (qi - 1, step(qi, True)))
    o_ref[...] = acc_ref[...].astype(o_ref.dtype)


def _sb_call(q, k, v):
    bsz, s_len, _ = q.shape
    rows = SB_ROWS_PER_STEP if bsz % SB_ROWS_PER_STEP == 0 else 1
    blk = pl.BlockSpec((rows, BLOCK, SB_WIDTH), lambda b, i: (b, i, 0))
    whole = pl.BlockSpec((rows, s_len, SB_WIDTH), lambda b, i: (b, 0, 0), pipeline_mode=pl.Buffered(1))
    return pl.pallas_call(
        _sb_kernel,
        grid=(bsz // rows, s_len // BLOCK),
        in_specs=[blk, whole, whole],
        out_specs=blk,
        out_shape=jax.ShapeDtypeStruct((bsz, s_len, SB_WIDTH), BF16),
        scratch_shapes=[pltpu.VMEM((rows, BLOCK, SB_WIDTH), F32),
                        pltpu.VMEM((rows * SB_HEADS * BLOCK, BLOCK), F32)],
        compiler_params=_params("parallel", "arbitrary"),
        name="stick_breaking",
    )(q, k, v)


def _swa_kernel(sinks_ref, q_ref, kvp_ref, kvc_ref, bias_ref, o_ref):
    first = jnp.minimum(pl.program_id(1), 1)
    q = (q_ref[0].astype(F32) * (HEAD_DIM ** -0.5)).astype(BF16)
    kv = jnp.concatenate([kvp_ref[0], kvc_ref[0]], axis=0)
    outs = []
    for h in range(SWA_HEADS):
        g = h // SWA_GROUP
        qh = q[:, h * HEAD_DIM:(h + 1) * HEAD_DIM]
        kh = kv[:, g * HEAD_DIM:(g + 1) * HEAD_DIM]
        vh = kv[:, SWA_KV_WIDTH + g * HEAD_DIM:SWA_KV_WIDTH + (g + 1) * HEAD_DIM]
        logits = _dot_nt(qh, kh) + bias_ref[first, h]
        sink = sinks_ref[h]
        m = jnp.maximum(jnp.max(logits, axis=-1, keepdims=True), sink)
        p = jnp.exp(logits - m)
        denom = jnp.sum(p, axis=-1, keepdims=True) + jnp.exp(sink - m)
        outs.append(_dot(p.astype(BF16), vh) / denom)
    o_ref[0] = jnp.concatenate(outs, axis=-1).astype(o_ref.dtype)


def _swa_call(q, kv, bias, sinks):
    bsz, s_len, _ = q.shape
    return pl.pallas_call(
        _swa_kernel,
        grid=(bsz, s_len // BLOCK),
        in_specs=[pl.BlockSpec(memory_space=pltpu.SMEM),
                  pl.BlockSpec((1, BLOCK, SWA_WIDTH), lambda b, i: (b, i, 0)),
                  pl.BlockSpec((1, BLOCK, 2 * SWA_KV_WIDTH), lambda b, i: (b, jnp.maximum(i - 1, 0), 0)),
                  pl.BlockSpec((1, BLOCK, 2 * SWA_KV_WIDTH), lambda b, i: (b, i, 0)),
                  _resident(bias.shape)],
        out_specs=pl.BlockSpec((1, BLOCK, SWA_WIDTH), lambda b, i: (b, i, 0)),
        out_shape=jax.ShapeDtypeStruct((bsz, s_len, SWA_WIDTH), BF16),
        compiler_params=_params("parallel", "arbitrary"),
        name="sliding_window",
    )(sinks, q, kv, kv, bias)


def _t5_bucket(dist):
    max_exact = NUM_BUCKETS // 2
    d = np.maximum(dist, 1)
    large = max_exact + (np.log(d / max_exact) / np.log(MAX_DISTANCE / max_exact)
                         * (NUM_BUCKETS - max_exact)).astype(np.int32)
    large = np.minimum(large, NUM_BUCKETS - 1)
    return np.where(dist < max_exact, dist, large).astype(np.int32)


def _swa_bias(rel_bias):
    qi = np.arange(BLOCK)[:, None]
    kj = np.arange(2 * BLOCK)[None, :]
    dist = qi + BLOCK - kj
    bucket = jnp.asarray(_t5_bucket(np.clip(dist, 0, None)))[None]
    table = rel_bias.astype(F32)
    bias = jnp.zeros((SWA_HEADS, BLOCK, 2 * BLOCK), F32)
    for b in range(NUM_BUCKETS):
        bias = jnp.where(bucket == b, table[b][:, None, None], bias)
    in_window = (dist >= 0) & (dist < BLOCK)
    first = jnp.asarray(in_window & (kj >= BLOCK))[None]
    rest = jnp.asarray(in_window)[None]
    return jnp.stack([jnp.where(first, bias, NEG_BIG), jnp.where(rest, bias, NEG_BIG)])


def _mlstm_kernel(q_ref, k_ref, v_ref, og_ref, gates_ref, gates_t_ref,
                  gbias_row_ref, gbias_col_ref, normg_ref, y_ref,
                  c_ref, n_ref, m_ref):
    @pl.when(pl.program_id(1) == 0)
    def _():
        c_ref[...] = jnp.zeros_like(c_ref)
        n_ref[...] = jnp.zeros_like(n_ref)
        m_ref[...] = jnp.zeros_like(m_ref)

    iota = lambda shape, axis: lax.broadcasted_iota(jnp.int32, shape, axis)
    row = iota((BLOCK, BLOCK), 0)
    col = iota((BLOCK, BLOCK), 1)
    causal = col <= row
    one_where = lambda cond: jnp.where(cond, 1.0, 0.0).astype(BF16)
    tri = one_where(causal)
    tri_t = one_where(row <= col)
    heads = [iota((1, ML_WIDTH), 1) // HEAD_DIM == h for h in range(ML_HEADS)]
    same_head = iota((ML_WIDTH, ML_WIDTH), 0) // HEAD_DIM == iota((ML_WIDTH, ML_WIDTH), 1) // HEAD_DIM
    head_mean = jnp.where(same_head, 1.0 / HEAD_DIM, 0.0).astype(BF16)
    sel_tile = one_where(iota((BLOCK, ML_HEADS * BLOCK), 0) == iota((BLOCK, ML_HEADS * BLOCK), 1) // BLOCK)
    sel_head = one_where(iota((BLOCK, ML_WIDTH), 0) == iota((BLOCK, ML_WIDTH), 1) // HEAD_DIM)
    rows_of_head = one_where(iota((ML_HEADS * BLOCK, ML_WIDTH), 0) // BLOCK
                             == iota((ML_HEADS * BLOCK, ML_WIDTH), 1) // HEAD_DIM)

    def spread(tile, selector):
        live = iota(tile.shape, 1) < ML_HEADS
        return sum(_dot(p, selector) for p in _split_bf16(jnp.where(live, tile, 0.0), 2))

    total = q_ref.shape[1]
    chunks = [slice(c * BLOCK, (c + 1) * BLOCK) for c in range(total // BLOCK)]
    row_in_chunk = iota((total, BLOCK), 0) % BLOCK
    causal_heads = iota((ML_HEADS * BLOCK, BLOCK), 1) <= iota((ML_HEADS * BLOCK, BLOCK), 0) % BLOCK
    qb = q_ref[0]
    kb = k_ref[0]
    v_all = v_ref[0]

    gates = gates_ref[0] + gbias_row_ref[...]
    gates_t = gates_t_ref[...] + gbias_col_ref[...]
    lsf = _split_bf16(_log_sigmoid(gates), 3)
    b_cols = jnp.concatenate([sum(_dot(tri, p[rows, :]) for p in lsf) for rows in chunks], axis=0)
    lsf_t = _log_sigmoid(gates_t)
    lsf_t = jnp.concatenate([lsf_t[:, rows] for rows in chunks], axis=0)
    b_rows = sum(_dot(p, tri_t) for p in _split_bf16(lsf_t, 3))
    b_tile = pltpu.roll(b_cols, BLOCK - ML_HEADS, axis=1)
    a_tile = gates - b_tile

    run_max = a_tile
    shift = 1
    while shift < BLOCK:
        run_max = jnp.where(row_in_chunk >= shift,
                            jnp.maximum(run_max, pltpu.roll(run_max, shift, axis=0)), run_max)
        shift *= 2

    m_prev = m_ref[0:1, :]
    m_prevs, m_news, decays, w_logs = [], [], [], []
    for c, rows in enumerate(chunks):
        b_end = b_tile[(c + 1) * BLOCK - 1:(c + 1) * BLOCK, :]
        w_log = b_end - b_tile[rows, :] + gates[rows, :]
        m_new = jnp.maximum(b_end + m_prev, jnp.max(w_log, axis=0, keepdims=True))
        decays.append(jnp.broadcast_to(jnp.exp(b_end + m_prev - m_new), (8, BLOCK)))
        m_prevs.append(jnp.broadcast_to(m_prev, (BLOCK, BLOCK)))
        m_news.append(jnp.broadcast_to(m_new, (BLOCK, BLOCK)))
        w_logs.append(w_log)
        m_prev = m_new
    m_ref[...] = jnp.broadcast_to(m_prev, m_ref.shape)
    m_before = jnp.concatenate(m_prevs, axis=0)
    u = jnp.maximum(m_before, run_max)
    neg_u = spread(-u, sel_tile)
    inter = spread(jnp.exp(m_before - u), sel_head)
    floor = spread(jnp.exp(-(b_tile + u)), sel_head)
    w = spread(jnp.exp(jnp.concatenate(w_logs, axis=0) - jnp.concatenate(m_news, axis=0)), sel_head)
    decay = spread(jnp.concatenate(decays, axis=0), sel_head)

    s_cat = []
    for c, rows in enumerate(chunks):
        q_c = qb[rows, :]
        q_stack = jnp.concatenate([jnp.where(head, q_c, jnp.zeros_like(q_c)) for head in heads], axis=0)
        a_rows = gates_t[0:ML_HEADS, rows] - b_rows[8 * c + ML_HEADS:8 * c + 2 * ML_HEADS, :]
        log_d = jnp.concatenate([a_rows[h:h + 1, :] + neg_u[rows, h * BLOCK:(h + 1) * BLOCK]
                                 for h in range(ML_HEADS)], axis=0)
        s = (_dot_nt(q_stack, kb[rows, :]) * jnp.exp(jnp.where(causal_heads, log_d, NEG_BIG))).astype(BF16)
        s_cat.append(jnp.concatenate([s[h * BLOCK:(h + 1) * BLOCK, :] for h in range(ML_HEADS)], axis=1))
    s_cat = jnp.concatenate(s_cat, axis=0)
    den_within = _dot(s_cat, rows_of_head)
    wk = (w * kb.astype(F32)).astype(BF16)
    v_ones = jnp.concatenate([v_all, jnp.ones_like(v_all)], axis=1)

    c_state = c_ref[...]
    n_state = n_ref[...]
    nums, dens = [], []
    for c, rows in enumerate(chunks):
        v_c = v_all[rows, :]
        v_heads = jnp.concatenate([jnp.where(head, v_c, jnp.zeros_like(v_c)) for head in heads], axis=0)
        carried = _dot(qb[rows, :], jnp.concatenate([c_state, n_state], axis=1).astype(BF16))
        nums.append(inter[rows, :] * carried[:, :ML_WIDTH] + _dot(s_cat[rows, :], v_heads))
        dens.append(inter[rows, :] * carried[:, ML_WIDTH:] + den_within[rows, :])
        update = _dot_tn(wk[rows, :], v_ones[rows, :])
        decay_c = decay[8 * c:8 * c + 1, :]
        c_state = decay_c * c_state + jnp.where(same_head, update[:, :ML_WIDTH], 0.0)
        n_state = decay_c * n_state + jnp.where(same_head, update[:, ML_WIDTH:], 0.0)
    c_ref[...] = c_state
    n_ref[...] = n_state

    hid = jnp.concatenate(nums, axis=0) / jnp.maximum(jnp.abs(jnp.concatenate(dens, axis=0)), floor)
    mu = sum(_dot(p, head_mean) for p in _split_bf16(hid, 2))
    dev = hid - mu
    var = sum(_dot(p, head_mean) for p in _split_bf16(dev * dev, 2))
    y = _sigmoid(og_ref[0]) * (dev * lax.rsqrt(var + LN_EPS) * normg_ref[...])
    y_ref[0] = y.astype(y_ref.dtype)


def _mlstm_call(q, k, v, og, gates, gates_t, gbias_row, gbias_col, norm_g):
    bsz, s_len, _ = q.shape
    nb = SEQ_BLOCKS_PER_STEP if (s_len // BLOCK) % SEQ_BLOCKS_PER_STEP == 0 else 1
    tok = lambda w: pl.BlockSpec((1, nb * BLOCK, w), lambda b, c: (b, c, 0))
    return pl.pallas_call(
        _mlstm_kernel,
        grid=(bsz, s_len // (nb * BLOCK)),
        in_specs=[tok(ML_WIDTH), tok(ML_WIDTH), tok(ML_WIDTH), tok(ML_WIDTH), tok(GATE_PAD),
                  pl.BlockSpec((2 * ML_HEADS, nb * BLOCK), lambda b, c: (0, b * (s_len // (nb * BLOCK)) + c)),
                  _resident(gbias_row.shape), _resident(gbias_col.shape), _resident(norm_g.shape)],
        out_specs=tok(ML_WIDTH),
        out_shape=jax.ShapeDtypeStruct((bsz, s_len, ML_WIDTH), BF16),
        scratch_shapes=[pltpu.VMEM((ML_WIDTH, ML_WIDTH), F32),
                        pltpu.VMEM((ML_WIDTH, ML_WIDTH), F32),
                        pltpu.VMEM((8, 128), F32)],
        compiler_params=_params("parallel", "arbitrary"),
        name="mlstm",
    )(q, k, v, og, gates, gates_t, gbias_row, gbias_col, norm_g)


def _kv_kernel(mem_ref, w_ref, o_ref):
    o_ref[...] = _dot(mem_ref[...].astype(BF16), w_ref[...].astype(BF16)).astype(o_ref.dtype)


def _kv_call(mem, w, layer):
    n = mem.shape[0]
    tm = min(256, n)
    return pl.pallas_call(
        _kv_kernel,
        grid=(n // tm,),
        in_specs=[pl.BlockSpec((tm, D_MODEL), lambda i: (i, 0)), _layer_slab(w.shape, layer)],
        out_specs=pl.BlockSpec((tm, 2 * D_MODEL), lambda i: (i, 0)),
        out_shape=jax.ShapeDtypeStruct((n, 2 * D_MODEL), BF16),
        compiler_params=_params("parallel"),
        name="xattn_kv_proj",
    )(mem, w)


def _mix_out_xattn_kernel(x_ref, ysb_ref, ysw_ref, yml_ref, wmix_ref, g1_ref, b1_ref,
                          kv_ref, wq_ref, wo_ref, g_ref, b_ref, o_ref):
    mix = (_dot(ysb_ref[0], wmix_ref[0:SB_WIDTH, :].astype(BF16))
           + _dot(ysw_ref[0], wmix_ref[SB_WIDTH:SB_WIDTH + SWA_WIDTH, :].astype(BF16))
           + _dot(yml_ref[0], wmix_ref[SB_WIDTH + SWA_WIDTH:, :].astype(BF16)))
    x = _layer_norm(ALPHA * x_ref[0] + mix, g1_ref[...], b1_ref[...])
    q = _dot(x.astype(BF16), wq_ref[...].astype(BF16)).astype(BF16)
    outs = []
    for h in range(XATTN_HEADS):
        lo = h * XATTN_HEAD_DIM
        kh = kv_ref[0, :, lo:lo + XATTN_HEAD_DIM]
        vh = kv_ref[0, :, D_MODEL + lo:D_MODEL + lo + XATTN_HEAD_DIM]
        s = _dot_nt(q[:, lo:lo + XATTN_HEAD_DIM], kh) * (XATTN_HEAD_DIM ** -0.5)
        p = jnp.exp(s - jnp.max(s, axis=-1, keepdims=True))
        denom = jnp.sum(p, axis=-1, keepdims=True)
        outs.append((_dot(p.astype(BF16), vh) / denom).astype(BF16))
    attn = jnp.concatenate(outs, axis=-1)
    y = ALPHA * x + _dot(attn, wo_ref[...].astype(BF16))
    o_ref[0] = _layer_norm(y, g_ref[...], b_ref[...])


def _mix_out_xattn_call(x, y_sb, y_sw, y_ml, w_mix, kv, w_q, w_o, layer, ln_g, ln_b):
    bsz, s_len, _ = x.shape
    tm = min(TOKEN_TILE, s_len)
    tok = lambda w: pl.BlockSpec((1, tm, w), lambda bi, i: (bi, i, 0))
    ln = lambda t, k: _layer_slab(t.shape, 4 * layer + k)
    return pl.pallas_call(
        _mix_out_xattn_kernel,
        grid=(bsz, s_len // tm),
        in_specs=[tok(D_MODEL), tok(SB_WIDTH), tok(SWA_WIDTH), tok(ML_WIDTH),
                  _layer_slab(w_mix.shape, layer), ln(ln_g, 1), ln(ln_b, 1),
                  pl.BlockSpec((1, kv.shape[1], 2 * D_MODEL), lambda bi, i: (bi, 0, 0)),
                  _layer_slab(w_q.shape, layer), _layer_slab(w_o.shape, layer), ln(ln_g, 2), ln(ln_b, 2)],
        out_specs=tok(D_MODEL),
        out_shape=jax.ShapeDtypeStruct((bsz, s_len, D_MODEL), F32),
        compiler_params=_params("parallel", "parallel"),
        name="mix_out_xattn_ln",
    )(x, y_sb, y_sw, y_ml, w_mix, ln_g, ln_b, kv, w_q, w_o, ln_g, ln_b)


def _mixer_heads(x, w_in, conv_w, conv_b, i_bias, f_bias, norm_g, sinks, bias):
    bsz, s_len, _ = x.shape
    n = bsz * s_len
    x2 = x.reshape(n, D_MODEL)
    w_perm = jnp.concatenate(
        [w_in[:, :_ML_I], w_in[:, _ML_O:], w_in[:, _ML_I:_ML_O],
         jnp.zeros((D_MODEL, GATE_PAD - 2 * ML_HEADS), w_in.dtype)], axis=1).astype(BF16)
    sb_q, sb_k, sb_v, sw_q, sw_kv, ml_v, ml_o, gates, ml_q, ml_k, gates_t = _inproj_call(
        x2, w_perm, conv_w, conv_b[None, :], s_len)
    seq = lambda t: t.reshape(bsz, s_len, t.shape[-1])

    y_sb = _sb_call(seq(sb_q), seq(sb_k), seq(sb_v))
    y_sw = _swa_call(seq(sw_q), seq(sw_kv), bias, sinks)

    gate_bias = jnp.concatenate([i_bias, f_bias])
    gbias_row = jnp.pad(gate_bias, (0, GATE_PAD - 2 * ML_HEADS))[None, :]
    y_ml = _mlstm_call(seq(ml_q), seq(ml_k), seq(ml_v), seq(ml_o), seq(gates), gates_t,
                       gbias_row, gate_bias[:, None], norm_g[None, :])
    return y_sb, y_sw, y_ml


def kernel(x, mem, ffn1_w_in, ffn1_w_out, mix_w_in, ml_conv_w, ml_conv_b, ml_i_bias, ml_f_bias, ml_norm_g, swa_sinks, rel_bias, mix_w_out, xattn_w_q, xattn_w_kv, xattn_w_o, ffn2_w_in, ffn2_w_out, ln_g, ln_b):
    bsz, s_len, _ = x.shape
    n = bsz * s_len
    mem2 = mem.reshape(-1, D_MODEL)
    bias = _swa_bias(rel_bias)
    ln_g = ln_g.reshape(DEPTH * 4, 1, D_MODEL)
    ln_b = ln_b.reshape(DEPTH * 4, 1, D_MODEL)
    for l in range(DEPTH):
        x = _ffn_call(x.reshape(n, D_MODEL), ffn1_w_in, ffn1_w_out, l, ln_g, ln_b, 4 * l
                      ).reshape(bsz, s_len, D_MODEL)
        y_sb, y_sw, y_ml = _mixer_heads(x, mix_w_in[l], ml_conv_w[l], ml_conv_b[l], ml_i_bias[l], ml_f_bias[l],
                                        ml_norm_g[l], swa_sinks[l], bias)
        kv = _kv_call(mem2, xattn_w_kv, l).reshape(bsz, -1, 2 * D_MODEL)
        x = _mix_out_xattn_call(x, y_sb, y_sw, y_ml, mix_w_out, kv, xattn_w_q, xattn_w_o, l, ln_g, ln_b)
        x = _ffn_call(x.reshape(n, D_MODEL), ffn2_w_in, ffn2_w_out, l, ln_g, ln_b, 4 * l + 3
                      ).reshape(bsz, s_len, D_MODEL)
    return x
```

```python
import functools

import numpy as np
import jax
import jax.numpy as jnp
from jax import lax
from jax.experimental import pallas as pl
from jax.experimental.pallas import tpu as pltpu

F32 = jnp.float32
BF16 = jnp.bfloat16

D_MODEL = 1024
DEPTH = 2
HEAD_DIM = 64
SB_HEADS = 4
SWA_HEADS = 8
SWA_KV_HEADS = 2
SWA_GROUP = SWA_HEADS // SWA_KV_HEADS
ML_HEADS = 4
SB_WIDTH = SB_HEADS * HEAD_DIM
SWA_WIDTH = SWA_HEADS * HEAD_DIM
SWA_KV_WIDTH = SWA_KV_HEADS * HEAD_DIM
ML_WIDTH = ML_HEADS * HEAD_DIM
BLOCK = 128
CONV_WIDTH = 4
NUM_BUCKETS = 32
MAX_DISTANCE = 128
XATTN_HEADS = 4
XATTN_HEAD_DIM = D_MODEL // XATTN_HEADS
D_FF = 2816
ALPHA = (2 * DEPTH) ** 0.25
LN_EPS = 1e-5
NEG_BIG = -1e30
SB_SKIP_BELOW = -104.0

_SB_Q, _SB_K, _SB_V = 0, 256, 512
_SW_Q, _SW_K, _SW_V = 768, 1280, 1408
_ML_QK, _ML_V, _ML_I, _ML_F, _ML_O = 1536, 2048, 2304, 2308, 2312
N_IN = 2568
GATE_PAD = 128

VMEM_LIMIT = 56 * 1024 * 1024

SEQ_BLOCKS_PER_STEP = 16
INPROJ_TILE = 1024
XATTN_TILE = 1024
SB_ROWS_PER_STEP = 4
TOKEN_TILE = 512
FF_CHUNK = 256


def _params(*sem):
    return pltpu.CompilerParams(dimension_semantics=sem, vmem_limit_bytes=VMEM_LIMIT)


def _resident(shape):
    nd = len(shape)
    return pl.BlockSpec(shape, lambda *_: (0,) * nd, pipeline_mode=pl.Buffered(1))


def _layer_slab(stacked_shape, index):
    rest = tuple(stacked_shape[1:])
    return pl.BlockSpec((None,) + rest, lambda *_: (index,) + (0,) * len(rest), pipeline_mode=pl.Buffered(1))


def _layer_norm(y, g, b):
    mu = jnp.mean(y, axis=-1, keepdims=True)
    d = y - mu
    var = jnp.mean(d * d, axis=-1, keepdims=True)
    return d * lax.rsqrt(var + LN_EPS) * g + b


def _log_sigmoid(x):
    return jnp.minimum(x, 0.0) - jnp.log(1.0 + jnp.exp(-jnp.abs(x)))


def _sigmoid(x):
    return 1.0 / (1.0 + jnp.exp(-x))


def _dot(a, b):
    return jnp.dot(a, b, preferred_element_type=F32)


def _dot_nt(a, b):
    return lax.dot_general(a, b, (((1,), (1,)), ((), ())), preferred_element_type=F32)


def _dot_tn(a, b):
    return lax.dot_general(a, b, (((0,), (0,)), ((), ())), preferred_element_type=F32)


def _split_bf16(x, pieces):
    out = []
    r = x
    for _ in range(pieces - 1):
        p = r.astype(BF16)
        out.append(p)
        r = r - p.astype(F32)
    out.append(r.astype(BF16))
    return out


def _ffn_kernel(x_ref, win_ref, wout_ref, g_ref, b_ref, o_ref, acc_ref):
    xb = x_ref[...].astype(BF16)
    for c in range(D_FF // FF_CHUNK):
        lo = c * FF_CHUNK
        a = _dot(xb, win_ref[:, lo:lo + FF_CHUNK].astype(BF16))
        b = _dot(xb, win_ref[:, D_FF + lo:D_FF + lo + FF_CHUNK].astype(BF16))
        h = (a * _sigmoid(a) * b).astype(BF16)
        part = _dot(h, wout_ref[lo:lo + FF_CHUNK, :].astype(BF16))
        if c == 0:
            acc_ref[...] = part
        else:
            acc_ref[...] += part
    y = ALPHA * x_ref[...] + 0.5 * acc_ref[...]
    o_ref[...] = _layer_norm(y, g_ref[...], b_ref[...])


def _ffn_call(x, w_in, w_out, layer, ln_g, ln_b, ln_index):
    n = x.shape[0]
    tm = min(TOKEN_TILE, n)
    row = lambda i: (i, 0)
    return pl.pallas_call(
        _ffn_kernel,
        grid=(n // tm,),
        in_specs=[pl.BlockSpec((tm, D_MODEL), row), _layer_slab(w_in.shape, layer), _layer_slab(w_out.shape, layer),
                  _layer_slab(ln_g.shape, ln_index), _layer_slab(ln_b.shape, ln_index)],
        out_specs=pl.BlockSpec((tm, D_MODEL), row),
        out_shape=jax.ShapeDtypeStruct((n, D_MODEL), F32),
        scratch_shapes=[pltpu.VMEM((tm, D_MODEL), F32)],
        compiler_params=_params("parallel"),
        name="ffn_ln",
    )(x, w_in, w_out, ln_g, ln_b)


_INPROJ_OUTS = (
    ("sb_q", 0, 256, BF16), ("sb_k", 256, 256, BF16), ("sb_v", 512, 256, BF16),
    ("sw_q", 768, 512, BF16), ("sw_kv", 1280, 256, BF16),
    ("ml_v", 2048, 256, BF16), ("ml_o", 2304, 256, F32), ("gates", 2560, GATE_PAD, F32),
)
_ML_QK_PERMUTED = 1536
N_IN_PADDED = 2688


def _inproj_kernel(tiles_per_seq, x_ref, w_ref, convw_ref, convb_ref, *refs):
    out_refs, (mlq_ref, mlk_ref, gates_t_ref, tail_ref) = refs[:len(_INPROJ_OUTS)], refs[len(_INPROJ_OUTS):]
    tm = x_ref.shape[0]
    xb = x_ref[...].astype(BF16)

    @pl.when(pl.program_id(0) % tiles_per_seq == 0)
    def _():
        tail_ref[...] = jnp.zeros_like(tail_ref)

    qk = _dot(xb, w_ref[:, _ML_QK_PERMUTED:_ML_QK_PERMUTED + 2 * ML_WIDTH])
    tail = tail_ref[...]
    tail_ref[...] = qk[tm - 8:, :]
    first_rows = lax.broadcasted_iota(jnp.int32, tail.shape, 0)
    conv = convb_ref[...] + convw_ref[CONV_WIDTH - 1:CONV_WIDTH, :] * qk
    for back in range(1, CONV_WIDTH):
        rolled = pltpu.roll(qk, back, axis=0)
        head = jnp.where(first_rows < back, pltpu.roll(tail, back, axis=0), rolled[0:8, :])
        shifted = jnp.concatenate([head, rolled[8:, :]], axis=0)
        conv = conv + convw_ref[CONV_WIDTH - 1 - back:CONV_WIDTH - back, :] * shifted
    act = conv * _sigmoid(conv)
    mlq_ref[...] = act[:, :ML_WIDTH].astype(mlq_ref.dtype)
    mlk_ref[...] = (act[:, ML_WIDTH:] * (HEAD_DIM ** -0.5)).astype(mlk_ref.dtype)

    for (name, lo, width, dt), o_ref in zip(_INPROJ_OUTS, out_refs):
        res = _dot(xb, w_ref[:, lo:lo + width])
        o_ref[...] = res.astype(dt)
        if name == "gates":
            gates_t_ref[...] = res.T[0:gates_t_ref.shape[0], :]


def _inproj_call(x, w, conv_w, conv_b, s_len):
    n = x.shape[0]
    tm = min(INPROJ_TILE, s_len)
    assert s_len % tm == 0
    row = lambda i: (i, 0)
    widths = [width for _, _, width, _ in _INPROJ_OUTS] + [ML_WIDTH, ML_WIDTH]
    dtypes = [dt for _, _, _, dt in _INPROJ_OUTS] + [BF16, BF16]
    return pl.pallas_call(
        functools.partial(_inproj_kernel, s_len // tm),
        grid=(n // tm,),
        in_specs=[pl.BlockSpec((tm, D_MODEL), row), _resident(w.shape), _resident(conv_w.shape),
                  _resident(conv_b.shape)],
        out_specs=[pl.BlockSpec((tm, width), row) for width in widths]
                  + [pl.BlockSpec((2 * ML_HEADS, tm), lambda i: (0, i))],
        out_shape=[jax.ShapeDtypeStruct((n, width), dt) for width, dt in zip(widths, dtypes)]
                  + [jax.ShapeDtypeStruct((2 * ML_HEADS, n), F32)],
        scratch_shapes=[pltpu.VMEM((8, 2 * ML_WIDTH), F32)],
        compiler_params=_params("arbitrary"),
        name="mix_in_proj",
    )(x, w, conv_w, conv_b)


def _sb_kernel(q_ref, k_ref, v_ref, o_ref, acc_ref, carry_ref):
    qi = pl.program_id(1)
    rows = q_ref.shape[0]
    lane_head = lax.broadcasted_iota(jnp.int32, (1, SB_WIDTH), 1) // HEAD_DIM
    heads = [lane_head == h for h in range(SB_HEADS)]
    q_stacks = []
    for bi in range(rows):
        q = (q_ref[bi].astype(F32) * (HEAD_DIM ** -0.5)).astype(BF16)
        q_stacks.append(jnp.concatenate([jnp.where(head, q, jnp.zeros_like(q)) for head in heads], axis=0))
    tall = rows * SB_HEADS * BLOCK
    strict = (lax.broadcasted_iota(jnp.int32, (tall, BLOCK), 1)
              < lax.broadcasted_iota(jnp.int32, (tall, BLOCK), 0) % BLOCK)
    row = lax.broadcasted_iota(jnp.int32, (2 * BLOCK, 2 * BLOCK), 0) % BLOCK
    col = lax.broadcasted_iota(jnp.int32, (2 * BLOCK, 2 * BLOCK), 1)
    suffix_and_total = jnp.where((row >= col) | (col >= BLOCK), 1.0, 0.0).astype(BF16)

    def step(j, diagonal):
        start = pl.multiple_of(j * BLOCK, BLOCK)
        z = jnp.concatenate([_dot_nt(q_stacks[bi], k_ref[bi, pl.ds(start, BLOCK), :]) for bi in range(rows)],
                            axis=0)
        lom = jnp.minimum(-z, 0.0) - jnp.log(1.0 + jnp.exp(-jnp.abs(z)))
        if diagonal:
            lom = jnp.where(strict, lom, 0.0)
        sums = _dot(jnp.concatenate(_split_bf16(lom, 2), axis=1), suffix_and_total)
        if diagonal:
            w = jnp.where(strict, jnp.exp(z + sums[:, :BLOCK]), 0.0)
            carry = sums[:, BLOCK:]
        else:
            carry = carry_ref[...]
            w = jnp.exp(z + sums[:, :BLOCK] + carry)
            carry = carry + sums[:, BLOCK:]
        carry_ref[...] = carry
        w = w.astype(BF16)
        for bi in range(rows):
            vb = v_ref[bi, pl.ds(start, BLOCK), :]
            v_heads = jnp.concatenate([jnp.where(head, vb, jnp.zeros_like(vb)) for head in heads], axis=0)
            tiles = [w[(bi * SB_HEADS + h) * BLOCK:(bi * SB_HEADS + h + 1) * BLOCK, :] for h in range(SB_HEADS)]
            pv = _dot(jnp.concatenate(tiles, axis=1), v_heads)
            if diagonal:
                acc_ref[bi] = pv
            else:
                acc_ref[bi] += pv
        return jnp.max(carry)

    def more_blocks(state):
        j, largest_carry = state
        return (j >= 0) & (largest_carry > SB_SKIP_BELOW)

    def next_block(state):
        j, _ = state
        return j - 1, step(j, False)

    lax.while_loop(more_blocks, next_block, (qi - 1, step(qi, True)))
    o_ref[...] = acc_ref[...].astype(o_ref.dtype)


def _sb_call(q, k, v):
    bsz, s_len, _ = q.shape
    rows = SB_ROWS_PER_STEP if bsz % SB_ROWS_PER_STEP == 0 else 1
    blk = pl.BlockSpec((rows, BLOCK, SB_WIDTH), lambda b, i: (b, i, 0))
    whole = pl.BlockSpec((rows, s_len, SB_WIDTH), lambda b, i: (b, 0, 0), pipeline_mode=pl.Buffered(1))
    return pl.pallas_call(
        _sb_kernel,
        grid=(bsz // rows, s_len // BLOCK),
        in_specs=[blk, whole, whole],
        out_specs=blk,
        out_shape=jax.ShapeDtypeStruct((bsz, s_len, SB_WIDTH), BF16),
        scratch_shapes=[pltpu.VMEM((rows, BLOCK, SB_WIDTH), F32),
                        pltpu.VMEM((rows * SB_HEADS * BLOCK, BLOCK), F32)],
        compiler_params=_params("parallel", "arbitrary"),
        name="stick_breaking",
    )(q, k, v)


def _swa_kernel(sinks_ref, q_ref, kvp_ref, kvc_ref, bias_ref, o_ref):
    first = jnp.minimum(pl.program_id(1), 1)
    q = (q_ref[0].astype(F32) * (HEAD_DIM ** -0.5)).astype(BF16)
    kv = jnp.concatenate([kvp_ref[0], kvc_ref[0]], axis=0)
    outs = []
    for h in range(SWA_HEADS):
        g = h // SWA_GROUP
        qh = q[:, h * HEAD_DIM:(h + 1) * HEAD_DIM]
        kh = kv[:, g * HEAD_DIM:(g + 1) * HEAD_DIM]
        vh = kv[:, SWA_KV_WIDTH + g * HEAD_DIM:SWA_KV_WIDTH + (g + 1) * HEAD_DIM]
        logits = _dot_nt(qh, kh) + bias_ref[first, h]
        sink = sinks_ref[h]
        m = jnp.maximum(jnp.max(logits, axis=-1, keepdims=True), sink)
        p = jnp.exp(logits - m)
        denom = jnp.sum(p, axis=-1, keepdims=True) + jnp.exp(sink - m)
        outs.append(_dot(p.astype(BF16), vh) / denom)
    o_ref[0] = jnp.concatenate(outs, axis=-1).astype(o_ref.dtype)


def _swa_call(q, kv, bias, sinks):
    bsz, s_len, _ = q.shape
    return pl.pallas_call(
        _swa_kernel,
        grid=(bsz, s_len // BLOCK),
        in_specs=[pl.BlockSpec(memory_space=pltpu.SMEM),
                  pl.BlockSpec((1, BLOCK, SWA_WIDTH), lambda b, i: (b, i, 0)),
                  pl.BlockSpec((1, BLOCK, 2 * SWA_KV_WIDTH), lambda b, i: (b, jnp.maximum(i - 1, 0), 0)),
                  pl.BlockSpec((1, BLOCK, 2 * SWA_KV_WIDTH), lambda b, i: (b, i, 0)),
                  _resident(bias.shape)],
        out_specs=pl.BlockSpec((1, BLOCK, SWA_WIDTH), lambda b, i: (b, i, 0)),
        out_shape=jax.ShapeDtypeStruct((bsz, s_len, SWA_WIDTH), BF16),
        compiler_params=_params("parallel", "arbitrary"),
        name="sliding_window",
    )(sinks, q, kv, kv, bias)


def _t5_bucket(dist):
    max_exact = NUM_BUCKETS // 2
    d = np.maximum(dist, 1)
    large = max_exact + (np.log(d / max_exact) / np.log(MAX_DISTANCE / max_exact)
                         * (NUM_BUCKETS - max_exact)).astype(np.int32)
    large = np.minimum(large, NUM_BUCKETS - 1)
    return np.where(dist < max_exact, dist, large).astype(np.int32)


def _swa_bias(rel_bias):
    qi = np.arange(BLOCK)[:, None]
    kj = np.arange(2 * BLOCK)[None, :]
    dist = qi + BLOCK - kj
    bucket = jnp.asarray(_t5_bucket(np.clip(dist, 0, None)))[None]
    table = rel_bias.astype(F32)
    bias = jnp.zeros((SWA_HEADS, BLOCK, 2 * BLOCK), F32)
    for b in range(NUM_BUCKETS):
        bias = jnp.where(bucket == b, table[b][:, None, None], bias)
    in_window = (dist >= 0) & (dist < BLOCK)
    first = jnp.asarray(in_window & (kj >= BLOCK))[None]
    rest = jnp.asarray(in_window)[None]
    return jnp.stack([jnp.where(first, bias, NEG_BIG), jnp.where(rest, bias, NEG_BIG)])


def _mlstm_kernel(q_ref, k_ref, v_ref, og_ref, gates_ref, gates_t_ref,
                  gbias_row_ref, gbias_col_ref, normg_ref, y_ref,
                  c_ref, n_ref, m_ref):
    @pl.when(pl.program_id(1) == 0)
    def _():
        c_ref[...] = jnp.zeros_like(c_ref)
        n_ref[...] = jnp.zeros_like(n_ref)
        m_ref[...] = jnp.zeros_like(m_ref)

    iota = lambda shape, axis: lax.broadcasted_iota(jnp.int32, shape, axis)
    row = iota((BLOCK, BLOCK), 0)
    col = iota((BLOCK, BLOCK), 1)
    causal = col <= row
    one_where = lambda cond: jnp.where(cond, 1.0, 0.0).astype(BF16)
    tri = one_where(causal)
    tri_t = one_where(row <= col)
    heads = [iota((1, ML_WIDTH), 1) // HEAD_DIM == h for h in range(ML_HEADS)]
    same_head = iota((ML_WIDTH, ML_WIDTH), 0) // HEAD_DIM == iota((ML_WIDTH, ML_WIDTH), 1) // HEAD_DIM
    head_mean = jnp.where(same_head, 1.0 / HEAD_DIM, 0.0).astype(BF16)
    sel_tile = one_where(iota((BLOCK, ML_HEADS * BLOCK), 0) == iota((BLOCK, ML_HEADS * BLOCK), 1) // BLOCK)
    sel_head = one_where(iota((BLOCK, ML_WIDTH), 0) == iota((BLOCK, ML_WIDTH), 1) // HEAD_DIM)
    rows_of_head = one_where(iota((ML_HEADS * BLOCK, ML_WIDTH), 0) // BLOCK
                             == iota((ML_HEADS * BLOCK, ML_WIDTH), 1) // HEAD_DIM)

    def spread(tile, selector):
        live = iota(tile.shape, 1) < ML_HEADS
        return sum(_dot(p, selector) for p in _split_bf16(jnp.where(live, tile, 0.0), 2))

    total = q_ref.shape[1]
    chunks = [slice(c * BLOCK, (c + 1) * BLOCK) for c in range(total // BLOCK)]
    row_in_chunk = iota((total, BLOCK), 0) % BLOCK
    causal_heads = iota((ML_HEADS * BLOCK, BLOCK), 1) <= iota((ML_HEADS * BLOCK, BLOCK), 0) % BLOCK
    qb = q_ref[0]
    kb = k_ref[0]
    v_all = v_ref[0]

    gates = gates_ref[0] + gbias_row_ref[...]
    gates_t = gates_t_ref[...] + gbias_col_ref[...]
    lsf = _split_bf16(_log_sigmoid(gates), 3)
    b_cols = jnp.concatenate([sum(_dot(tri, p[rows, :]) for p in lsf) for rows in chunks], axis=0)
    lsf_t = _log_sigmoid(gates_t)
    lsf_t = jnp.concatenate([lsf_t[:, rows] for rows in chunks], axis=0)
    b_rows = sum(_dot(p, tri_t) for p in _split_bf16(lsf_t, 3))
    b_tile = pltpu.roll(b_cols, BLOCK - ML_HEADS, axis=1)
    a_tile = gates - b_tile

    run_max = a_tile
    shift = 1
    while shift < BLOCK:
        run_max = jnp.where(row_in_chunk >= shift,
                            jnp.maximum(run_max, pltpu.roll(run_max, shift, axis=0)), run_max)
        shift *= 2

    m_prev = m_ref[0:1, :]
    m_prevs, m_news, decays, w_logs = [], [], [], []
    for c, rows in enumerate(chunks):
        b_end = b_tile[(c + 1) * BLOCK - 1:(c + 1) * BLOCK, :]
        w_log = b_end - b_tile[rows, :] + gates[rows, :]
        m_new = jnp.maximum(b_end + m_prev, jnp.max(w_log, axis=0, keepdims=True))
        decays.append(jnp.broadcast_to(jnp.exp(b_end + m_prev - m_new), (8, BLOCK)))
        m_prevs.append(jnp.broadcast_to(m_prev, (BLOCK, BLOCK)))
        m_news.append(jnp.broadcast_to(m_new, (BLOCK, BLOCK)))
        w_logs.append(w_log)
        m_prev = m_new
    m_ref[...] = jnp.broadcast_to(m_prev, m_ref.shape)
    m_before = jnp.concatenate(m_prevs, axis=0)
    u = jnp.maximum(m_before, run_max)
    neg_u = spread(-u, sel_tile)
    inter = spread(jnp.exp(m_before - u), sel_head)
    floor = spread(jnp.exp(-(b_tile + u)), sel_head)
    w = spread(jnp.exp(jnp.concatenate(w_logs, axis=0) - jnp.concatenate(m_news, axis=0)), sel_head)
    decay = spread(jnp.concatenate(decays, axis=0), sel_head)

    s_cat = []
    for c, rows in enumerate(chunks):
        q_c = qb[rows, :]
        q_stack = jnp.concatenate([jnp.where(head, q_c, jnp.zeros_like(q_c)) for head in heads], axis=0)
        a_rows = gates_t[0:ML_HEADS, rows] - b_rows[8 * c + ML_HEADS:8 * c + 2 * ML_HEADS, :]
        log_d = jnp.concatenate([a_rows[h:h + 1, :] + neg_u[rows, h * BLOCK:(h + 1) * BLOCK]
                                 for h in range(ML_HEADS)], axis=0)
        s = (_dot_nt(q_stack, kb[rows, :]) * jnp.exp(jnp.where(causal_heads, log_d, NEG_BIG))).astype(BF16)
        s_cat.append(jnp.concatenate([s[h * BLOCK:(h + 1) * BLOCK, :] for h in range(ML_HEADS)], axis=1))
    s_cat = jnp.concatenate(s_cat, axis=0)
    den_within = _dot(s_cat, rows_of_head)
    wk = (w * kb.astype(F32)).astype(BF16)
    v_ones = jnp.concatenate([v_all, jnp.ones_like(v_all)], axis=1)

    c_state = c_ref[...]
    n_state = n_ref[...]
    nums, dens = [], []
    for c, rows in enumerate(chunks):
        v_c = v_all[rows, :]
        v_heads = jnp.concatenate([jnp.where(head, v_c, jnp.zeros_like(v_c)) for head in heads], axis=0)
        carried = _dot(qb[rows, :], jnp.concatenate([c_state, n_state], axis=1).astype(BF16))
        nums.append(inter[rows, :] * carried[:, :ML_WIDTH] + _dot(s_cat[rows, :], v_heads))
        dens.append(inter[rows, :] * carried[:, ML_WIDTH:] + den_within[rows, :])
        update = _dot_tn(wk[rows, :], v_ones[rows, :])
        decay_c = decay[8 * c:8 * c + 1, :]
        c_state = decay_c * c_state + jnp.where(same_head, update[:, :ML_WIDTH], 0.0)
        n_state = decay_c * n_state + jnp.where(same_head, update[:, ML_WIDTH:], 0.0)
    c_ref[...] = c_state
    n_ref[...] = n_state

    hid = jnp.concatenate(nums, axis=0) / jnp.maximum(jnp.abs(jnp.concatenate(dens, axis=0)), floor)
    mu = sum(_dot(p, head_mean) for p in _split_bf16(hid, 2))
    dev = hid - mu
    var = sum(_dot(p, head_mean) for p in _split_bf16(dev * dev, 2))
    y = _sigmoid(og_ref[0]) * (dev * lax.rsqrt(var + LN_EPS) * normg_ref[...])
    y_ref[0] = y.astype(y_ref.dtype)


def _mlstm_call(q, k, v, og, gates, gates_t, gbias_row, gbias_col, norm_g):
    bsz, s_len, _ = q.shape
    nb = SEQ_BLOCKS_PER_STEP if (s_len // BLOCK) % SEQ_BLOCKS_PER_STEP == 0 else 1
    tok = lambda w: pl.BlockSpec((1, nb * BLOCK, w), lambda b, c: (b, c, 0))
    return pl.pallas_call(
        _mlstm_kernel,
        grid=(bsz, s_len // (nb * BLOCK)),
        in_specs=[tok(ML_WIDTH), tok(ML_WIDTH), tok(ML_WIDTH), tok(ML_WIDTH), tok(GATE_PAD),
                  pl.BlockSpec((2 * ML_HEADS, nb * BLOCK), lambda b, c: (0, b * (s_len // (nb * BLOCK)) + c)),
                  _resident(gbias_row.shape), _resident(gbias_col.shape), _resident(norm_g.shape)],
        out_specs=tok(ML_WIDTH),
        out_shape=jax.ShapeDtypeStruct((bsz, s_len, ML_WIDTH), BF16),
        scratch_shapes=[pltpu.VMEM((ML_WIDTH, ML_WIDTH), F32),
                        pltpu.VMEM((ML_WIDTH, ML_WIDTH), F32),
                        pltpu.VMEM((8, 128), F32)],
        compiler_params=_params("parallel", "arbitrary"),
        name="mlstm",
    )(q, k, v, og, gates, gates_t, gbias_row, gbias_col, norm_g)


def _kv_kernel(mem_ref, w_ref, o_ref):
    o_ref[...] = _dot(mem_ref[...].astype(BF16), w_ref[...].astype(BF16)).astype(o_ref.dtype)


def _kv_call(mem, w, layer):
    n = mem.shape[0]
    tm = min(256, n)
    return pl.pallas_call(
        _kv_kernel,
        grid=(n // tm,),
        in_specs=[pl.BlockSpec((tm, D_MODEL), lambda i: (i, 0)), _layer_slab(w.shape, layer)],
        out_specs=pl.BlockSpec((tm, 2 * D_MODEL), lambda i: (i, 0)),
        out_shape=jax.ShapeDtypeStruct((n, 2 * D_MODEL), BF16),
        compiler_params=_params("parallel"),
        name="xattn_kv_proj",
    )(mem, w)


def _mix_out_xattn_kernel(x_ref, ysb_ref, ysw_ref, yml_ref, wmix_ref, g1_ref, b1_ref,
                          kv_ref, wq_ref, wo_ref, g_ref, b_ref, o_ref):
    mix = (_dot(ysb_ref[0], wmix_ref[0:SB_WIDTH, :].astype(BF16))
           + _dot(ysw_ref[0], wmix_ref[SB_WIDTH:SB_WIDTH + SWA_WIDTH, :].astype(BF16))
           + _dot(yml_ref[0], wmix_ref[SB_WIDTH + SWA_WIDTH:, :].astype(BF16)))
    x = _layer_norm(ALPHA * x_ref[0] + mix, g1_ref[...], b1_ref[...])
    q = _dot(x.astype(BF16), wq_ref[...].astype(BF16)).astype(BF16)
    outs = []
    for h in range(XATTN_HEADS):
        lo = h * XATTN_HEAD_DIM
        kh = kv_ref[0, :, lo:lo + XATTN_HEAD_DIM]
        vh = kv_ref[0, :, D_MODEL + lo:D_MODEL + lo + XATTN_HEAD_DIM]
        s = _dot_nt(q[:, lo:lo + XATTN_HEAD_DIM], kh) * (XATTN_HEAD_DIM ** -0.5)
        p = jnp.exp(s - jnp.max(s, axis=-1, keepdims=True))
        denom = jnp.sum(p, axis=-1, keepdims=True)
        outs.append((_dot(p.astype(BF16), vh) / denom).astype(BF16))
    attn = jnp.concatenate(outs, axis=-1)
    y = ALPHA * x + _dot(attn, wo_ref[...].astype(BF16))
    o_ref[0] = _layer_norm(y, g_ref[...], b_ref[...])


def _mix_out_xattn_call(x, y_sb, y_sw, y_ml, w_mix, kv, w_q, w_o, layer, ln_g, ln_b):
    bsz, s_len, _ = x.shape
    tm = min(XATTN_TILE, s_len)
    tok = lambda w: pl.BlockSpec((1, tm, w), lambda bi, i: (bi, i, 0))
    ln = lambda t, k: _layer_slab(t.shape, 4 * layer + k)
    return pl.pallas_call(
        _mix_out_xattn_kernel,
        grid=(bsz, s_len // tm),
        in_specs=[tok(D_MODEL), tok(SB_WIDTH), tok(SWA_WIDTH), tok(ML_WIDTH),
                  _layer_slab(w_mix.shape, layer), ln(ln_g, 1), ln(ln_b, 1),
                  pl.BlockSpec((1, kv.shape[1], 2 * D_MODEL), lambda bi, i: (bi, 0, 0)),
                  _layer_slab(w_q.shape, layer), _layer_slab(w_o.shape, layer), ln(ln_g, 2), ln(ln_b, 2)],
        out_specs=tok(D_MODEL),
        out_shape=jax.ShapeDtypeStruct((bsz, s_len, D_MODEL), F32),
        compiler_params=_params("parallel", "parallel"),
        name="mix_out_xattn_ln",
    )(x, y_sb, y_sw, y_ml, w_mix, ln_g, ln_b, kv, w_q, w_o, ln_g, ln_b)


def _mixer_heads(x, w_in, conv_w, conv_b, i_bias, f_bias, norm_g, sinks, bias):
    bsz, s_len, _ = x.shape
    n = bsz * s_len
    x2 = x.reshape(n, D_MODEL)
    w_perm = jnp.concatenate(
        [w_in[:, :_ML_I], w_in[:, _ML_O:], w_in[:, _ML_I:_ML_O],
         jnp.zeros((D_MODEL, GATE_PAD - 2 * ML_HEADS), w_in.dtype)], axis=1).astype(BF16)
    sb_q, sb_k, sb_v, sw_q, sw_kv, ml_v, ml_o, gates, ml_q, ml_k, gates_t = _inproj_call(
        x2, w_perm, conv_w, conv_b[None, :], s_len)
    seq = lambda t: t.reshape(bsz, s_len, t.shape[-1])

    y_sb = _sb_call(seq(sb_q), seq(sb_k), seq(sb_v))
    y_sw = _swa_call(seq(sw_q), seq(sw_kv), bias, sinks)

    gate_bias = jnp.concatenate([i_bias, f_bias])
    gbias_row = jnp.pad(gate_bias, (0, GATE_PAD - 2 * ML_HEADS))[None, :]
    y_ml = _mlstm_call(seq(ml_q), seq(ml_k), seq(ml_v), seq(ml_o), seq(gates), gates_t,
                       gbias_row, gate_bias[:, None], norm_g[None, :])
    return y_sb, y_sw, y_ml


def kernel(x, mem, ffn1_w_in, ffn1_w_out, mix_w_in, ml_conv_w, ml_conv_b, ml_i_bias, ml_f_bias, ml_norm_g, swa_sinks, rel_bias, mix_w_out, xattn_w_q, xattn_w_kv, xattn_w_o, ffn2_w_in, ffn2_w_out, ln_g, ln_b):
    bsz, s_len, _ = x.shape
    n = bsz * s_len
    mem2 = mem.reshape(-1, D_MODEL)
    bias = _swa_bias(rel_bias)
    ln_g = ln_g.reshape(DEPTH * 4, 1, D_MODEL)
    ln_b = ln_b.reshape(DEPTH * 4, 1, D_MODEL)
    for l in range(DEPTH):
        x = _ffn_call(x.reshape(n, D_MODEL), ffn1_w_in, ffn1_w_out, l, ln_g, ln_b, 4 * l
                      ).reshape(bsz, s_len, D_MODEL)
        y_sb, y_sw, y_ml = _mixer_heads(x, mix_w_in[l], ml_conv_w[l], ml_conv_b[l], ml_i_bias[l], ml_f_bias[l],
                                        ml_norm_g[l], swa_sinks[l], bias)
        kv = _kv_call(mem2, xattn_w_kv, l).reshape(bsz, -1, 2 * D_MODEL)
        x = _mix_out_xattn_call(x, y_sb, y_sw, y_ml, mix_w_out, kv, xattn_w_q, xattn_w_o, l, ln_g, ln_b)
        x = _ffn_call(x.reshape(n, D_MODEL), ffn2_w_in, ffn2_w_out, l, ln_g, ln_b, 4 * l + 3
                      ).reshape(bsz, s_len, D_MODEL)
    return x
```

```python
import functools

import numpy as np
import jax
import jax.numpy as jnp
from jax import lax
from jax.experimental import pallas as pl
from jax.experimental.pallas import tpu as pltpu

F32 = jnp.float32
BF16 = jnp.bfloat16

D_MODEL = 1024
DEPTH = 2
HEAD_DIM = 64
SB_HEADS = 4
SWA_HEADS = 8
SWA_KV_HEADS = 2
SWA_GROUP = SWA_HEADS // SWA_KV_HEADS
ML_HEADS = 4
SB_WIDTH = SB_HEADS * HEAD_DIM
SWA_WIDTH = SWA_HEADS * HEAD_DIM
SWA_KV_WIDTH = SWA_KV_HEADS * HEAD_DIM
ML_WIDTH = ML_HEADS * HEAD_DIM
BLOCK = 128
CONV_WIDTH = 4
NUM_BUCKETS = 32
MAX_DISTANCE = 128
XATTN_HEADS = 4
XATTN_HEAD_DIM = D_MODEL // XATTN_HEADS
D_FF = 2816
ALPHA = (2 * DEPTH) ** 0.25
LN_EPS = 1e-5
NEG_BIG = -1e30
SB_SKIP_BELOW = -104.0

_ML_I, _ML_O = 2304, 2312
GATE_PAD = 128

VMEM_LIMIT = 56 * 1024 * 1024

SEQ_BLOCKS_PER_STEP = 16
INPROJ_TILE = 1024
XATTN_TILE = 1024
SB_ROWS_PER_STEP = 4
TOKEN_TILE = 512
FF_CHUNK = 256


def _params(*sem):
    return pltpu.CompilerParams(dimension_semantics=sem, vmem_limit_bytes=VMEM_LIMIT)


def _resident(shape):
    nd = len(shape)
    return pl.BlockSpec(shape, lambda *_: (0,) * nd, pipeline_mode=pl.Buffered(1))


def _layer_slab(stacked_shape, index):
    rest = tuple(stacked_shape[1:])
    return pl.BlockSpec((None,) + rest, lambda *_: (index,) + (0,) * len(rest), pipeline_mode=pl.Buffered(1))


def _layer_norm(y, g, b):
    mu = jnp.mean(y, axis=-1, keepdims=True)
    d = y - mu
    var = jnp.mean(d * d, axis=-1, keepdims=True)
    return d * lax.rsqrt(var + LN_EPS) * g + b


def _log_sigmoid(x):
    return jnp.minimum(x, 0.0) - jnp.log(1.0 + jnp.exp(-jnp.abs(x)))


def _sigmoid(x):
    return 1.0 / (1.0 + jnp.exp(-x))


def _dot(a, b):
    return jnp.dot(a, b, preferred_element_type=F32)


def _dot_nt(a, b):
    return lax.dot_general(a, b, (((1,), (1,)), ((), ())), preferred_element_type=F32)


def _dot_tn(a, b):
    return lax.dot_general(a, b, (((0,), (0,)), ((), ())), preferred_element_type=F32)


def _split_bf16(x, pieces):
    out = []
    r = x
    for _ in range(pieces - 1):
        p = r.astype(BF16)
        out.append(p)
        r = r - p.astype(F32)
    out.append(r.astype(BF16))
    return out


def _ffn_kernel(x_ref, win_ref, wout_ref, g_ref, b_ref, o_ref, acc_ref):
    xb = x_ref[...].astype(BF16)
    for c in range(D_FF // FF_CHUNK):
        lo = c * FF_CHUNK
        a = _dot(xb, win_ref[:, lo:lo + FF_CHUNK].astype(BF16))
        b = _dot(xb, win_ref[:, D_FF + lo:D_FF + lo + FF_CHUNK].astype(BF16))
        h = (a * _sigmoid(a) * b).astype(BF16)
        part = _dot(h, wout_ref[lo:lo + FF_CHUNK, :].astype(BF16))
        if c == 0:
            acc_ref[...] = part
        else:
            acc_ref[...] += part
    y = ALPHA * x_ref[...] + 0.5 * acc_ref[...]
    o_ref[...] = _layer_norm(y, g_ref[...], b_ref[...])


def _ffn_call(x, w_in, w_out, layer, ln_g, ln_b, ln_index):
    n = x.shape[0]
    tm = min(TOKEN_TILE, n)
    row = lambda i: (i, 0)
    return pl.pallas_call(
        _ffn_kernel,
        grid=(n // tm,),
        in_specs=[pl.BlockSpec((tm, D_MODEL), row), _layer_slab(w_in.shape, layer), _layer_slab(w_out.shape, layer),
                  _layer_slab(ln_g.shape, ln_index), _layer_slab(ln_b.shape, ln_index)],
        out_specs=pl.BlockSpec((tm, D_MODEL), row),
        out_shape=jax.ShapeDtypeStruct((n, D_MODEL), F32),
        scratch_shapes=[pltpu.VMEM((tm, D_MODEL), F32)],
        compiler_params=_params("parallel"),
        name="ffn_ln",
    )(x, w_in, w_out, ln_g, ln_b)


_INPROJ_OUTS = (
    ("sb_q", 0, 256, BF16), ("sb_k", 256, 256, BF16), ("sb_v", 512, 256, BF16),
    ("sw_q", 768, 512, BF16), ("sw_kv", 1280, 256, BF16),
    ("ml_v", 2048, 256, BF16), ("ml_o", 2304, 256, F32), ("gates", 2560, GATE_PAD, F32),
)
_ML_QK_PERMUTED = 1536


def _inproj_kernel(tiles_per_seq, x_ref, w_ref, convw_ref, convb_ref, *refs):
    out_refs, (mlq_ref, mlk_ref, gates_t_ref, tail_ref) = refs[:len(_INPROJ_OUTS)], refs[len(_INPROJ_OUTS):]
    tm = x_ref.shape[0]
    xb = x_ref[...].astype(BF16)

    @pl.when(pl.program_id(0) % tiles_per_seq == 0)
    def _():
        tail_ref[...] = jnp.zeros_like(tail_ref)

    qk = _dot(xb, w_ref[:, _ML_QK_PERMUTED:_ML_QK_PERMUTED + 2 * ML_WIDTH])
    tail = tail_ref[...]
    tail_ref[...] = qk[tm - 8:, :]
    first_rows = lax.broadcasted_iota(jnp.int32, tail.shape, 0)
    conv = convb_ref[...] + convw_ref[CONV_WIDTH - 1:CONV_WIDTH, :] * qk
    for back in range(1, CONV_WIDTH):
        rolled = pltpu.roll(qk, back, axis=0)
        head = jnp.where(first_rows < back, pltpu.roll(tail, back, axis=0), rolled[0:8, :])
        shifted = jnp.concatenate([head, rolled[8:, :]], axis=0)
        conv = conv + convw_ref[CONV_WIDTH - 1 - back:CONV_WIDTH - back, :] * shifted
    act = conv * _sigmoid(conv)
    mlq_ref[...] = act[:, :ML_WIDTH].astype(mlq_ref.dtype)
    mlk_ref[...] = (act[:, ML_WIDTH:] * (HEAD_DIM ** -0.5)).astype(mlk_ref.dtype)

    for (name, lo, width, dt), o_ref in zip(_INPROJ_OUTS, out_refs):
        res = _dot(xb, w_ref[:, lo:lo + width])
        o_ref[...] = res.astype(dt)
        if name == "gates":
            gates_t_ref[...] = res.T[0:gates_t_ref.shape[0], :]


def _inproj_call(x, w, conv_w, conv_b, s_len):
    n = x.shape[0]
    tm = min(INPROJ_TILE, s_len)
    assert s_len % tm == 0
    row = lambda i: (i, 0)
    widths = [width for _, _, width, _ in _INPROJ_OUTS] + [ML_WIDTH, ML_WIDTH]
    dtypes = [dt for _, _, _, dt in _INPROJ_OUTS] + [BF16, BF16]
    return pl.pallas_call(
        functools.partial(_inproj_kernel, s_len // tm),
        grid=(n // tm,),
        in_specs=[pl.BlockSpec((tm, D_MODEL), row), _resident(w.shape), _resident(conv_w.shape),
                  _resident(conv_b.shape)],
        out_specs=[pl.BlockSpec((tm, width), row) for width in widths]
                  + [pl.BlockSpec((2 * ML_HEADS, tm), lambda i: (0, i))],
        out_shape=[jax.ShapeDtypeStruct((n, width), dt) for width, dt in zip(widths, dtypes)]
                  + [jax.ShapeDtypeStruct((2 * ML_HEADS, n), F32)],
        scratch_shapes=[pltpu.VMEM((8, 2 * ML_WIDTH), F32)],
        compiler_params=_params("arbitrary"),
        name="mix_in_proj",
    )(x, w, conv_w, conv_b)


def _sb_kernel(q_ref, k_ref, v_ref, o_ref, acc_ref, carry_ref):
    qi = pl.program_id(1)
    rows = q_ref.shape[0]
    lane_head = lax.broadcasted_iota(jnp.int32, (1, SB_WIDTH), 1) // HEAD_DIM
    heads = [lane_head == h for h in range(SB_HEADS)]
    q_stacks = []
    for bi in range(rows):
        q = (q_ref[bi].astype(F32) * (HEAD_DIM ** -0.5)).astype(BF16)
        q_stacks.append(jnp.concatenate([jnp.where(head, q, jnp.zeros_like(q)) for head in heads], axis=0))
    tall = rows * SB_HEADS * BLOCK
    strict = (lax.broadcasted_iota(jnp.int32, (tall, BLOCK), 1)
              < lax.broadcasted_iota(jnp.int32, (tall, BLOCK), 0) % BLOCK)
    row = lax.broadcasted_iota(jnp.int32, (2 * BLOCK, 2 * BLOCK), 0) % BLOCK
    col = lax.broadcasted_iota(jnp.int32, (2 * BLOCK, 2 * BLOCK), 1)
    suffix_and_total = jnp.where((row >= col) | (col >= BLOCK), 1.0, 0.0).astype(BF16)

    def step(j, diagonal):
        start = pl.multiple_of(j * BLOCK, BLOCK)
        z = jnp.concatenate([_dot_nt(q_stacks[bi], k_ref[bi, pl.ds(start, BLOCK), :]) for bi in range(rows)],
                            axis=0)
        lom = jnp.minimum(-z, 0.0) - jnp.log(1.0 + jnp.exp(-jnp.abs(z)))
        if diagonal:
            lom = jnp.where(strict, lom, 0.0)
        sums = _dot(jnp.concatenate(_split_bf16(lom, 2), axis=1), suffix_and_total)
        if diagonal:
            w = jnp.where(strict, jnp.exp(z + sums[:, :BLOCK]), 0.0)
            carry = sums[:, BLOCK:]
        else:
            carry = carry_ref[...]
            w = jnp.exp(z + sums[:, :BLOCK] + carry)
            carry = carry + sums[:, BLOCK:]
        carry_ref[...] = carry
        w = w.astype(BF16)
        for bi in range(rows):
            vb = v_ref[bi, pl.ds(start, BLOCK), :]
            v_heads = jnp.concatenate([jnp.where(head, vb, jnp.zeros_like(vb)) for head in heads], axis=0)
            tiles = [w[(bi * SB_HEADS + h) * BLOCK:(bi * SB_HEADS + h + 1) * BLOCK, :] for h in range(SB_HEADS)]
            pv = _dot(jnp.concatenate(tiles, axis=1), v_heads)
            if diagonal:
                acc_ref[bi] = pv
            else:
                acc_ref[bi] += pv
        return jnp.max(carry)

    def more_blocks(state):
        j, largest_carry = state
        return (j >= 0) & (largest_carry > SB_SKIP_BELOW)

    def next_block(state):
        j, _ = state
        return j - 1, step(j, False)

    lax.while_loop(more_blocks, next_block, (qi - 1, step(qi, True)))
    o_ref[...] = acc_ref[...].astype(o_ref.dtype)


def _sb_call(q, k, v):
    bsz, s_len, _ = q.shape
    rows = SB_ROWS_PER_STEP if bsz % SB_ROWS_PER_STEP == 0 else 1
    blk = pl.BlockSpec((rows, BLOCK, SB_WIDTH), lambda b, i: (b, i, 0))
    whole = pl.BlockSpec((rows, s_len, SB_WIDTH), lambda b, i: (b, 0, 0), pipeline_mode=pl.Buffered(1))
    return pl.pallas_call(
        _sb_kernel,
        grid=(bsz // rows, s_len // BLOCK),
        in_specs=[blk, whole, whole],
        out_specs=blk,
        out_shape=jax.ShapeDtypeStruct((bsz, s_len, SB_WIDTH), BF16),
        scratch_shapes=[pltpu.VMEM((rows, BLOCK, SB_WIDTH), F32),
                        pltpu.VMEM((rows * SB_HEADS * BLOCK, BLOCK), F32)],
        compiler_params=_params("parallel", "arbitrary"),
        name="stick_breaking",
    )(q, k, v)


def _swa_kernel(sinks_ref, q_ref, kvp_ref, kvc_ref, bias_ref, o_ref):
    first = jnp.minimum(pl.program_id(1), 1)
    q = (q_ref[0].astype(F32) * (HEAD_DIM ** -0.5)).astype(BF16)
    kv = jnp.concatenate([kvp_ref[0], kvc_ref[0]], axis=0)
    outs = []
    for h in range(SWA_HEADS):
        g = h // SWA_GROUP
        qh = q[:, h * HEAD_DIM:(h + 1) * HEAD_DIM]
        kh = kv[:, g * HEAD_DIM:(g + 1) * HEAD_DIM]
        vh = kv[:, SWA_KV_WIDTH + g * HEAD_DIM:SWA_KV_WIDTH + (g + 1) * HEAD_DIM]
        logits = _dot_nt(qh, kh) + bias_ref[first, h]
        sink = sinks_ref[h]
        m = jnp.maximum(jnp.max(logits, axis=-1, keepdims=True), sink)
        p = jnp.exp(logits - m)
        denom = jnp.sum(p, axis=-1, keepdims=True) + jnp.exp(sink - m)
        outs.append(_dot(p.astype(BF16), vh) / denom)
    o_ref[0] = jnp.concatenate(outs, axis=-1).astype(o_ref.dtype)


def _swa_call(q, kv, bias, sinks):
    bsz, s_len, _ = q.shape
    return pl.pallas_call(
        _swa_kernel,
        grid=(bsz, s_len // BLOCK),
        in_specs=[pl.BlockSpec(memory_space=pltpu.SMEM),
                  pl.BlockSpec((1, BLOCK, SWA_WIDTH), lambda b, i: (b, i, 0)),
                  pl.BlockSpec((1, BLOCK, 2 * SWA_KV_WIDTH), lambda b, i: (b, jnp.maximum(i - 1, 0), 0)),
                  pl.BlockSpec((1, BLOCK, 2 * SWA_KV_WIDTH), lambda b, i: (b, i, 0)),
                  _resident(bias.shape)],
        out_specs=pl.BlockSpec((1, BLOCK, SWA_WIDTH), lambda b, i: (b, i, 0)),
        out_shape=jax.ShapeDtypeStruct((bsz, s_len, SWA_WIDTH), BF16),
        compiler_params=_params("parallel", "arbitrary"),
        name="sliding_window",
    )(sinks, q, kv, kv, bias)


def _t5_bucket(dist):
    max_exact = NUM_BUCKETS // 2
    d = np.maximum(dist, 1)
    large = max_exact + (np.log(d / max_exact) / np.log(MAX_DISTANCE / max_exact)
                         * (NUM_BUCKETS - max_exact)).astype(np.int32)
    large = np.minimum(large, NUM_BUCKETS - 1)
    return np.where(dist < max_exact, dist, large).astype(np.int32)


def _swa_bias(rel_bias):
    qi = np.arange(BLOCK)[:, None]
    kj = np.arange(2 * BLOCK)[None, :]
    dist = qi + BLOCK - kj
    bucket = jnp.asarray(_t5_bucket(np.clip(dist, 0, None)))[None]
    table = rel_bias.astype(F32)
    bias = jnp.zeros((SWA_HEADS, BLOCK, 2 * BLOCK), F32)
    for b in range(NUM_BUCKETS):
        bias = jnp.where(bucket == b, table[b][:, None, None], bias)
    in_window = (dist >= 0) & (dist < BLOCK)
    first = jnp.asarray(in_window & (kj >= BLOCK))[None]
    rest = jnp.asarray(in_window)[None]
    return jnp.stack([jnp.where(first, bias, NEG_BIG), jnp.where(rest, bias, NEG_BIG)])


def _mlstm_kernel(q_ref, k_ref, v_ref, og_ref, gates_ref, gates_t_ref,
                  gbias_row_ref, gbias_col_ref, normg_ref, y_ref,
                  c_ref, n_ref, m_ref):
    @pl.when(pl.program_id(1) == 0)
    def _():
        c_ref[...] = jnp.zeros_like(c_ref)
        n_ref[...] = jnp.zeros_like(n_ref)
        m_ref[...] = jnp.zeros_like(m_ref)

    iota = lambda shape, axis: lax.broadcasted_iota(jnp.int32, shape, axis)
    row = iota((BLOCK, BLOCK), 0)
    col = iota((BLOCK, BLOCK), 1)
    causal = col <= row
    one_where = lambda cond: jnp.where(cond, 1.0, 0.0).astype(BF16)
    tri = one_where(causal)
    tri_t = one_where(row <= col)
    heads = [iota((1, ML_WIDTH), 1) // HEAD_DIM == h for h in range(ML_HEADS)]
    same_head = iota((ML_WIDTH, ML_WIDTH), 0) // HEAD_DIM == iota((ML_WIDTH, ML_WIDTH), 1) // HEAD_DIM
    head_mean = jnp.where(same_head, 1.0 / HEAD_DIM, 0.0).astype(BF16)
    sel_tile = one_where(iota((BLOCK, ML_HEADS * BLOCK), 0) == iota((BLOCK, ML_HEADS * BLOCK), 1) // BLOCK)
    sel_head = one_where(iota((BLOCK, ML_WIDTH), 0) == iota((BLOCK, ML_WIDTH), 1) // HEAD_DIM)
    rows_of_head = one_where(iota((ML_HEADS * BLOCK, ML_WIDTH), 0) // BLOCK
                             == iota((ML_HEADS * BLOCK, ML_WIDTH), 1) // HEAD_DIM)

    def spread(tile, selector):
        live = iota(tile.shape, 1) < ML_HEADS
        return sum(_dot(p, selector) for p in _split_bf16(jnp.where(live, tile, 0.0), 2))

    total = q_ref.shape[1]
    chunks = [slice(c * BLOCK, (c + 1) * BLOCK) for c in range(total // BLOCK)]
    row_in_chunk = iota((total, BLOCK), 0) % BLOCK
    causal_heads = iota((ML_HEADS * BLOCK, BLOCK), 1) <= iota((ML_HEADS * BLOCK, BLOCK), 0) % BLOCK
    qb = q_ref[0]
    kb = k_ref[0]
    v_all = v_ref[0]

    gates = gates_ref[0] + gbias_row_ref[...]
    gates_t = gates_t_ref[...] + gbias_col_ref[...]
    lsf = _split_bf16(_log_sigmoid(gates), 3)
    b_cols = jnp.concatenate([sum(_dot(tri, p[rows, :]) for p in lsf) for rows in chunks], axis=0)
    lsf_t = _log_sigmoid(gates_t)
    lsf_t = jnp.concatenate([lsf_t[:, rows] for rows in chunks], axis=0)
    b_rows = sum(_dot(p, tri_t) for p in _split_bf16(lsf_t, 3))
    b_tile = pltpu.roll(b_cols, BLOCK - ML_HEADS, axis=1)
    a_tile = gates - b_tile

    run_max = a_tile
    shift = 1
    while shift < BLOCK:
        run_max = jnp.where(row_in_chunk >= shift,
                            jnp.maximum(run_max, pltpu.roll(run_max, shift, axis=0)), run_max)
        shift *= 2

    m_prev = m_ref[0:1, :]
    m_prevs, m_news, decays, w_logs = [], [], [], []
    for c, rows in enumerate(chunks):
        b_end = b_tile[(c + 1) * BLOCK - 1:(c + 1) * BLOCK, :]
        w_log = b_end - b_tile[rows, :] + gates[rows, :]
        m_new = jnp.maximum(b_end + m_prev, jnp.max(w_log, axis=0, keepdims=True))
        decays.append(jnp.broadcast_to(jnp.exp(b_end + m_prev - m_new), (8, BLOCK)))
        m_prevs.append(jnp.broadcast_to(m_prev, (BLOCK, BLOCK)))
        m_news.append(jnp.broadcast_to(m_new, (BLOCK, BLOCK)))
        w_logs.append(w_log)
        m_prev = m_new
    m_ref[...] = jnp.broadcast_to(m_prev, m_ref.shape)
    m_before = jnp.concatenate(m_prevs, axis=0)
    u = jnp.maximum(m_before, run_max)
    neg_u = spread(-u, sel_tile)
    inter = spread(jnp.exp(m_before - u), sel_head)
    floor = spread(jnp.exp(-(b_tile + u)), sel_head)
    w = spread(jnp.exp(jnp.concatenate(w_logs, axis=0) - jnp.concatenate(m_news, axis=0)), sel_head)
    decay = spread(jnp.concatenate(decays, axis=0), sel_head)

    s_cat = []
    for c, rows in enumerate(chunks):
        q_c = qb[rows, :]
        q_stack = jnp.concatenate([jnp.where(head, q_c, jnp.zeros_like(q_c)) for head in heads], axis=0)
        a_rows = gates_t[0:ML_HEADS, rows] - b_rows[8 * c + ML_HEADS:8 * c + 2 * ML_HEADS, :]
        log_d = jnp.concatenate([a_rows[h:h + 1, :] + neg_u[rows, h * BLOCK:(h + 1) * BLOCK]
                                 for h in range(ML_HEADS)], axis=0)
        s = (_dot_nt(q_stack, kb[rows, :]) * jnp.exp(jnp.where(causal_heads, log_d, NEG_BIG))).astype(BF16)
        s_cat.append(jnp.concatenate([s[h * BLOCK:(h + 1) * BLOCK, :] for h in range(ML_HEADS)], axis=1))
    s_cat = jnp.concatenate(s_cat, axis=0)
    den_within = _dot(s_cat, rows_of_head)
    wk = (w * kb.astype(F32)).astype(BF16)
    v_ones = jnp.concatenate([v_all, jnp.ones_like(v_all)], axis=1)

    c_state = c_ref[...]
    n_state = n_ref[...]
    nums, dens = [], []
    for c, rows in enumerate(chunks):
        v_c = v_all[rows, :]
        v_heads = jnp.concatenate([jnp.where(head, v_c, jnp.zeros_like(v_c)) for head in heads], axis=0)
        carried = _dot(qb[rows, :], jnp.concatenate([c_state, n_state], axis=1).astype(BF16))
        nums.append(inter[rows, :] * carried[:, :ML_WIDTH] + _dot(s_cat[rows, :], v_heads))
        dens.append(inter[rows, :] * carried[:, ML_WIDTH:] + den_within[rows, :])
        update = _dot_tn(wk[rows, :], v_ones[rows, :])
        decay_c = decay[8 * c:8 * c + 1, :]
        c_state = decay_c * c_state + jnp.where(same_head, update[:, :ML_WIDTH], 0.0)
        n_state = decay_c * n_state + jnp.where(same_head, update[:, ML_WIDTH:], 0.0)
    c_ref[...] = c_state
    n_ref[...] = n_state

    hid = jnp.concatenate(nums, axis=0) / jnp.maximum(jnp.abs(jnp.concatenate(dens, axis=0)), floor)
    mu = sum(_dot(p, head_mean) for p in _split_bf16(hid, 2))
    dev = hid - mu
    var = sum(_dot(p, head_mean) for p in _split_bf16(dev * dev, 2))
    y = _sigmoid(og_ref[0]) * (dev * lax.rsqrt(var + LN_EPS) * normg_ref[...])
    y_ref[0] = y.astype(y_ref.dtype)


def _mlstm_call(q, k, v, og, gates, gates_t, gbias_row, gbias_col, norm_g):
    bsz, s_len, _ = q.shape
    nb = SEQ_BLOCKS_PER_STEP if (s_len // BLOCK) % SEQ_BLOCKS_PER_STEP == 0 else 1
    tok = lambda w: pl.BlockSpec((1, nb * BLOCK, w), lambda b, c: (b, c, 0))
    return pl.pallas_call(
        _mlstm_kernel,
        grid=(bsz, s_len // (nb * BLOCK)),
        in_specs=[tok(ML_WIDTH), tok(ML_WIDTH), tok(ML_WIDTH), tok(ML_WIDTH), tok(GATE_PAD),
                  pl.BlockSpec((2 * ML_HEADS, nb * BLOCK), lambda b, c: (0, b * (s_len // (nb * BLOCK)) + c)),
                  _resident(gbias_row.shape), _resident(gbias_col.shape), _resident(norm_g.shape)],
        out_specs=tok(ML_WIDTH),
        out_shape=jax.ShapeDtypeStruct((bsz, s_len, ML_WIDTH), BF16),
        scratch_shapes=[pltpu.VMEM((ML_WIDTH, ML_WIDTH), F32),
                        pltpu.VMEM((ML_WIDTH, ML_WIDTH), F32),
                        pltpu.VMEM((8, 128), F32)],
        compiler_params=_params("parallel", "arbitrary"),
        name="mlstm",
    )(q, k, v, og, gates, gates_t, gbias_row, gbias_col, norm_g)


def _kv_kernel(mem_ref, w_ref, o_ref):
    o_ref[...] = _dot(mem_ref[...].astype(BF16), w_ref[...].astype(BF16)).astype(o_ref.dtype)


def _kv_call(mem, w, layer):
    n = mem.shape[0]
    tm = min(256, n)
    return pl.pallas_call(
        _kv_kernel,
        grid=(n // tm,),
        in_specs=[pl.BlockSpec((tm, D_MODEL), lambda i: (i, 0)), _layer_slab(w.shape, layer)],
        out_specs=pl.BlockSpec((tm, 2 * D_MODEL), lambda i: (i, 0)),
        out_shape=jax.ShapeDtypeStruct((n, 2 * D_MODEL), BF16),
        compiler_params=_params("parallel"),
        name="xattn_kv_proj",
    )(mem, w)


def _mix_out_xattn_kernel(x_ref, ysb_ref, ysw_ref, yml_ref, wmix_ref, g1_ref, b1_ref,
                          kv_ref, wq_ref, wo_ref, g_ref, b_ref, o_ref):
    mix = (_dot(ysb_ref[0], wmix_ref[0:SB_WIDTH, :].astype(BF16))
           + _dot(ysw_ref[0], wmix_ref[SB_WIDTH:SB_WIDTH + SWA_WIDTH, :].astype(BF16))
           + _dot(yml_ref[0], wmix_ref[SB_WIDTH + SWA_WIDTH:, :].astype(BF16)))
    x = _layer_norm(ALPHA * x_ref[0] + mix, g1_ref[...], b1_ref[...])
    q = _dot(x.astype(BF16), wq_ref[...].astype(BF16)).astype(BF16)
    outs = []
    for h in range(XATTN_HEADS):
        lo = h * XATTN_HEAD_DIM
        kh = kv_ref[0, :, lo:lo + XATTN_HEAD_DIM]
        vh = kv_ref[0, :, D_MODEL + lo:D_MODEL + lo + XATTN_HEAD_DIM]
        s = _dot_nt(q[:, lo:lo + XATTN_HEAD_DIM], kh) * (XATTN_HEAD_DIM ** -0.5)
        p = jnp.exp(s - jnp.max(s, axis=-1, keepdims=True))
        denom = jnp.sum(p, axis=-1, keepdims=True)
        outs.append((_dot(p.astype(BF16), vh) / denom).astype(BF16))
    attn = jnp.concatenate(outs, axis=-1)
    y = ALPHA * x + _dot(attn, wo_ref[...].astype(BF16))
    o_ref[0] = _layer_norm(y, g_ref[...], b_ref[...])


def _mix_out_xattn_call(x, y_sb, y_sw, y_ml, w_mix, kv, w_q, w_o, layer, ln_g, ln_b):
    bsz, s_len, _ = x.shape
    tm = min(XATTN_TILE, s_len)
    tok = lambda w: pl.BlockSpec((1, tm, w), lambda bi, i: (bi, i, 0))
    ln = lambda t, k: _layer_slab(t.shape, 4 * layer + k)
    return pl.pallas_call(
        _mix_out_xattn_kernel,
        grid=(bsz, s_len // tm),
        in_specs=[tok(D_MODEL), tok(SB_WIDTH), tok(SWA_WIDTH), tok(ML_WIDTH),
                  _layer_slab(w_mix.shape, layer), ln(ln_g, 1), ln(ln_b, 1),
                  pl.BlockSpec((1, kv.shape[1], 2 * D_MODEL), lambda bi, i: (bi, 0, 0)),
                  _layer_slab(w_q.shape, layer), _layer_slab(w_o.shape, layer), ln(ln_g, 2), ln(ln_b, 2)],
        out_specs=tok(D_MODEL),
        out_shape=jax.ShapeDtypeStruct((bsz, s_len, D_MODEL), F32),
        compiler_params=_params("parallel", "parallel"),
        name="mix_out_xattn_ln",
    )(x, y_sb, y_sw, y_ml, w_mix, ln_g, ln_b, kv, w_q, w_o, ln_g, ln_b)


def _mixer_heads(x, w_in, conv_w, conv_b, i_bias, f_bias, norm_g, sinks, bias):
    bsz, s_len, _ = x.shape
    n = bsz * s_len
    x2 = x.reshape(n, D_MODEL)
    w_perm = jnp.concatenate(
        [w_in[:, :_ML_I], w_in[:, _ML_O:], w_in[:, _ML_I:_ML_O],
         jnp.zeros((D_MODEL, GATE_PAD - 2 * ML_HEADS), w_in.dtype)], axis=1).astype(BF16)
    sb_q, sb_k, sb_v, sw_q, sw_kv, ml_v, ml_o, gates, ml_q, ml_k, gates_t = _inproj_call(
        x2, w_perm, conv_w, conv_b[None, :], s_len)
    seq = lambda t: t.reshape(bsz, s_len, t.shape[-1])

    y_sb = _sb_call(seq(sb_q), seq(sb_k), seq(sb_v))
    y_sw = _swa_call(seq(sw_q), seq(sw_kv), bias, sinks)

    gate_bias = jnp.concatenate([i_bias, f_bias])
    gbias_row = jnp.pad(gate_bias, (0, GATE_PAD - 2 * ML_HEADS))[None, :]
    y_ml = _mlstm_call(seq(ml_q), seq(ml_k), seq(ml_v), seq(ml_o), seq(gates), gates_t,
                       gbias_row, gate_bias[:, None], norm_g[None, :])
    return y_sb, y_sw, y_ml


def kernel(x, mem, ffn1_w_in, ffn1_w_out, mix_w_in, ml_conv_w, ml_conv_b, ml_i_bias, ml_f_bias, ml_norm_g, swa_sinks, rel_bias, mix_w_out, xattn_w_q, xattn_w_kv, xattn_w_o, ffn2_w_in, ffn2_w_out, ln_g, ln_b):
    bsz, s_len, _ = x.shape
    n = bsz * s_len
    mem2 = mem.reshape(-1, D_MODEL)
    bias = _swa_bias(rel_bias)
    ln_g = ln_g.reshape(DEPTH * 4, 1, D_MODEL)
    ln_b = ln_b.reshape(DEPTH * 4, 1, D_MODEL)
    for l in range(DEPTH):
        x = _ffn_call(x.reshape(n, D_MODEL), ffn1_w_in, ffn1_w_out, l, ln_g, ln_b, 4 * l
                      ).reshape(bsz, s_len, D_MODEL)
        y_sb, y_sw, y_ml = _mixer_heads(x, mix_w_in[l], ml_conv_w[l], ml_conv_b[l], ml_i_bias[l], ml_f_bias[l],
                                        ml_norm_g[l], swa_sinks[l], bias)
        kv = _kv_call(mem2, xattn_w_kv, l).reshape(bsz, -1, 2 * D_MODEL)
        x = _mix_out_xattn_call(x, y_sb, y_sw, y_ml, mix_w_out, kv, xattn_w_q, xattn_w_o, l, ln_g, ln_b)
        x = _ffn_call(x.reshape(n, D_MODEL), ffn2_w_in, ffn2_w_out, l, ln_g, ln_b, 4 * l + 3
                      ).reshape(bsz, s_len, D_MODEL)
    return x
```

```python
import functools

import numpy as np
import jax
import jax.numpy as jnp
from jax import lax
from jax.experimental import pallas as pl
from jax.experimental.pallas import tpu as pltpu

F32 = jnp.float32
BF16 = jnp.bfloat16

D_MODEL = 1024
DEPTH = 2
HEAD_DIM = 64
SB_HEADS = 4
SWA_HEADS = 8
SWA_KV_HEADS = 2
SWA_GROUP = SWA_HEADS // SWA_KV_HEADS
ML_HEADS = 4
SB_WIDTH = SB_HEADS * HEAD_DIM
SWA_WIDTH = SWA_HEADS * HEAD_DIM
SWA_KV_WIDTH = SWA_KV_HEADS * HEAD_DIM
ML_WIDTH = ML_HEADS * HEAD_DIM
BLOCK = 128
CONV_WIDTH = 4
NUM_BUCKETS = 32
MAX_DISTANCE = 128
XATTN_HEADS = 4
XATTN_HEAD_DIM = D_MODEL // XATTN_HEADS
D_FF = 2816
ALPHA = (2 * DEPTH) ** 0.25
LN_EPS = 1e-5
NEG_BIG = -1e30
SB_SKIP_BELOW = -104.0

_ML_I, _ML_O = 2304, 2312
GATE_PAD = 128

VMEM_LIMIT = 56 * 1024 * 1024

SEQ_BLOCKS_PER_STEP = 16
INPROJ_TILE = 1024
XATTN_TILE = 1024
SB_ROWS_PER_STEP = 4
TOKEN_TILE = 512
FF_CHUNK = 256


def _params(*sem):
    return pltpu.CompilerParams(dimension_semantics=sem, vmem_limit_bytes=VMEM_LIMIT)


def _resident(shape):
    nd = len(shape)
    return pl.BlockSpec(shape, lambda *_: (0,) * nd, pipeline_mode=pl.Buffered(1))


def _layer_slab(stacked_shape, index):
    rest = tuple(stacked_shape[1:])
    return pl.BlockSpec((None,) + rest, lambda *_: (index,) + (0,) * len(rest), pipeline_mode=pl.Buffered(1))


def _layer_norm(y, g, b):
    mu = jnp.mean(y, axis=-1, keepdims=True)
    d = y - mu
    var = jnp.mean(d * d, axis=-1, keepdims=True)
    return d * lax.rsqrt(var + LN_EPS) * g + b


def _log_sigmoid(x):
    return jnp.minimum(x, 0.0) - jnp.log(1.0 + jnp.exp(-jnp.abs(x)))


def _sigmoid(x):
    return 1.0 / (1.0 + jnp.exp(-x))


def _dot(a, b):
    return jnp.dot(a, b, preferred_element_type=F32)


def _dot_nt(a, b):
    return lax.dot_general(a, b, (((1,), (1,)), ((), ())), preferred_element_type=F32)


def _dot_tn(a, b):
    return lax.dot_general(a, b, (((0,), (0,)), ((), ())), preferred_element_type=F32)


def _split_bf16(x, pieces):
    out = []
    r = x
    for _ in range(pieces - 1):
        p = r.astype(BF16)
        out.append(p)
        r = r - p.astype(F32)
    out.append(r.astype(BF16))
    return out


def _ffn_kernel(x_ref, win_ref, wout_ref, g_ref, b_ref, o_ref, acc_ref):
    xb = x_ref[...].astype(BF16)
    for c in range(D_FF // FF_CHUNK):
        lo = c * FF_CHUNK
        a = _dot(xb, win_ref[:, lo:lo + FF_CHUNK].astype(BF16))
        b = _dot(xb, win_ref[:, D_FF + lo:D_FF + lo + FF_CHUNK].astype(BF16))
        h = (a * _sigmoid(a) * b).astype(BF16)
        part = _dot(h, wout_ref[lo:lo + FF_CHUNK, :].astype(BF16))
        if c == 0:
            acc_ref[...] = part
        else:
            acc_ref[...] += part
    y = ALPHA * x_ref[...] + 0.5 * acc_ref[...]
    o_ref[...] = _layer_norm(y, g_ref[...], b_ref[...])


def _ffn_call(x, w_in, w_out, layer, ln_g, ln_b, ln_index):
    n = x.shape[0]
    tm = min(TOKEN_TILE, n)
    row = lambda i: (i, 0)
    return pl.pallas_call(
        _ffn_kernel,
        grid=(n // tm,),
        in_specs=[pl.BlockSpec((tm, D_MODEL), row), _layer_slab(w_in.shape, layer), _layer_slab(w_out.shape, layer),
                  _layer_slab(ln_g.shape, ln_index), _layer_slab(ln_b.shape, ln_index)],
        out_specs=pl.BlockSpec((tm, D_MODEL), row),
        out_shape=jax.ShapeDtypeStruct((n, D_MODEL), F32),
        scratch_shapes=[pltpu.VMEM((tm, D_MODEL), F32)],
        compiler_params=_params("parallel"),
        name="ffn_ln",
    )(x, w_in, w_out, ln_g, ln_b)


_INPROJ_OUTS = (
    ("sb_q", 0, 256, BF16), ("sb_k", 256, 256, BF16), ("sb_v", 512, 256, BF16),
    ("sw_q", 768, 512, BF16), ("sw_kv", 1280, 256, BF16),
    ("ml_v", 2048, 256, BF16), ("ml_o", 2304, 256, F32), ("gates", 2560, GATE_PAD, F32),
)
_ML_QK_PERMUTED = 1536


def _inproj_kernel(tiles_per_seq, x_ref, w_ref, convw_ref, convb_ref, *refs):
    out_refs, (mlq_ref, mlk_ref, gates_t_ref, tail_ref) = refs[:len(_INPROJ_OUTS)], refs[len(_INPROJ_OUTS):]
    tm = x_ref.shape[0]
    xb = x_ref[...].astype(BF16)

    @pl.when(pl.program_id(0) % tiles_per_seq == 0)
    def _():
        tail_ref[...] = jnp.zeros_like(tail_ref)

    qk = _dot(xb, w_ref[:, _ML_QK_PERMUTED:_ML_QK_PERMUTED + 2 * ML_WIDTH])
    tail = tail_ref[...]
    tail_ref[...] = qk[tm - 8:, :]
    first_rows = lax.broadcasted_iota(jnp.int32, tail.shape, 0)
    conv = convb_ref[...] + convw_ref[CONV_WIDTH - 1:CONV_WIDTH, :] * qk
    for back in range(1, CONV_WIDTH):
        rolled = pltpu.roll(qk, back, axis=0)
        head = jnp.where(first_rows < back, pltpu.roll(tail, back, axis=0), rolled[0:8, :])
        shifted = jnp.concatenate([head, rolled[8:, :]], axis=0)
        conv = conv + convw_ref[CONV_WIDTH - 1 - back:CONV_WIDTH - back, :] * shifted
    act = conv * _sigmoid(conv)
    mlq_ref[...] = act[:, :ML_WIDTH].astype(mlq_ref.dtype)
    mlk_ref[...] = (act[:, ML_WIDTH:] * (HEAD_DIM ** -0.5)).astype(mlk_ref.dtype)

    for (name, lo, width, dt), o_ref in zip(_INPROJ_OUTS, out_refs):
        res = _dot(xb, w_ref[:, lo:lo + width])
        o_ref[...] = res.astype(dt)
        if name == "gates":
            gates_t_ref[...] = res.T[0:gates_t_ref.shape[0], :]


def _inproj_call(x, w, conv_w, conv_b, s_len):
    n = x.shape[0]
    tm = min(INPROJ_TILE, s_len)
    assert s_len % tm == 0
    row = lambda i: (i, 0)
    widths = [width for _, _, width, _ in _INPROJ_OUTS] + [ML_WIDTH, ML_WIDTH]
    dtypes = [dt for _, _, _, dt in _INPROJ_OUTS] + [BF16, BF16]
    return pl.pallas_call(
        functools.partial(_inproj_kernel, s_len // tm),
        grid=(n // tm,),
        in_specs=[pl.BlockSpec((tm, D_MODEL), row), _resident(w.shape), _resident(conv_w.shape),
                  _resident(conv_b.shape)],
        out_specs=[pl.BlockSpec((tm, width), row) for width in widths]
                  + [pl.BlockSpec((2 * ML_HEADS, tm), lambda i: (0, i))],
        out_shape=[jax.ShapeDtypeStruct((n, width), dt) for width, dt in zip(widths, dtypes)]
                  + [jax.ShapeDtypeStruct((2 * ML_HEADS, n), F32)],
        scratch_shapes=[pltpu.VMEM((8, 2 * ML_WIDTH), F32)],
        compiler_params=_params("arbitrary"),
        name="mix_in_proj",
    )(x, w, conv_w, conv_b)


def _sb_kernel(q_ref, k_ref, v_ref, o_ref, acc_ref, carry_ref):
    qi = pl.program_id(1)
    rows = q_ref.shape[0]
    lane_head = lax.broadcasted_iota(jnp.int32, (1, SB_WIDTH), 1) // HEAD_DIM
    heads = [lane_head == h for h in range(SB_HEADS)]
    q_stacks = []
    for bi in range(rows):
        q = (q_ref[bi].astype(F32) * (HEAD_DIM ** -0.5)).astype(BF16)
        q_stacks.append(jnp.concatenate([jnp.where(head, q, jnp.zeros_like(q)) for head in heads], axis=0))
    tall = rows * SB_HEADS * BLOCK
    strict = (lax.broadcasted_iota(jnp.int32, (tall, BLOCK), 1)
              < lax.broadcasted_iota(jnp.int32, (tall, BLOCK), 0) % BLOCK)
    row = lax.broadcasted_iota(jnp.int32, (2 * BLOCK, 2 * BLOCK), 0) % BLOCK
    col = lax.broadcasted_iota(jnp.int32, (2 * BLOCK, 2 * BLOCK), 1)
    suffix_and_total = jnp.where((row >= col) | (col >= BLOCK), 1.0, 0.0).astype(BF16)

    def step(j, diagonal):
        start = pl.multiple_of(j * BLOCK, BLOCK)
        z = jnp.concatenate([_dot_nt(q_stacks[bi], k_ref[bi, pl.ds(start, BLOCK), :]) for bi in range(rows)],
                            axis=0)
        lom = jnp.minimum(-z, 0.0) - jnp.log(1.0 + jnp.exp(-jnp.abs(z)))
        if diagonal:
            lom = jnp.where(strict, lom, 0.0)
        sums = _dot(jnp.concatenate(_split_bf16(lom, 2), axis=1), suffix_and_total)
        if diagonal:
            w = jnp.where(strict, jnp.exp(z + sums[:, :BLOCK]), 0.0)
            carry = sums[:, BLOCK:]
        else:
            carry = carry_ref[...]
            w = jnp.exp(z + sums[:, :BLOCK] + carry)
            carry = carry + sums[:, BLOCK:]
        carry_ref[...] = carry
        w = w.astype(BF16)
        for bi in range(rows):
            vb = v_ref[bi, pl.ds(start, BLOCK), :]
            v_heads = jnp.concatenate([jnp.where(head, vb, jnp.zeros_like(vb)) for head in heads], axis=0)
            tiles = [w[(bi * SB_HEADS + h) * BLOCK:(bi * SB_HEADS + h + 1) * BLOCK, :] for h in range(SB_HEADS)]
            pv = _dot(jnp.concatenate(tiles, axis=1), v_heads)
            if diagonal:
                acc_ref[bi] = pv
            else:
                acc_ref[bi] += pv
        return jnp.max(carry)

    def more_blocks(state):
        j, largest_carry = state
        return (j >= 0) & (largest_carry > SB_SKIP_BELOW)

    def next_block(state):
        j, _ = state
        return j - 1, step(j, False)

    lax.fori_loop(0, qi, lambda jj, _: step(qi - 1 - jj, False), step(qi, True))
    o_ref[...] = acc_ref[...].astype(o_ref.dtype)


def _sb_call(q, k, v):
    bsz, s_len, _ = q.shape
    rows = SB_ROWS_PER_STEP if bsz % SB_ROWS_PER_STEP == 0 else 1
    blk = pl.BlockSpec((rows, BLOCK, SB_WIDTH), lambda b, i: (b, i, 0))
    whole = pl.BlockSpec((rows, s_len, SB_WIDTH), lambda b, i: (b, 0, 0), pipeline_mode=pl.Buffered(1))
    return pl.pallas_call(
        _sb_kernel,
        grid=(bsz // rows, s_len // BLOCK),
        in_specs=[blk, whole, whole],
        out_specs=blk,
        out_shape=jax.ShapeDtypeStruct((bsz, s_len, SB_WIDTH), BF16),
        scratch_shapes=[pltpu.VMEM((rows, BLOCK, SB_WIDTH), F32),
                        pltpu.VMEM((rows * SB_HEADS * BLOCK, BLOCK), F32)],
        compiler_params=_params("parallel", "arbitrary"),
        name="stick_breaking",
    )(q, k, v)


def _swa_kernel(sinks_ref, q_ref, kvp_ref, kvc_ref, bias_ref, o_ref):
    first = jnp.minimum(pl.program_id(1), 1)
    q = (q_ref[0].astype(F32) * (HEAD_DIM ** -0.5)).astype(BF16)
    kv = jnp.concatenate([kvp_ref[0], kvc_ref[0]], axis=0)
    outs = []
    for h in range(SWA_HEADS):
        g = h // SWA_GROUP
        qh = q[:, h * HEAD_DIM:(h + 1) * HEAD_DIM]
        kh = kv[:, g * HEAD_DIM:(g + 1) * HEAD_DIM]
        vh = kv[:, SWA_KV_WIDTH + g * HEAD_DIM:SWA_KV_WIDTH + (g + 1) * HEAD_DIM]
        logits = _dot_nt(qh, kh) + bias_ref[first, h]
        sink = sinks_ref[h]
        m = jnp.maximum(jnp.max(logits, axis=-1, keepdims=True), sink)
        p = jnp.exp(logits - m)
        denom = jnp.sum(p, axis=-1, keepdims=True) + jnp.exp(sink - m)
        outs.append(_dot(p.astype(BF16), vh) / denom)
    o_ref[0] = jnp.concatenate(outs, axis=-1).astype(o_ref.dtype)


def _swa_call(q, kv, bias, sinks):
    bsz, s_len, _ = q.shape
    return pl.pallas_call(
        _swa_kernel,
        grid=(bsz, s_len // BLOCK),
        in_specs=[pl.BlockSpec(memory_space=pltpu.SMEM),
                  pl.BlockSpec((1, BLOCK, SWA_WIDTH), lambda b, i: (b, i, 0)),
                  pl.BlockSpec((1, BLOCK, 2 * SWA_KV_WIDTH), lambda b, i: (b, jnp.maximum(i - 1, 0), 0)),
                  pl.BlockSpec((1, BLOCK, 2 * SWA_KV_WIDTH), lambda b, i: (b, i, 0)),
                  _resident(bias.shape)],
        out_specs=pl.BlockSpec((1, BLOCK, SWA_WIDTH), lambda b, i: (b, i, 0)),
        out_shape=jax.ShapeDtypeStruct((bsz, s_len, SWA_WIDTH), BF16),
        compiler_params=_params("parallel", "arbitrary"),
        name="sliding_window",
    )(sinks, q, kv, kv, bias)


def _t5_bucket(dist):
    max_exact = NUM_BUCKETS // 2
    d = np.maximum(dist, 1)
    large = max_exact + (np.log(d / max_exact) / np.log(MAX_DISTANCE / max_exact)
                         * (NUM_BUCKETS - max_exact)).astype(np.int32)
    large = np.minimum(large, NUM_BUCKETS - 1)
    return np.where(dist < max_exact, dist, large).astype(np.int32)


def _swa_bias(rel_bias):
    qi = np.arange(BLOCK)[:, None]
    kj = np.arange(2 * BLOCK)[None, :]
    dist = qi + BLOCK - kj
    bucket = jnp.asarray(_t5_bucket(np.clip(dist, 0, None)))[None]
    table = rel_bias.astype(F32)
    bias = jnp.zeros((SWA_HEADS, BLOCK, 2 * BLOCK), F32)
    for b in range(NUM_BUCKETS):
        bias = jnp.where(bucket == b, table[b][:, None, None], bias)
    in_window = (dist >= 0) & (dist < BLOCK)
    first = jnp.asarray(in_window & (kj >= BLOCK))[None]
    rest = jnp.asarray(in_window)[None]
    return jnp.stack([jnp.where(first, bias, NEG_BIG), jnp.where(rest, bias, NEG_BIG)])


def _mlstm_kernel(q_ref, k_ref, v_ref, og_ref, gates_ref, gates_t_ref,
                  gbias_row_ref, gbias_col_ref, normg_ref, y_ref,
                  c_ref, n_ref, m_ref):
    @pl.when(pl.program_id(1) == 0)
    def _():
        c_ref[...] = jnp.zeros_like(c_ref)
        n_ref[...] = jnp.zeros_like(n_ref)
        m_ref[...] = jnp.zeros_like(m_ref)

    iota = lambda shape, axis: lax.broadcasted_iota(jnp.int32, shape, axis)
    row = iota((BLOCK, BLOCK), 0)
    col = iota((BLOCK, BLOCK), 1)
    causal = col <= row
    one_where = lambda cond: jnp.where(cond, 1.0, 0.0).astype(BF16)
    tri = one_where(causal)
    tri_t = one_where(row <= col)
    heads = [iota((1, ML_WIDTH), 1) // HEAD_DIM == h for h in range(ML_HEADS)]
    same_head = iota((ML_WIDTH, ML_WIDTH), 0) // HEAD_DIM == iota((ML_WIDTH, ML_WIDTH), 1) // HEAD_DIM
    head_mean = jnp.where(same_head, 1.0 / HEAD_DIM, 0.0).astype(BF16)
    sel_tile = one_where(iota((BLOCK, ML_HEADS * BLOCK), 0) == iota((BLOCK, ML_HEADS * BLOCK), 1) // BLOCK)
    sel_head = one_where(iota((BLOCK, ML_WIDTH), 0) == iota((BLOCK, ML_WIDTH), 1) // HEAD_DIM)
    rows_of_head = one_where(iota((ML_HEADS * BLOCK, ML_WIDTH), 0) // BLOCK
                             == iota((ML_HEADS * BLOCK, ML_WIDTH), 1) // HEAD_DIM)

    def spread(tile, selector):
        live = iota(tile.shape, 1) < ML_HEADS
        return sum(_dot(p, selector) for p in _split_bf16(jnp.where(live, tile, 0.0), 2))

    total = q_ref.shape[1]
    chunks = [slice(c * BLOCK, (c + 1) * BLOCK) for c in range(total // BLOCK)]
    row_in_chunk = iota((total, BLOCK), 0) % BLOCK
    causal_heads = iota((ML_HEADS * BLOCK, BLOCK), 1) <= iota((ML_HEADS * BLOCK, BLOCK), 0) % BLOCK
    qb = q_ref[0]
    kb = k_ref[0]
    v_all = v_ref[0]

    gates = gates_ref[0] + gbias_row_ref[...]
    gates_t = gates_t_ref[...] + gbias_col_ref[...]
    lsf = _split_bf16(_log_sigmoid(gates), 3)
    b_cols = jnp.concatenate([sum(_dot(tri, p[rows, :]) for p in lsf) for rows in chunks], axis=0)
    lsf_t = _log_sigmoid(gates_t)
    lsf_t = jnp.concatenate([lsf_t[:, rows] for rows in chunks], axis=0)
    b_rows = sum(_dot(p, tri_t) for p in _split_bf16(lsf_t, 3))
    b_tile = pltpu.roll(b_cols, BLOCK - ML_HEADS, axis=1)
    a_tile = gates - b_tile

    run_max = a_tile
    shift = 1
    while shift < BLOCK:
        run_max = jnp.where(row_in_chunk >= shift,
                            jnp.maximum(run_max, pltpu.roll(run_max, shift, axis=0)), run_max)
        shift *= 2

    m_prev = m_ref[0:1, :]
    m_prevs, m_news, decays, w_logs = [], [], [], []
    for c, rows in enumerate(chunks):
        b_end = b_tile[(c + 1) * BLOCK - 1:(c + 1) * BLOCK, :]
        w_log = b_end - b_tile[rows, :] + gates[rows, :]
        m_new = jnp.maximum(b_end + m_prev, jnp.max(w_log, axis=0, keepdims=True))
        decays.append(jnp.broadcast_to(jnp.exp(b_end + m_prev - m_new), (8, BLOCK)))
        m_prevs.append(jnp.broadcast_to(m_prev, (BLOCK, BLOCK)))
        m_news.append(jnp.broadcast_to(m_new, (BLOCK, BLOCK)))
        w_logs.append(w_log)
        m_prev = m_new
    m_ref[...] = jnp.broadcast_to(m_prev, m_ref.shape)
    m_before = jnp.concatenate(m_prevs, axis=0)
    u = jnp.maximum(m_before, run_max)
    neg_u = spread(-u, sel_tile)
    inter = spread(jnp.exp(m_before - u), sel_head)
    floor = spread(jnp.exp(-(b_tile + u)), sel_head)
    w = spread(jnp.exp(jnp.concatenate(w_logs, axis=0) - jnp.concatenate(m_news, axis=0)), sel_head)
    decay = spread(jnp.concatenate(decays, axis=0), sel_head)

    s_cat = []
    for c, rows in enumerate(chunks):
        q_c = qb[rows, :]
        q_stack = jnp.concatenate([jnp.where(head, q_c, jnp.zeros_like(q_c)) for head in heads], axis=0)
        a_rows = gates_t[0:ML_HEADS, rows] - b_rows[8 * c + ML_HEADS:8 * c + 2 * ML_HEADS, :]
        log_d = jnp.concatenate([a_rows[h:h + 1, :] + neg_u[rows, h * BLOCK:(h + 1) * BLOCK]
                                 for h in range(ML_HEADS)], axis=0)
        s = (_dot_nt(q_stack, kb[rows, :]) * jnp.exp(jnp.where(causal_heads, log_d, NEG_BIG))).astype(BF16)
        s_cat.append(jnp.concatenate([s[h * BLOCK:(h + 1) * BLOCK, :] for h in range(ML_HEADS)], axis=1))
    s_cat = jnp.concatenate(s_cat, axis=0)
    den_within = _dot(s_cat, rows_of_head)
    wk = (w * kb.astype(F32)).astype(BF16)
    v_ones = jnp.concatenate([v_all, jnp.ones_like(v_all)], axis=1)

    c_state = c_ref[...]
    n_state = n_ref[...]
    nums, dens = [], []
    for c, rows in enumerate(chunks):
        v_c = v_all[rows, :]
        v_heads = jnp.concatenate([jnp.where(head, v_c, jnp.zeros_like(v_c)) for head in heads], axis=0)
        carried = _dot(qb[rows, :], jnp.concatenate([c_state, n_state], axis=1).astype(BF16))
        nums.append(inter[rows, :] * carried[:, :ML_WIDTH] + _dot(s_cat[rows, :], v_heads))
        dens.append(inter[rows, :] * carried[:, ML_WIDTH:] + den_within[rows, :])
        update = _dot_tn(wk[rows, :], v_ones[rows, :])
        decay_c = decay[8 * c:8 * c + 1, :]
        c_state = decay_c * c_state + jnp.where(same_head, update[:, :ML_WIDTH], 0.0)
        n_state = decay_c * n_state + jnp.where(same_head, update[:, ML_WIDTH:], 0.0)
    c_ref[...] = c_state
    n_ref[...] = n_state

    hid = jnp.concatenate(nums, axis=0) / jnp.maximum(jnp.abs(jnp.concatenate(dens, axis=0)), floor)
    mu = sum(_dot(p, head_mean) for p in _split_bf16(hid, 2))
    dev = hid - mu
    var = sum(_dot(p, head_mean) for p in _split_bf16(dev * dev, 2))
    y = _sigmoid(og_ref[0]) * (dev * lax.rsqrt(var + LN_EPS) * normg_ref[...])
    y_ref[0] = y.astype(y_ref.dtype)


def _mlstm_call(q, k, v, og, gates, gates_t, gbias_row, gbias_col, norm_g):
    bsz, s_len, _ = q.shape
    nb = SEQ_BLOCKS_PER_STEP if (s_len // BLOCK) % SEQ_BLOCKS_PER_STEP == 0 else 1
    tok = lambda w: pl.BlockSpec((1, nb * BLOCK, w), lambda b, c: (b, c, 0))
    return pl.pallas_call(
        _mlstm_kernel,
        grid=(bsz, s_len // (nb * BLOCK)),
        in_specs=[tok(ML_WIDTH), tok(ML_WIDTH), tok(ML_WIDTH), tok(ML_WIDTH), tok(GATE_PAD),
                  pl.BlockSpec((2 * ML_HEADS, nb * BLOCK), lambda b, c: (0, b * (s_len // (nb * BLOCK)) + c)),
                  _resident(gbias_row.shape), _resident(gbias_col.shape), _resident(norm_g.shape)],
        out_specs=tok(ML_WIDTH),
        out_shape=jax.ShapeDtypeStruct((bsz, s_len, ML_WIDTH), BF16),
        scratch_shapes=[pltpu.VMEM((ML_WIDTH, ML_WIDTH), F32),
                        pltpu.VMEM((ML_WIDTH, ML_WIDTH), F32),
                        pltpu.VMEM((8, 128), F32)],
        compiler_params=_params("parallel", "arbitrary"),
        name="mlstm",
    )(q, k, v, og, gates, gates_t, gbias_row, gbias_col, norm_g)


def _kv_kernel(mem_ref, w_ref, o_ref):
    o_ref[...] = _dot(mem_ref[...].astype(BF16), w_ref[...].astype(BF16)).astype(o_ref.dtype)


def _kv_call(mem, w, layer):
    n = mem.shape[0]
    tm = min(256, n)
    return pl.pallas_call(
        _kv_kernel,
        grid=(n // tm,),
        in_specs=[pl.BlockSpec((tm, D_MODEL), lambda i: (i, 0)), _layer_slab(w.shape, layer)],
        out_specs=pl.BlockSpec((tm, 2 * D_MODEL), lambda i: (i, 0)),
        out_shape=jax.ShapeDtypeStruct((n, 2 * D_MODEL), BF16),
        compiler_params=_params("parallel"),
        name="xattn_kv_proj",
    )(mem, w)


def _mix_out_xattn_kernel(x_ref, ysb_ref, ysw_ref, yml_ref, wmix_ref, g1_ref, b1_ref,
                          kv_ref, wq_ref, wo_ref, g_ref, b_ref, o_ref):
    mix = (_dot(ysb_ref[0], wmix_ref[0:SB_WIDTH, :].astype(BF16))
           + _dot(ysw_ref[0], wmix_ref[SB_WIDTH:SB_WIDTH + SWA_WIDTH, :].astype(BF16))
           + _dot(yml_ref[0], wmix_ref[SB_WIDTH + SWA_WIDTH:, :].astype(BF16)))
    x = _layer_norm(ALPHA * x_ref[0] + mix, g1_ref[...], b1_ref[...])
    q = _dot(x.astype(BF16), wq_ref[...].astype(BF16)).astype(BF16)
    outs = []
    for h in range(XATTN_HEADS):
        lo = h * XATTN_HEAD_DIM
        kh = kv_ref[0, :, lo:lo + XATTN_HEAD_DIM]
        vh = kv_ref[0, :, D_MODEL + lo:D_MODEL + lo + XATTN_HEAD_DIM]
        s = _dot_nt(q[:, lo:lo + XATTN_HEAD_DIM], kh) * (XATTN_HEAD_DIM ** -0.5)
        p = jnp.exp(s - jnp.max(s, axis=-1, keepdims=True))
        denom = jnp.sum(p, axis=-1, keepdims=True)
        outs.append((_dot(p.astype(BF16), vh) / denom).astype(BF16))
    attn = jnp.concatenate(outs, axis=-1)
    y = ALPHA * x + _dot(attn, wo_ref[...].astype(BF16))
    o_ref[0] = _layer_norm(y, g_ref[...], b_ref[...])


def _mix_out_xattn_call(x, y_sb, y_sw, y_ml, w_mix, kv, w_q, w_o, layer, ln_g, ln_b):
    bsz, s_len, _ = x.shape
    tm = min(XATTN_TILE, s_len)
    tok = lambda w: pl.BlockSpec((1, tm, w), lambda bi, i: (bi, i, 0))
    ln = lambda t, k: _layer_slab(t.shape, 4 * layer + k)
    return pl.pallas_call(
        _mix_out_xattn_kernel,
        grid=(bsz, s_len // tm),
        in_specs=[tok(D_MODEL), tok(SB_WIDTH), tok(SWA_WIDTH), tok(ML_WIDTH),
                  _layer_slab(w_mix.shape, layer), ln(ln_g, 1), ln(ln_b, 1),
                  pl.BlockSpec((1, kv.shape[1], 2 * D_MODEL), lambda bi, i: (bi, 0, 0)),
                  _layer_slab(w_q.shape, layer), _layer_slab(w_o.shape, layer), ln(ln_g, 2), ln(ln_b, 2)],
        out_specs=tok(D_MODEL),
        out_shape=jax.ShapeDtypeStruct((bsz, s_len, D_MODEL), F32),
        compiler_params=_params("parallel", "parallel"),
        name="mix_out_xattn_ln",
    )(x, y_sb, y_sw, y_ml, w_mix, ln_g, ln_b, kv, w_q, w_o, ln_g, ln_b)


def _mixer_heads(x, w_in, conv_w, conv_b, i_bias, f_bias, norm_g, sinks, bias):
    bsz, s_len, _ = x.shape
    n = bsz * s_len
    x2 = x.reshape(n, D_MODEL)
    w_perm = jnp.concatenate(
        [w_in[:, :_ML_I], w_in[:, _ML_O:], w_in[:, _ML_I:_ML_O],
         jnp.zeros((D_MODEL, GATE_PAD - 2 * ML_HEADS), w_in.dtype)], axis=1).astype(BF16)
    sb_q, sb_k, sb_v, sw_q, sw_kv, ml_v, ml_o, gates, ml_q, ml_k, gates_t = _inproj_call(
        x2, w_perm, conv_w, conv_b[None, :], s_len)
    seq = lambda t: t.reshape(bsz, s_len, t.shape[-1])

    y_sb = _sb_call(seq(sb_q), seq(sb_k), seq(sb_v))
    y_sw = _swa_call(seq(sw_q), seq(sw_kv), bias, sinks)

    gate_bias = jnp.concatenate([i_bias, f_bias])
    gbias_row = jnp.pad(gate_bias, (0, GATE_PAD - 2 * ML_HEADS))[None, :]
    y_ml = _mlstm_call(seq(ml_q), seq(ml_k), seq(ml_v), seq(ml_o), seq(gates), gates_t,
                       gbias_row, gate_bias[:, None], norm_g[None, :])
    return y_sb, y_sw, y_ml


def kernel(x, mem, ffn1_w_in, ffn1_w_out, mix_w_in, ml_conv_w, ml_conv_b, ml_i_bias, ml_f_bias, ml_norm_g, swa_sinks, rel_bias, mix_w_out, xattn_w_q, xattn_w_kv, xattn_w_o, ffn2_w_in, ffn2_w_out, ln_g, ln_b):
    bsz, s_len, _ = x.shape
    n = bsz * s_len
    mem2 = mem.reshape(-1, D_MODEL)
    bias = _swa_bias(rel_bias)
    ln_g = ln_g.reshape(DEPTH * 4, 1, D_MODEL)
    ln_b = ln_b.reshape(DEPTH * 4, 1, D_MODEL)
    for l in range(DEPTH):
        x = _ffn_call(x.reshape(n, D_MODEL), ffn1_w_in, ffn1_w_out, l, ln_g, ln_b, 4 * l
                      ).reshape(bsz, s_len, D_MODEL)
        y_sb, y_sw, y_ml = _mixer_heads(x, mix_w_in[l], ml_conv_w[l], ml_conv_b[l], ml_i_bias[l], ml_f_bias[l],
                                        ml_norm_g[l], swa_sinks[l], bias)
        kv = _kv_call(mem2, xattn_w_kv, l).reshape(bsz, -1, 2 * D_MODEL)
        x = _mix_out_xattn_call(x, y_sb, y_sw, y_ml, mix_w_out, kv, xattn_w_q, xattn_w_o, l, ln_g, ln_b)
        x = _ffn_call(x.reshape(n, D_MODEL), ffn2_w_in, ffn2_w_out, l, ln_g, ln_b, 4 * l + 3
                      ).reshape(bsz, s_len, D_MODEL)
    return x
```

```python
import functools

import numpy as np
import jax
import jax.numpy as jnp
from jax import lax
from jax.experimental import pallas as pl
from jax.experimental.pallas import tpu as pltpu

F32 = jnp.float32
BF16 = jnp.bfloat16

D_MODEL = 1024
DEPTH = 2
HEAD_DIM = 64
SB_HEADS = 4
SWA_HEADS = 8
SWA_KV_HEADS = 2
SWA_GROUP = SWA_HEADS // SWA_KV_HEADS
ML_HEADS = 4
SB_WIDTH = SB_HEADS * HEAD_DIM
SWA_WIDTH = SWA_HEADS * HEAD_DIM
SWA_KV_WIDTH = SWA_KV_HEADS * HEAD_DIM
ML_WIDTH = ML_HEADS * HEAD_DIM
BLOCK = 128
CONV_WIDTH = 4
NUM_BUCKETS = 32
MAX_DISTANCE = 128
XATTN_HEADS = 4
XATTN_HEAD_DIM = D_MODEL // XATTN_HEADS
D_FF = 2816
ALPHA = (2 * DEPTH) ** 0.25
LN_EPS = 1e-5
NEG_BIG = -1e30
SB_SKIP_BELOW = -104.0

_ML_I, _ML_O = 2304, 2312
GATE_PAD = 128

VMEM_LIMIT = 56 * 1024 * 1024

SEQ_BLOCKS_PER_STEP = 16
INPROJ_TILE = 1024
XATTN_TILE = 1024
SWA_BLOCKS_PER_STEP = 4
SB_ROWS_PER_STEP = 4
TOKEN_TILE = 512
FF_CHUNK = 256


def _params(*sem):
    return pltpu.CompilerParams(dimension_semantics=sem, vmem_limit_bytes=VMEM_LIMIT)


def _resident(shape):
    nd = len(shape)
    return pl.BlockSpec(shape, lambda *_: (0,) * nd, pipeline_mode=pl.Buffered(1))


def _layer_slab(stacked_shape, index):
    rest = tuple(stacked_shape[1:])
    return pl.BlockSpec((None,) + rest, lambda *_: (index,) + (0,) * len(rest), pipeline_mode=pl.Buffered(1))


def _layer_norm(y, g, b):
    mu = jnp.mean(y, axis=-1, keepdims=True)
    d = y - mu
    var = jnp.mean(d * d, axis=-1, keepdims=True)
    return d * lax.rsqrt(var + LN_EPS) * g + b


def _log_sigmoid(x):
    return jnp.minimum(x, 0.0) - jnp.log(1.0 + jnp.exp(-jnp.abs(x)))


def _sigmoid(x):
    return 1.0 / (1.0 + jnp.exp(-x))


def _dot(a, b):
    return jnp.dot(a, b, preferred_element_type=F32)


def _dot_nt(a, b):
    return lax.dot_general(a, b, (((1,), (1,)), ((), ())), preferred_element_type=F32)


def _dot_tn(a, b):
    return lax.dot_general(a, b, (((0,), (0,)), ((), ())), preferred_element_type=F32)


def _split_bf16(x, pieces):
    out = []
    r = x
    for _ in range(pieces - 1):
        p = r.astype(BF16)
        out.append(p)
        r = r - p.astype(F32)
    out.append(r.astype(BF16))
    return out


def _ffn_kernel(x_ref, win_ref, wout_ref, g_ref, b_ref, o_ref, acc_ref):
    xb = x_ref[...].astype(BF16)
    for c in range(D_FF // FF_CHUNK):
        lo = c * FF_CHUNK
        a = _dot(xb, win_ref[:, lo:lo + FF_CHUNK].astype(BF16))
        b = _dot(xb, win_ref[:, D_FF + lo:D_FF + lo + FF_CHUNK].astype(BF16))
        h = (a * _sigmoid(a) * b).astype(BF16)
        part = _dot(h, wout_ref[lo:lo + FF_CHUNK, :].astype(BF16))
        if c == 0:
            acc_ref[...] = part
        else:
            acc_ref[...] += part
    y = ALPHA * x_ref[...] + 0.5 * acc_ref[...]
    o_ref[...] = _layer_norm(y, g_ref[...], b_ref[...])


def _ffn_call(x, w_in, w_out, layer, ln_g, ln_b, ln_index):
    n = x.shape[0]
    tm = min(TOKEN_TILE, n)
    row = lambda i: (i, 0)
    return pl.pallas_call(
        _ffn_kernel,
        grid=(n // tm,),
        in_specs=[pl.BlockSpec((tm, D_MODEL), row), _layer_slab(w_in.shape, layer), _layer_slab(w_out.shape, layer),
                  _layer_slab(ln_g.shape, ln_index), _layer_slab(ln_b.shape, ln_index)],
        out_specs=pl.BlockSpec((tm, D_MODEL), row),
        out_shape=jax.ShapeDtypeStruct((n, D_MODEL), F32),
        scratch_shapes=[pltpu.VMEM((tm, D_MODEL), F32)],
        compiler_params=_params("parallel"),
        name="ffn_ln",
    )(x, w_in, w_out, ln_g, ln_b)


_INPROJ_OUTS = (
    ("sb_q", 0, 256, BF16), ("sb_k", 256, 256, BF16), ("sb_v", 512, 256, BF16),
    ("sw_q", 768, 512, BF16), ("sw_kv", 1280, 256, BF16),
    ("ml_v", 2048, 256, BF16), ("ml_o", 2304, 256, F32), ("gates", 2560, GATE_PAD, F32),
)
_ML_QK_PERMUTED = 1536


def _inproj_kernel(tiles_per_seq, x_ref, w_ref, convw_ref, convb_ref, *refs):
    out_refs, (mlq_ref, mlk_ref, gates_t_ref, tail_ref) = refs[:len(_INPROJ_OUTS)], refs[len(_INPROJ_OUTS):]
    tm = x_ref.shape[0]
    xb = x_ref[...].astype(BF16)

    @pl.when(pl.program_id(0) % tiles_per_seq == 0)
    def _():
        tail_ref[...] = jnp.zeros_like(tail_ref)

    qk = _dot(xb, w_ref[:, _ML_QK_PERMUTED:_ML_QK_PERMUTED + 2 * ML_WIDTH])
    tail = tail_ref[...]
    tail_ref[...] = qk[tm - 8:, :]
    first_rows = lax.broadcasted_iota(jnp.int32, tail.shape, 0)
    conv = convb_ref[...] + convw_ref[CONV_WIDTH - 1:CONV_WIDTH, :] * qk
    for back in range(1, CONV_WIDTH):
        rolled = pltpu.roll(qk, back, axis=0)
        head = jnp.where(first_rows < back, pltpu.roll(tail, back, axis=0), rolled[0:8, :])
        shifted = jnp.concatenate([head, rolled[8:, :]], axis=0)
        conv = conv + convw_ref[CONV_WIDTH - 1 - back:CONV_WIDTH - back, :] * shifted
    act = conv * _sigmoid(conv)
    mlq_ref[...] = act[:, :ML_WIDTH].astype(mlq_ref.dtype)
    mlk_ref[...] = (act[:, ML_WIDTH:] * (HEAD_DIM ** -0.5)).astype(mlk_ref.dtype)

    for (name, lo, width, dt), o_ref in zip(_INPROJ_OUTS, out_refs):
        res = _dot(xb, w_ref[:, lo:lo + width])
        o_ref[...] = res.astype(dt)
        if name == "gates":
            gates_t_ref[...] = res.T[0:gates_t_ref.shape[0], :]


def _inproj_call(x, w, conv_w, conv_b, s_len):
    n = x.shape[0]
    tm = min(INPROJ_TILE, s_len)
    assert s_len % tm == 0
    row = lambda i: (i, 0)
    widths = [width for _, _, width, _ in _INPROJ_OUTS] + [ML_WIDTH, ML_WIDTH]
    dtypes = [dt for _, _, _, dt in _INPROJ_OUTS] + [BF16, BF16]
    return pl.pallas_call(
        functools.partial(_inproj_kernel, s_len // tm),
        grid=(n // tm,),
        in_specs=[pl.BlockSpec((tm, D_MODEL), row), _resident(w.shape), _resident(conv_w.shape),
                  _resident(conv_b.shape)],
        out_specs=[pl.BlockSpec((tm, width), row) for width in widths]
                  + [pl.BlockSpec((2 * ML_HEADS, tm), lambda i: (0, i))],
        out_shape=[jax.ShapeDtypeStruct((n, width), dt) for width, dt in zip(widths, dtypes)]
                  + [jax.ShapeDtypeStruct((2 * ML_HEADS, n), F32)],
        scratch_shapes=[pltpu.VMEM((8, 2 * ML_WIDTH), F32)],
        compiler_params=_params("arbitrary"),
        name="mix_in_proj",
    )(x, w, conv_w, conv_b)


def _sb_kernel(q_ref, k_ref, v_ref, o_ref, acc_ref, carry_ref):
    qi = pl.program_id(1)
    rows = q_ref.shape[0]
    lane_head = lax.broadcasted_iota(jnp.int32, (1, SB_WIDTH), 1) // HEAD_DIM
    heads = [lane_head == h for h in range(SB_HEADS)]
    q_stacks = []
    for bi in range(rows):
        q = (q_ref[bi].astype(F32) * (HEAD_DIM ** -0.5)).astype(BF16)
        q_stacks.append(jnp.concatenate([jnp.where(head, q, jnp.zeros_like(q)) for head in heads], axis=0))
    tall = rows * SB_HEADS * BLOCK
    strict = (lax.broadcasted_iota(jnp.int32, (tall, BLOCK), 1)
              < lax.broadcasted_iota(jnp.int32, (tall, BLOCK), 0) % BLOCK)
    row = lax.broadcasted_iota(jnp.int32, (2 * BLOCK, 2 * BLOCK), 0) % BLOCK
    col = lax.broadcasted_iota(jnp.int32, (2 * BLOCK, 2 * BLOCK), 1)
    suffix_and_total = jnp.where((row >= col) | (col >= BLOCK), 1.0, 0.0).astype(BF16)

    def step(j, diagonal):
        start = pl.multiple_of(j * BLOCK, BLOCK)
        z = jnp.concatenate([_dot_nt(q_stacks[bi], k_ref[bi, pl.ds(start, BLOCK), :]) for bi in range(rows)],
                            axis=0)
        lom = jnp.minimum(-z, 0.0) - jnp.log(1.0 + jnp.exp(-jnp.abs(z)))
        if diagonal:
            lom = jnp.where(strict, lom, 0.0)
        sums = _dot(jnp.concatenate(_split_bf16(lom, 2), axis=1), suffix_and_total)
        if diagonal:
            w = jnp.where(strict, jnp.exp(z + sums[:, :BLOCK]), 0.0)
            carry = sums[:, BLOCK:]
        else:
            carry = carry_ref[...]
            w = jnp.exp(z + sums[:, :BLOCK] + carry)
            carry = carry + sums[:, BLOCK:]
        carry_ref[...] = carry
        w = w.astype(BF16)
        for bi in range(rows):
            vb = v_ref[bi, pl.ds(start, BLOCK), :]
            v_heads = jnp.concatenate([jnp.where(head, vb, jnp.zeros_like(vb)) for head in heads], axis=0)
            tiles = [w[(bi * SB_HEADS + h) * BLOCK:(bi * SB_HEADS + h + 1) * BLOCK, :] for h in range(SB_HEADS)]
            pv = _dot(jnp.concatenate(tiles, axis=1), v_heads)
            if diagonal:
                acc_ref[bi] = pv
            else:
                acc_ref[bi] += pv
        return jnp.max(carry)

    def more_blocks(state):
        j, largest_carry = state
        return (j >= 0) & (largest_carry > SB_SKIP_BELOW)

    def next_block(state):
        j, _ = state
        return j - 1, step(j, False)

    lax.while_loop(more_blocks, next_block, (qi - 1, step(qi, True)))
    o_ref[...] = acc_ref[...].astype(o_ref.dtype)


def _sb_call(q, k, v):
    bsz, s_len, _ = q.shape
    rows = SB_ROWS_PER_STEP if bsz % SB_ROWS_PER_STEP == 0 else 1
    blk = pl.BlockSpec((rows, BLOCK, SB_WIDTH), lambda b, i: (b, i, 0))
    whole = pl.BlockSpec((rows, s_len, SB_WIDTH), lambda b, i: (b, 0, 0), pipeline_mode=pl.Buffered(1))
    return pl.pallas_call(
        _sb_kernel,
        grid=(bsz // rows, s_len // BLOCK),
        in_specs=[blk, whole, whole],
        out_specs=blk,
        out_shape=jax.ShapeDtypeStruct((bsz, s_len, SB_WIDTH), BF16),
        scratch_shapes=[pltpu.VMEM((rows, BLOCK, SB_WIDTH), F32),
                        pltpu.VMEM((rows * SB_HEADS * BLOCK, BLOCK), F32)],
        compiler_params=_params("parallel", "arbitrary"),
        name="stick_breaking",
    )(q, k, v)


def _swa_kernel(sink_rows_ref, q_ref, kvp_ref, kvc_ref, bias_ref, o_ref):
    first = jnp.minimum(pl.program_id(1), 1)
    for blk in range(q_ref.shape[1] // BLOCK):
        rows = slice(blk * BLOCK, (blk + 1) * BLOCK)
        q = (q_ref[0, rows, :].astype(F32) * (HEAD_DIM ** -0.5)).astype(BF16)
        if blk == 0:
            kv = jnp.concatenate([kvp_ref[0], kvc_ref[0, rows, :]], axis=0)
        else:
            kv = kvc_ref[0, (blk - 1) * BLOCK:(blk + 1) * BLOCK, :]
        outs = []
        for g in range(SWA_KV_HEADS):
            group = range(g * SWA_GROUP, (g + 1) * SWA_GROUP)
            q_stack = jnp.concatenate([q[:, h * HEAD_DIM:(h + 1) * HEAD_DIM] for h in group], axis=0)
            kg = kv[:, g * HEAD_DIM:(g + 1) * HEAD_DIM]
            vg = kv[:, SWA_KV_WIDTH + g * HEAD_DIM:SWA_KV_WIDTH + (g + 1) * HEAD_DIM]
            logits = _dot_nt(kg, q_stack) + (bias_ref[first, g] if blk == 0 else bias_ref[1, g])
            sink = sink_rows_ref[g]
            m = jnp.maximum(jnp.max(logits, axis=0, keepdims=True), sink)
            p = jnp.exp(logits - m)
            denom = jnp.sum(p, axis=0, keepdims=True) + jnp.exp(sink - m)
            o_t = _dot_tn(vg, p.astype(BF16)) / denom
            outs += [o_t[:, j * BLOCK:(j + 1) * BLOCK].T for j in range(SWA_GROUP)]
        o_ref[0, rows, :] = jnp.concatenate(outs, axis=-1).astype(o_ref.dtype)


def _swa_call(q, kv, bias, sinks):
    bsz, s_len, _ = q.shape
    nb = SWA_BLOCKS_PER_STEP if (s_len // BLOCK) % SWA_BLOCKS_PER_STEP == 0 else 1
    bias = bias.reshape(2, SWA_KV_HEADS, SWA_GROUP, BLOCK, 2 * BLOCK).transpose(0, 1, 4, 2, 3)
    bias = bias.reshape(2, SWA_KV_HEADS, 2 * BLOCK, SWA_GROUP * BLOCK)
    sink_rows = jnp.repeat(sinks.astype(F32), BLOCK).reshape(SWA_KV_HEADS, 1, SWA_GROUP * BLOCK)
    return pl.pallas_call(
        _swa_kernel,
        grid=(bsz, s_len // (nb * BLOCK)),
        in_specs=[_resident(sink_rows.shape),
                  pl.BlockSpec((1, nb * BLOCK, SWA_WIDTH), lambda b, i: (b, i, 0)),
                  pl.BlockSpec((1, BLOCK, 2 * SWA_KV_WIDTH), lambda b, i: (b, jnp.maximum(i * nb - 1, 0), 0)),
                  pl.BlockSpec((1, nb * BLOCK, 2 * SWA_KV_WIDTH), lambda b, i: (b, i, 0)),
                  _resident(bias.shape)],
        out_specs=pl.BlockSpec((1, nb * BLOCK, SWA_WIDTH), lambda b, i: (b, i, 0)),
        out_shape=jax.ShapeDtypeStruct((bsz, s_len, SWA_WIDTH), BF16),
        compiler_params=_params("parallel", "arbitrary"),
        name="sliding_window",
    )(sink_rows, q, kv, kv, bias)


def _t5_bucket(dist):
    max_exact = NUM_BUCKETS // 2
    d = np.maximum(dist, 1)
    large = max_exact + (np.log(d / max_exact) / np.log(MAX_DISTANCE / max_exact)
                         * (NUM_BUCKETS - max_exact)).astype(np.int32)
    large = np.minimum(large, NUM_BUCKETS - 1)
    return np.where(dist < max_exact, dist, large).astype(np.int32)


def _swa_bias(rel_bias):
    qi = np.arange(BLOCK)[:, None]
    kj = np.arange(2 * BLOCK)[None, :]
    dist = qi + BLOCK - kj
    bucket = jnp.asarray(_t5_bucket(np.clip(dist, 0, None)))[None]
    table = rel_bias.astype(F32)
    bias = jnp.zeros((SWA_HEADS, BLOCK, 2 * BLOCK), F32)
    for b in range(NUM_BUCKETS):
        bias = jnp.where(bucket == b, table[b][:, None, None], bias)
    in_window = (dist >= 0) & (dist < BLOCK)
    first = jnp.asarray(in_window & (kj >= BLOCK))[None]
    rest = jnp.asarray(in_window)[None]
    return jnp.stack([jnp.where(first, bias, NEG_BIG), jnp.where(rest, bias, NEG_BIG)])


def _mlstm_kernel(q_ref, k_ref, v_ref, og_ref, gates_ref, gates_t_ref,
                  gbias_row_ref, gbias_col_ref, normg_ref, y_ref,
                  c_ref, n_ref, m_ref):
    @pl.when(pl.program_id(1) == 0)
    def _():
        c_ref[...] = jnp.zeros_like(c_ref)
        n_ref[...] = jnp.zeros_like(n_ref)
        m_ref[...] = jnp.zeros_like(m_ref)

    iota = lambda shape, axis: lax.broadcasted_iota(jnp.int32, shape, axis)
    row = iota((BLOCK, BLOCK), 0)
    col = iota((BLOCK, BLOCK), 1)
    causal = col <= row
    one_where = lambda cond: jnp.where(cond, 1.0, 0.0).astype(BF16)
    tri = one_where(causal)
    tri_t = one_where(row <= col)
    heads = [iota((1, ML_WIDTH), 1) // HEAD_DIM == h for h in range(ML_HEADS)]
    same_head = iota((ML_WIDTH, ML_WIDTH), 0) // HEAD_DIM == iota((ML_WIDTH, ML_WIDTH), 1) // HEAD_DIM
    head_mean = jnp.where(same_head, 1.0 / HEAD_DIM, 0.0).astype(BF16)
    sel_tile = one_where(iota((BLOCK, ML_HEADS * BLOCK), 0) == iota((BLOCK, ML_HEADS * BLOCK), 1) // BLOCK)
    sel_head = one_where(iota((BLOCK, ML_WIDTH), 0) == iota((BLOCK, ML_WIDTH), 1) // HEAD_DIM)
    rows_of_head = one_where(iota((ML_HEADS * BLOCK, ML_WIDTH), 0) // BLOCK
                             == iota((ML_HEADS * BLOCK, ML_WIDTH), 1) // HEAD_DIM)

    def spread(tile, selector):
        live = iota(tile.shape, 1) < ML_HEADS
        return sum(_dot(p, selector) for p in _split_bf16(jnp.where(live, tile, 0.0), 2))

    total = q_ref.shape[1]
    chunks = [slice(c * BLOCK, (c + 1) * BLOCK) for c in range(total // BLOCK)]
    row_in_chunk = iota((total, BLOCK), 0) % BLOCK
    causal_heads = iota((ML_HEADS * BLOCK, BLOCK), 1) <= iota((ML_HEADS * BLOCK, BLOCK), 0) % BLOCK
    qb = q_ref[0]
    kb = k_ref[0]
    v_all = v_ref[0]

    gates = gates_ref[0] + gbias_row_ref[...]
    gates_t = gates_t_ref[...] + gbias_col_ref[...]
    lsf = _split_bf16(_log_sigmoid(gates), 3)
    b_cols = jnp.concatenate([sum(_dot(tri, p[rows, :]) for p in lsf) for rows in chunks], axis=0)
    lsf_t = _log_sigmoid(gates_t)
    lsf_t = jnp.concatenate([lsf_t[:, rows] for rows in chunks], axis=0)
    b_rows = sum(_dot(p, tri_t) for p in _split_bf16(lsf_t, 3))
    b_tile = pltpu.roll(b_cols, BLOCK - ML_HEADS, axis=1)
    a_tile = gates - b_tile

    run_max = a_tile
    shift = 1
    while shift < BLOCK:
        run_max = jnp.where(row_in_chunk >= shift,
                            jnp.maximum(run_max, pltpu.roll(run_max, shift, axis=0)), run_max)
        shift *= 2

    m_prev = m_ref[0:1, :]
    m_prevs, m_news, decays, w_logs = [], [], [], []
    for c, rows in enumerate(chunks):
        b_end = b_tile[(c + 1) * BLOCK - 1:(c + 1) * BLOCK, :]
        w_log = b_end - b_tile[rows, :] + gates[rows, :]
        m_new = jnp.maximum(b_end + m_prev, jnp.max(w_log, axis=0, keepdims=True))
        decays.append(jnp.broadcast_to(jnp.exp(b_end + m_prev - m_new), (8, BLOCK)))
        m_prevs.append(jnp.broadcast_to(m_prev, (BLOCK, BLOCK)))
        m_news.append(jnp.broadcast_to(m_new, (BLOCK, BLOCK)))
        w_logs.append(w_log)
        m_prev = m_new
    m_ref[...] = jnp.broadcast_to(m_prev, m_ref.shape)
    m_before = jnp.concatenate(m_prevs, axis=0)
    u = jnp.maximum(m_before, run_max)
    neg_u = spread(-u, sel_tile)
    inter = spread(jnp.exp(m_before - u), sel_head)
    floor = spread(jnp.exp(-(b_tile + u)), sel_head)
    w = spread(jnp.exp(jnp.concatenate(w_logs, axis=0) - jnp.concatenate(m_news, axis=0)), sel_head)
    decay = spread(jnp.concatenate(decays, axis=0), sel_head)

    s_cat = []
    for c, rows in enumerate(chunks):
        q_c = qb[rows, :]
        q_stack = jnp.concatenate([jnp.where(head, q_c, jnp.zeros_like(q_c)) for head in heads], axis=0)
        a_rows = gates_t[0:ML_HEADS, rows] - b_rows[8 * c + ML_HEADS:8 * c + 2 * ML_HEADS, :]
        log_d = jnp.concatenate([a_rows[h:h + 1, :] + neg_u[rows, h * BLOCK:(h + 1) * BLOCK]
                                 for h in range(ML_HEADS)], axis=0)
        s = (_dot_nt(q_stack, kb[rows, :]) * jnp.exp(jnp.where(causal_heads, log_d, NEG_BIG))).astype(BF16)
        s_cat.append(jnp.concatenate([s[h * BLOCK:(h + 1) * BLOCK, :] for h in range(ML_HEADS)], axis=1))
    s_cat = jnp.concatenate(s_cat, axis=0)
    den_within = _dot(s_cat, rows_of_head)
    wk = (w * kb.astype(F32)).astype(BF16)
    v_ones = jnp.concatenate([v_all, jnp.ones_like(v_all)], axis=1)

    c_state = c_ref[...]
    n_state = n_ref[...]
    nums, dens = [], []
    for c, rows in enumerate(chunks):
        v_c = v_all[rows, :]
        v_heads = jnp.concatenate([jnp.where(head, v_c, jnp.zeros_like(v_c)) for head in heads], axis=0)
        carried = _dot(qb[rows, :], jnp.concatenate([c_state, n_state], axis=1).astype(BF16))
        nums.append(inter[rows, :] * carried[:, :ML_WIDTH] + _dot(s_cat[rows, :], v_heads))
        dens.append(inter[rows, :] * carried[:, ML_WIDTH:] + den_within[rows, :])
        update = _dot_tn(wk[rows, :], v_ones[rows, :])
        decay_c = decay[8 * c:8 * c + 1, :]
        c_state = decay_c * c_state + jnp.where(same_head, update[:, :ML_WIDTH], 0.0)
        n_state = decay_c * n_state + jnp.where(same_head, update[:, ML_WIDTH:], 0.0)
    c_ref[...] = c_state
    n_ref[...] = n_state

    hid = jnp.concatenate(nums, axis=0) / jnp.maximum(jnp.abs(jnp.concatenate(dens, axis=0)), floor)
    mu = sum(_dot(p, head_mean) for p in _split_bf16(hid, 2))
    dev = hid - mu
    var = sum(_dot(p, head_mean) for p in _split_bf16(dev * dev, 2))
    y = _sigmoid(og_ref[0]) * (dev * lax.rsqrt(var + LN_EPS) * normg_ref[...])
    y_ref[0] = y.astype(y_ref.dtype)


def _mlstm_call(q, k, v, og, gates, gates_t, gbias_row, gbias_col, norm_g):
    bsz, s_len, _ = q.shape
    nb = SEQ_BLOCKS_PER_STEP if (s_len // BLOCK) % SEQ_BLOCKS_PER_STEP == 0 else 1
    tok = lambda w: pl.BlockSpec((1, nb * BLOCK, w), lambda b, c: (b, c, 0))
    return pl.pallas_call(
        _mlstm_kernel,
        grid=(bsz, s_len // (nb * BLOCK)),
        in_specs=[tok(ML_WIDTH), tok(ML_WIDTH), tok(ML_WIDTH), tok(ML_WIDTH), tok(GATE_PAD),
                  pl.BlockSpec((2 * ML_HEADS, nb * BLOCK), lambda b, c: (0, b * (s_len // (nb * BLOCK)) + c)),
                  _resident(gbias_row.shape), _resident(gbias_col.shape), _resident(norm_g.shape)],
        out_specs=tok(ML_WIDTH),
        out_shape=jax.ShapeDtypeStruct((bsz, s_len, ML_WIDTH), BF16),
        scratch_shapes=[pltpu.VMEM((ML_WIDTH, ML_WIDTH), F32),
                        pltpu.VMEM((ML_WIDTH, ML_WIDTH), F32),
                        pltpu.VMEM((8, 128), F32)],
        compiler_params=_params("parallel", "arbitrary"),
        name="mlstm",
    )(q, k, v, og, gates, gates_t, gbias_row, gbias_col, norm_g)


def _kv_kernel(mem_ref, w_ref, o_ref):
    o_ref[...] = _dot(mem_ref[...].astype(BF16), w_ref[...].astype(BF16)).astype(o_ref.dtype)


def _kv_call(mem, w, layer):
    n = mem.shape[0]
    tm = min(256, n)
    return pl.pallas_call(
        _kv_kernel,
        grid=(n // tm,),
        in_specs=[pl.BlockSpec((tm, D_MODEL), lambda i: (i, 0)), _layer_slab(w.shape, layer)],
        out_specs=pl.BlockSpec((tm, 2 * D_MODEL), lambda i: (i, 0)),
        out_shape=jax.ShapeDtypeStruct((n, 2 * D_MODEL), BF16),
        compiler_params=_params("parallel"),
        name="xattn_kv_proj",
    )(mem, w)


def _mix_out_xattn_kernel(x_ref, ysb_ref, ysw_ref, yml_ref, wmix_ref, g1_ref, b1_ref,
                          kv_ref, wq_ref, wo_ref, g_ref, b_ref, o_ref):
    mix = (_dot(ysb_ref[0], wmix_ref[0:SB_WIDTH, :].astype(BF16))
           + _dot(ysw_ref[0], wmix_ref[SB_WIDTH:SB_WIDTH + SWA_WIDTH, :].astype(BF16))
           + _dot(yml_ref[0], wmix_ref[SB_WIDTH + SWA_WIDTH:, :].astype(BF16)))
    x = _layer_norm(ALPHA * x_ref[0] + mix, g1_ref[...], b1_ref[...])
    q = _dot(x.astype(BF16), wq_ref[...].astype(BF16)).astype(BF16)
    outs = []
    for h in range(XATTN_HEADS):
        lo = h * XATTN_HEAD_DIM
        kh = kv_ref[0, :, lo:lo + XATTN_HEAD_DIM]
        vh = kv_ref[0, :, D_MODEL + lo:D_MODEL + lo + XATTN_HEAD_DIM]
        s = _dot_nt(q[:, lo:lo + XATTN_HEAD_DIM], kh) * (XATTN_HEAD_DIM ** -0.5)
        p = jnp.exp(s - jnp.max(s, axis=-1, keepdims=True))
        denom = jnp.sum(p, axis=-1, keepdims=True)
        outs.append((_dot(p.astype(BF16), vh) / denom).astype(BF16))
    attn = jnp.concatenate(outs, axis=-1)
    y = ALPHA * x + _dot(attn, wo_ref[...].astype(BF16))
    o_ref[0] = _layer_norm(y, g_ref[...], b_ref[...])


def _mix_out_xattn_call(x, y_sb, y_sw, y_ml, w_mix, kv, w_q, w_o, layer, ln_g, ln_b):
    bsz, s_len, _ = x.shape
    tm = min(XATTN_TILE, s_len)
    tok = lambda w: pl.BlockSpec((1, tm, w), lambda bi, i: (bi, i, 0))
    ln = lambda t, k: _layer_slab(t.shape, 4 * layer + k)
    return pl.pallas_call(
        _mix_out_xattn_kernel,
        grid=(bsz, s_len // tm),
        in_specs=[tok(D_MODEL), tok(SB_WIDTH), tok(SWA_WIDTH), tok(ML_WIDTH),
                  _layer_slab(w_mix.shape, layer), ln(ln_g, 1), ln(ln_b, 1),
                  pl.BlockSpec((1, kv.shape[1], 2 * D_MODEL), lambda bi, i: (bi, 0, 0)),
                  _layer_slab(w_q.shape, layer), _layer_slab(w_o.shape, layer), ln(ln_g, 2), ln(ln_b, 2)],
        out_specs=tok(D_MODEL),
        out_shape=jax.ShapeDtypeStruct((bsz, s_len, D_MODEL), F32),
        compiler_params=_params("parallel", "parallel"),
        name="mix_out_xattn_ln",
    )(x, y_sb, y_sw, y_ml, w_mix, ln_g, ln_b, kv, w_q, w_o, ln_g, ln_b)


def _mixer_heads(x, w_in, conv_w, conv_b, i_bias, f_bias, norm_g, sinks, bias):
    bsz, s_len, _ = x.shape
    n = bsz * s_len
    x2 = x.reshape(n, D_MODEL)
    w_perm = jnp.concatenate(
        [w_in[:, :_ML_I], w_in[:, _ML_O:], w_in[:, _ML_I:_ML_O],
         jnp.zeros((D_MODEL, GATE_PAD - 2 * ML_HEADS), w_in.dtype)], axis=1).astype(BF16)
    sb_q, sb_k, sb_v, sw_q, sw_kv, ml_v, ml_o, gates, ml_q, ml_k, gates_t = _inproj_call(
        x2, w_perm, conv_w, conv_b[None, :], s_len)
    seq = lambda t: t.reshape(bsz, s_len, t.shape[-1])

    y_sb = _sb_call(seq(sb_q), seq(sb_k), seq(sb_v))
    y_sw = _swa_call(seq(sw_q), seq(sw_kv), bias, sinks)

    gate_bias = jnp.concatenate([i_bias, f_bias])
    gbias_row = jnp.pad(gate_bias, (0, GATE_PAD - 2 * ML_HEADS))[None, :]
    y_ml = _mlstm_call(seq(ml_q), seq(ml_k), seq(ml_v), seq(ml_o), seq(gates), gates_t,
                       gbias_row, gate_bias[:, None], norm_g[None, :])
    return y_sb, y_sw, y_ml


def kernel(x, mem, ffn1_w_in, ffn1_w_out, mix_w_in, ml_conv_w, ml_conv_b, ml_i_bias, ml_f_bias, ml_norm_g, swa_sinks, rel_bias, mix_w_out, xattn_w_q, xattn_w_kv, xattn_w_o, ffn2_w_in, ffn2_w_out, ln_g, ln_b):
    bsz, s_len, _ = x.shape
    n = bsz * s_len
    mem2 = mem.reshape(-1, D_MODEL)
    bias = _swa_bias(rel_bias)
    ln_g = ln_g.reshape(DEPTH * 4, 1, D_MODEL)
    ln_b = ln_b.reshape(DEPTH * 4, 1, D_MODEL)
    for l in range(DEPTH):
        x = _ffn_call(x.reshape(n, D_MODEL), ffn1_w_in, ffn1_w_out, l, ln_g, ln_b, 4 * l
                      ).reshape(bsz, s_len, D_MODEL)
        y_sb, y_sw, y_ml = _mixer_heads(x, mix_w_in[l], ml_conv_w[l], ml_conv_b[l], ml_i_bias[l], ml_f_bias[l],
                                        ml_norm_g[l], swa_sinks[l], bias)
        kv = _kv_call(mem2, xattn_w_kv, l).reshape(bsz, -1, 2 * D_MODEL)
        x = _mix_out_xattn_call(x, y_sb, y_sw, y_ml, mix_w_out, kv, xattn_w_q, xattn_w_o, l, ln_g, ln_b)
        x = _ffn_call(x.reshape(n, D_MODEL), ffn2_w_in, ffn2_w_out, l, ln_g, ln_b, 4 * l + 3
                      ).reshape(bsz, s_len, D_MODEL)
    return x
```
